```python
import jax, jax.numpy as jnp
from jax import lax
import numpy as np

D_MODEL = 1024
BATCH = 8
SEQ = 8192
DEPTH = 1
DEC_BATCH = 2
DEC_SEQ = 8192
PAST_LEN = 128

HEAD_DIM = 64
HEADS_PER_GROUP = 4
DILATED_GROUPS = ((128, 1), (512, 4), (2048, 16))
N_ATT_GROUPS = len(DILATED_GROUPS)
N_ATT_HEADS = N_ATT_GROUPS * HEADS_PER_GROUP
D_ATT = N_ATT_HEADS * HEAD_DIM
D_ATT_OUT = HEADS_PER_GROUP * HEAD_DIM
ROPE_THETA = 10000.0
D_CONV = 768
CONV_WIDTH = 3
N_IN = 3 * D_ATT + 3 * D_CONV + 2 * D_MODEL
N_EXPERT_GROUPS = 4
EXPERTS_PER_GROUP = 8
N_EXPERTS = N_EXPERT_GROUPS * EXPERTS_PER_GROUP
TOP_K_IN_GROUP = 2
D_EXPERT = 512
MOE_BLOCK = 256
RMS_EPS = 1e-6
NEG_INF = -1e30

kernel_name = 'hybrid_dilated_attn_shortconv_hmoe_encoder'


def rmsnorm(x, g):
    xf = x.astype(jnp.float32)
    y = xf * lax.rsqrt(jnp.mean(xf * xf, axis=-1, keepdims=True) + RMS_EPS)
    return (y * g.astype(jnp.float32)).astype(x.dtype)


def rotary_tables(S):
    inv_freq = 1.0 / (ROPE_THETA ** (jnp.arange(0, HEAD_DIM, 2, dtype=jnp.float32) / HEAD_DIM))
    ang = jnp.arange(S, dtype=jnp.float32)[:, None] * inv_freq[None, :]
    return jnp.cos(ang), jnp.sin(ang)


def rotary(x, cos, sin):
    x1, x2 = jnp.split(x.astype(jnp.float32), 2, axis=-1)
    c = cos[None, :, None, :]
    s = sin[None, :, None, :]
    return jnp.concatenate([x1 * c - x2 * s, x2 * c + x1 * s], axis=-1).astype(x.dtype)


def banded_attention(q, k, v, half):
    N, H, L, dh = q.shape
    nb = -(-L // half)
    Lp = nb * half
    qb = jnp.pad(q, ((0, 0), (0, 0), (0, Lp - L), (0, 0))).reshape(N, H, nb, half, dh)

    def windows(t):
        tb = jnp.pad(t, ((0, 0), (0, 0), (half, Lp - L + half), (0, 0))).reshape(N, H, nb + 2, half, dh)
        return jnp.concatenate([tb[:, :, :-2], tb[:, :, 1:-1], tb[:, :, 2:]], axis=3)

    kw = windows(k)
    vw = windows(v)
    qpos = jnp.arange(Lp).reshape(nb, half)
    kpos = jnp.arange(nb)[:, None] * half - half + jnp.arange(3 * half)[None, :]
    mask = (jnp.abs(kpos[:, None, :] - qpos[:, :, None]) <= half) & ((kpos >= 0) & (kpos < L))[:, None, :]
    s = jnp.einsum('nhbqd,nhbkd->nhbqk', qb, kw, preferred_element_type=jnp.float32) * (dh ** -0.5)
    s = jnp.where(mask, s, NEG_INF)
    m = jnp.max(s, axis=-1, keepdims=True)
    p = jnp.exp(s - m)
    l = jnp.sum(p, axis=-1)
    o = jnp.einsum('nhbqk,nhbkd->nhbqd', p, vw.astype(jnp.float32)) / l[..., None]
    lse = m[..., 0] + jnp.log(l)
    return o.reshape(N, H, Lp, dh)[:, :, :L], lse.reshape(N, H, Lp)[:, :, :L]


def dilated_attention(q, k, v, window, dilation):
    B, S, H, dh = q.shape
    L = S // dilation
    half = window // 2 // dilation

    def to_sub(t):
        return t.reshape(B, L, dilation, H, dh).transpose(0, 2, 3, 1, 4).reshape(B * dilation, H, L, dh)

    o, lse = banded_attention(to_sub(q), to_sub(k), to_sub(v), half)
    o = o.reshape(B, dilation, H, L, dh).transpose(0, 3, 1, 2, 4).reshape(B, S, H, dh)
    lse = lse.reshape(B, dilation, H, L).transpose(0, 3, 1, 2).reshape(B, S, H)
    return o, lse


def short_conv(u, w):
    up = jnp.pad(u, ((0, 0), (1, 1), (0, 0)))
    return up[:, :-2] * w[0] + up[:, 1:-1] * w[1] + up[:, 2:] * w[2]


def hier_moe(h, w_rg, b_rg, w_re, b_re, w_g, w_u, w_d):
    B, S, D = h.shape
    T = B * S
    xt = h.reshape(T, D)
    grp_logits = (xt @ w_rg + b_rg).astype(jnp.float32)
    grp_prob = jax.nn.softmax(grp_logits, axis=-1)
    grp = jnp.argmax(grp_logits, axis=-1)
    grp_w = jnp.take_along_axis(grp_prob, grp[:, None], axis=-1)
    exp_logits = (xt @ w_re + b_re).astype(jnp.float32).reshape(T, N_EXPERT_GROUPS, EXPERTS_PER_GROUP)
    in_grp = jnp.take_along_axis(exp_logits, grp[:, None, None], axis=1)[:, 0]
    top_v, top_i = lax.top_k(in_grp, TOP_K_IN_GROUP)
    gate = grp_w * jax.nn.softmax(top_v, axis=-1)
    eid = grp[:, None].astype(jnp.int32) * EXPERTS_PER_GROUP + top_i.astype(jnp.int32)
    A = T * TOP_K_IN_GROUP
    flat_e = eid.reshape(A)
    flat_tok = jnp.repeat(jnp.arange(T, dtype=jnp.int32), TOP_K_IN_GROUP)
    flat_w = gate.reshape(A)
    order = jnp.argsort(flat_e)
    se, stok, sw = flat_e[order], flat_tok[order], flat_w[order]
    counts = jnp.bincount(flat_e, length=N_EXPERTS)
    starts = jnp.cumsum(counts) - counts
    pcounts = (counts + MOE_BLOCK - 1) // MOE_BLOCK * MOE_BLOCK
    pends = jnp.cumsum(pcounts)
    pstarts = pends - pcounts
    dest = pstarts[se] + jnp.arange(A) - starts[se]
    n_blocks = -(-A // MOE_BLOCK) + N_EXPERTS
    buf_tok = jnp.full((n_blocks * MOE_BLOCK,), T, jnp.int32).at[dest].set(stok)
    buf_w = jnp.zeros((n_blocks * MOE_BLOCK,), jnp.float32).at[dest].set(sw)
    block_e = jnp.minimum(jnp.searchsorted(pends, jnp.arange(n_blocks) * MOE_BLOCK, side='right'), N_EXPERTS - 1)
    x_pad = jnp.concatenate([xt, jnp.zeros((1, D), xt.dtype)], axis=0)
    xb = x_pad[buf_tok].reshape(n_blocks, MOE_BLOCK, D)

    def expert_block(args):
        xblk, e = args
        a = xblk @ w_g[e]
        u = xblk @ w_u[e]
        return (jax.nn.silu(a) * u) @ w_d[e]

    yb = lax.map(expert_block, (xb, block_e))
    y = jnp.zeros((T + 1, D), jnp.float32).at[buf_tok].add(yb.reshape(-1, D).astype(jnp.float32) * buf_w[:, None])
    return y[:T].reshape(B, S, D).astype(h.dtype)


def encoder_layer(x, n1, w_in, b_gate, conv_w, w_ao, w_co, w_o, n2, w_rg, b_rg, w_re, b_re, w_g, w_u, w_d):
    B, S, _ = x.shape
    h = rmsnorm(x, n1)
    z = h @ w_in
    sizes = [D_ATT] * 3 + [D_CONV] * 3 + [D_MODEL] * 2
    q, k, v, cu, cb, cc, za, zb = jnp.split(z, list(np.cumsum(sizes)[:-1]), axis=-1)
    cos, sin = rotary_tables(S)
    q = rotary(q.reshape(B, S, N_ATT_HEADS, HEAD_DIM), cos, sin)
    k = rotary(k.reshape(B, S, N_ATT_HEADS, HEAD_DIM), cos, sin)
    v = v.reshape(B, S, N_ATT_HEADS, HEAD_DIM)
    outs, lses = [], []
    for g, (window, dilation) in enumerate(DILATED_GROUPS):
        sl = slice(g * HEADS_PER_GROUP, (g + 1) * HEADS_PER_GROUP)
        o_g, l_g = dilated_attention(q[:, :, sl], k[:, :, sl], v[:, :, sl], window, dilation)
        outs.append(o_g)
        lses.append(l_g)
    alpha = jax.nn.softmax(jnp.stack(lses, axis=0), axis=0)
    o_att = jnp.sum(alpha[..., None] * jnp.stack(outs, axis=0), axis=0)
    att_branch = o_att.reshape(B, S, D_ATT_OUT).astype(x.dtype) @ w_ao
    conv_branch = (cb * short_conv(cc * cu, conv_w)) @ w_co
    g_a = jax.nn.sigmoid((za + b_gate[:D_MODEL]).astype(jnp.float32))
    g_b = jax.nn.sigmoid((zb + b_gate[D_MODEL:]).astype(jnp.float32))
    merged = (g_a * att_branch.astype(jnp.float32) + g_b * conv_branch.astype(jnp.float32)).astype(x.dtype)
    x = x + merged @ w_o
    x = x + hier_moe(rmsnorm(x, n2), w_rg, b_rg, w_re, b_re, w_g, w_u, w_d)
    return x


def setup_inputs(seed: int = 0) -> dict:
    key = jax.random.key(seed)
    ks = jax.random.split(key, 18)

    def nrm(k, shape, scale):
        return jax.random.normal(k, shape, jnp.float32) * scale

    return {
        'x_prompt': nrm(ks[0], (BATCH, SEQ, D_MODEL), 1.0),
        'x_sample': nrm(ks[1], (DEC_BATCH, DEC_SEQ, D_MODEL), 1.0),
        'norm1_g': 1.0 + nrm(ks[2], (DEPTH, D_MODEL), 0.05),
        'w_in': nrm(ks[3], (DEPTH, D_MODEL, N_IN), D_MODEL ** -0.5),
        'b_gate': nrm(ks[4], (DEPTH, 2 * D_MODEL), 0.1),
        'conv_w': nrm(ks[5], (DEPTH, CONV_WIDTH, D_CONV), CONV_WIDTH ** -0.5),
        'w_attn_out': nrm(ks[6], (DEPTH, D_ATT_OUT, D_MODEL), D_ATT_OUT ** -0.5),
        'w_conv_out': nrm(ks[7], (DEPTH, D_CONV, D_MODEL), D_CONV ** -0.5),
        'w_out': nrm(ks[8], (DEPTH, D_MODEL, D_MODEL), D_MODEL ** -0.5),
        'norm2_g': 1.0 + nrm(ks[9], (DEPTH, D_MODEL), 0.05),
        'w_router_group': nrm(ks[10], (DEPTH, D_MODEL, N_EXPERT_GROUPS), D_MODEL ** -0.5),
        'b_router_group': nrm(ks[11], (DEPTH, N_EXPERT_GROUPS), 0.01),
        'w_router_expert': nrm(ks[12], (DEPTH, D_MODEL, N_EXPERTS), D_MODEL ** -0.5),
        'b_router_expert': nrm(ks[13], (DEPTH, N_EXPERTS), 0.01),
        'w_exp_gate': nrm(ks[14], (DEPTH, N_EXPERTS, D_MODEL, D_EXPERT), D_MODEL ** -0.5),
        'w_exp_up': nrm(ks[15], (DEPTH, N_EXPERTS, D_MODEL, D_EXPERT), D_MODEL ** -0.5),
        'w_exp_down': nrm(ks[16], (DEPTH, N_EXPERTS, D_EXPERT, D_MODEL), D_EXPERT ** -0.5),
        'norm_f_g': 1.0 + nrm(ks[17], (D_MODEL,), 0.05),
    }


def reference(x_prompt, x_sample, norm1_g, w_in, b_gate, conv_w, w_attn_out, w_conv_out, w_out, norm2_g, w_router_group, b_router_group, w_router_expert, b_router_expert, w_exp_gate, w_exp_up, w_exp_down, norm_f_g):
    def trunk(x):
        for l in range(DEPTH):
            x = encoder_layer(x, norm1_g[l], w_in[l], b_gate[l], conv_w[l], w_attn_out[l], w_conv_out[l], w_out[l],
                              norm2_g[l], w_router_group[l], b_router_group[l], w_router_expert[l], b_router_expert[l],
                              w_exp_gate[l], w_exp_up[l], w_exp_down[l])
        return rmsnorm(x, norm_f_g)

    y_prompt = trunk(x_prompt)
    y_sample = trunk(x_sample)
    return (y_prompt, y_sample)
```

```python
import functools

import jax
import jax.numpy as jnp
from jax import lax
from jax.experimental import pallas as pl
from jax.experimental.pallas import tpu as pltpu

D_MODEL = 1024
HEAD_DIM = 64
HEADS_PER_GROUP = 4
DILATIONS = (1, 4, 16)
HALF = 64
D_GRP = HEADS_PER_GROUP * HEAD_DIM
D_ATT = 3 * D_GRP
D_CONV = 768
N_EXPERT_GROUPS = 4
EXPERTS_PER_GROUP = 8
N_EXPERTS = 32
D_EXPERT = 512
RMS_EPS = 1e-6
NEG = -1e30
ROPE_THETA = 10000.0

C_Q, C_K, C_V = 0, D_ATT, 2 * D_ATT
C_CU, C_CB, C_CC = 3 * D_ATT, 3 * D_ATT + D_CONV, 3 * D_ATT + 2 * D_CONV
C_ZA = 3 * D_ATT + 3 * D_CONV
C_ZB = C_ZA + D_MODEL

TS = 512
TQ = 2048
QB = 128
KW = QB + 2 * HALF
MOE_BLK = 256
TF = 256
ROUTER_ROWS = 48
VMEM_LIMIT = 56 * 1024 * 1024

F32 = jnp.float32
BF16 = jnp.bfloat16


def _rms(xf, g):
    return xf * lax.rsqrt(jnp.mean(xf * xf, axis=-1, keepdims=True) + RMS_EPS) * g


def _sigmoid(x):
    return 1.0 / (1.0 + jnp.exp(-x))


def _in_proj_kernel(x_ref, xp_ref, xn_ref, n1_ref, w_ref, bg_ref, cw_ref, wco_ref,
                    cos_ref, sa_ref, sb_ref,
                    q1, k1, v1, q2, k2, v2, q3, k3, v3, ga_ref, cp_ref,
                    pbuf, dbuf):
    j = pl.program_id(1)
    nj = pl.num_programs(1)
    g1 = n1_ref[...]
    h = _rms(x_ref[0], g1).astype(BF16)

    def proj(hh, c0, width):
        return jnp.dot(hh, w_ref[:, c0:c0 + width], preferred_element_type=F32)

    cosv = cos_ref[...]
    sav = sa_ref[...]
    sbv = sb_ref[...]

    def rotary(z):
        return z * cosv + pltpu.roll(z, 96, 1) * sav + pltpu.roll(z, 32, 1) * sbv

    def emit(outs, c0, rot, scale):
        for g, d in enumerate(DILATIONS):
            z = proj(h, c0 + g * D_GRP, D_GRP)
            if rot:
                z = jnp.concatenate([rotary(z[:, :128]), rotary(z[:, 128:])], axis=1)
            if scale != 1.0:
                z = z * scale
            if d == 1:
                outs[g][0, 0] = z.astype(BF16)
            else:
                dbuf[0] = z[:, :128]
                dbuf[1] = z[:, 128:]
                for r in range(d):
                    for c in range(2):
                        outs[g][0, r, :, c * 128:(c + 1) * 128] = (
                            dbuf[c, pl.ds(r, TS // d, stride=d), :].astype(BF16))

    emit((q1, q2, q3), C_Q, True, HEAD_DIM ** -0.5)
    emit((k1, k2, k3), C_K, True, 1.0)
    emit((v1, v2, v3), C_V, False, 1.0)

    p = proj(h, C_CC, D_CONV) * proj(h, C_CU, D_CONV)
    pbuf[8:8 + TS, :] = p
    hp = _rms(xp_ref[0], g1).astype(BF16)
    pp = proj(hp, C_CC, D_CONV) * proj(hp, C_CU, D_CONV)
    pbuf[0:8, :] = jnp.where(j > 0, pp, 0.0)
    hn = _rms(xn_ref[0], g1).astype(BF16)
    pn = proj(hn, C_CC, D_CONV) * proj(hn, C_CU, D_CONV)
    pbuf[8 + TS:16 + TS, :] = jnp.where(j < nj - 1, pn, 0.0)
    conv = (cw_ref[0:1, :] * pbuf[7:7 + TS, :] + cw_ref[1:2, :] * p
            + cw_ref[2:3, :] * pbuf[9:9 + TS, :])
    mix = (proj(h, C_CB, D_CONV) * conv).astype(BF16)
    conv_branch = jnp.dot(mix, wco_ref[...], preferred_element_type=F32)
    g_b = _sigmoid(proj(h, C_ZB, D_MODEL) + bg_ref[:, D_MODEL:])
    cp_ref[0] = g_b * conv_branch
    ga_ref[0] = _sigmoid(proj(h, C_ZA, D_MODEL) + bg_ref[:, :D_MODEL])


def _in_proj(x, n1, w_in, b_gate, conv_w, w_co, cos_t, sa_t, sb_t):
    B, S, D = x.shape
    nj = S // TS
    const = lambda shape: pl.BlockSpec(shape, lambda b, j: (0,) * len(shape),
                                       pipeline_mode=pl.Buffered(1))
    in_specs = [
        pl.BlockSpec((1, TS, D), lambda b, j: (b, j, 0)),
        pl.BlockSpec((1, 8, D), lambda b, j: (b, jnp.maximum(j * (TS // 8) - 1, 0), 0)),
        pl.BlockSpec((1, 8, D), lambda b, j: (b, jnp.minimum((j + 1) * (TS // 8), S // 8 - 1), 0)),
        const((1, D)),
        const(w_in.shape),
        const((1, 2 * D)),
        const((3, D_CONV)),
        const((D_CONV, D)),
        pl.BlockSpec((TS, 128), lambda b, j: (j, 0)),
        pl.BlockSpec((TS, 128), lambda b, j: (j, 0)),
        pl.BlockSpec((TS, 128), lambda b, j: (j, 0)),
    ]
    out_shape, out_specs = [], []
    for _ in range(3):
        for d in DILATIONS:
            out_shape.append(jax.ShapeDtypeStruct((B, d, S // d, D_GRP), BF16))
            out_specs.append(pl.BlockSpec((1, d, TS // d, D_GRP), lambda b, j: (b, 0, j, 0)))
    perm = [0, 3, 6, 1, 4, 7, 2, 5, 8]
    out_shape = [out_shape[i] for i in perm]
    out_specs = [out_specs[i] for i in perm]
    out_shape += [jax.ShapeDtypeStruct((B, S, D), F32), jax.ShapeDtypeStruct((B, S, D), F32)]
    out_specs += [pl.BlockSpec((1, TS, D), lambda b, j: (b, j, 0))] * 2
    return pl.pallas_call(
        _in_proj_kernel,
        grid=(B, nj),
        in_specs=in_specs,
        out_specs=out_specs,
        out_shape=out_shape,
        scratch_shapes=[pltpu.VMEM((TS + 16, D_CONV), F32), pltpu.VMEM((2, TS, 128), F32)],
        compiler_params=pltpu.CompilerParams(
            dimension_semantics=("parallel", "parallel"), vmem_limit_bytes=VMEM_LIMIT),
        name="in_proj",
    )(x, x, x, n1, w_in, b_gate, conv_w, w_co, cos_t, sa_t, sb_t)


def _attn_kernel(*refs):
    ins = refs[:21]
    o_ref = refs[21]
    kbufs = refs[22:28]
    os_ref, ls_ref = refs[28], refs[29]
    j = pl.program_id(1)
    nj = pl.num_programs(1)

    lane = lax.broadcasted_iota(jnp.int32, (QB, D_GRP), 1)
    head_of_lane = lane // HEAD_DIM
    row = lax.broadcasted_iota(jnp.int32, (HEADS_PER_GROUP * QB, KW), 0) % QB
    col = lax.broadcasted_iota(jnp.int32, (HEADS_PER_GROUP * QB, KW), 1)
    band = (col >= row) & (col <= row + 2 * HALF)

    for g, d in enumerate(DILATIONS):
        q_ref, kc, kp, kn, vc, vp, vn = ins[7 * g:7 * g + 7]
        kb, vb = kbufs[2 * g], kbufs[2 * g + 1]
        n = TQ // d
        nblk = n // QB
        for buf, prev, cur, nxt in ((kb, kp, kc, kn), (vb, vp, vc, vn)):
            buf[:, 0:HALF, :] = prev[0]
            buf[:, HALF:HALF + n, :] = cur[0]
            buf[:, HALF + n:, :] = nxt[0]

        def block(idx, carry, q_ref=q_ref, kb=kb, vb=vb, d=d, nblk=nblk, g=g):
            r = idx // nblk
            jb = idx % nblk
            base = pl.multiple_of(jb * QB, QB)
            qb = q_ref[0, r, pl.ds(base, QB), :]
            kw = kb[r, pl.ds(base, KW), :]
            vw = vb[r, pl.ds(base, KW), :]
            qs = jnp.concatenate(
                [jnp.where(head_of_lane == hh, qb, jnp.zeros_like(qb)) for hh in range(HEADS_PER_GROUP)],
                axis=0)
            s = lax.dot_general(qs, kw, (((1,), (1,)), ((), ())), preferred_element_type=F32)
            lo = jnp.where((j == 0) & (jb == 0), HALF, 0)
            hi = jnp.where((j == nj - 1) & (jb == nblk - 1), KW - HALF, KW)
            s = jnp.where(band & (col >= lo) & (col < hi), s, NEG)
            m = jnp.max(s, axis=1, keepdims=True)
            p = jnp.exp(s - m)
            l = jnp.sum(p, axis=1, keepdims=True)
            pv = jnp.dot(p.astype(BF16), vw, preferred_element_type=F32)
            lse = m + jnp.log(l)
            o = jnp.zeros((QB, D_GRP), F32)
            ls = jnp.zeros((QB, D_GRP), F32)
            for hh in range(HEADS_PER_GROUP):
                sl = slice(hh * QB, (hh + 1) * QB)
                sel = head_of_lane == hh
                o = jnp.where(sel, pv[sl] / l[sl], o)
                ls = jnp.where(sel, lse[sl], ls)
            for c in range(2):
                cs = slice(c * 128, (c + 1) * 128)
                if d == 1:
                    os_ref[2 * g + c, pl.ds(base, QB), :] = o[:, cs]
                    ls_ref[2 * g + c, pl.ds(base, QB), :] = ls[:, cs]
                else:
                    start = r + base * d
                    os_ref[2 * g + c, pl.ds(start, QB, stride=d), :] = o[:, cs]
                    ls_ref[2 * g + c, pl.ds(start, QB, stride=d), :] = ls[:, cs]
            return carry

        lax.fori_loop(0, d * nblk, block, 0)

    def merge(c, carry):
        rows = pl.ds(pl.multiple_of(c * QB, QB), QB)
        for c in range(2):
            l0, l1, l2 = ls_ref[c, rows, :], ls_ref[2 + c, rows, :], ls_ref[4 + c, rows, :]
            mm = jnp.maximum(jnp.maximum(l0, l1), l2)
            w0, w1, w2 = jnp.exp(l0 - mm), jnp.exp(l1 - mm), jnp.exp(l2 - mm)
            o = (w0 * os_ref[c, rows, :] + w1 * os_ref[2 + c, rows, :]
                 + w2 * os_ref[4 + c, rows, :]) / (w0 + w1 + w2)
            o_ref[0, rows, c * 128:(c + 1) * 128] = o.astype(BF16)
        return carry

    lax.fori_loop(0, TQ // QB, merge, 0)


def _attn(qkv):
    B = qkv[0].shape[0]
    S = qkv[0].shape[2]
    nj = S // TQ
    ins, in_specs, scratch = [], [], []
    for g, d in enumerate(DILATIONS):
        q, k, v = qkv[3 * g:3 * g + 3]
        n = TQ // d
        L = S // d
        nh = n // HALF
        cur = pl.BlockSpec((1, d, n, D_GRP), lambda b, j: (b, 0, j, 0))
        prev = pl.BlockSpec((1, d, HALF, D_GRP),
                            lambda b, j, nh=nh: (b, 0, jnp.maximum(j * nh - 1, 0), 0))
        nxt = pl.BlockSpec((1, d, HALF, D_GRP),
                           lambda b, j, nh=nh, L=L: (b, 0, jnp.minimum((j + 1) * nh, L // HALF - 1), 0))
        ins += [q, k, k, k, v, v, v]
        in_specs += [cur, cur, prev, nxt, cur, prev, nxt]
        scratch += [pltpu.VMEM((d, n + 2 * HALF, D_GRP), BF16)] * 2
    scratch += [pltpu.VMEM((6, TQ, 128), F32), pltpu.VMEM((6, TQ, 128), F32)]
    return pl.pallas_call(
        _attn_kernel,
        grid=(B, nj),
        in_specs=in_specs,
        out_specs=pl.BlockSpec((1, TQ, D_GRP), lambda b, j: (b, j, 0)),
        out_shape=jax.ShapeDtypeStruct((B, S, D_GRP), BF16),
        scratch_shapes=scratch,
        compiler_params=pltpu.CompilerParams(
            dimension_semantics=("parallel", "parallel"), vmem_limit_bytes=VMEM_LIMIT),
        name="attn",
    )(*ins)


def _post_kernel(o_ref, ga_ref, cp_ref, x_ref, wao_ref, wo_ref, n2_ref, wr_ref, br_ref, tri_ref,
                 x1_ref, h2_ref, rt_ref, cnt_ref, carry):
    first = (pl.program_id(0) == 0) & (pl.program_id(1) == 0)

    @pl.when(first)
    def _():
        carry[...] = jnp.zeros_like(carry)

    att = jnp.dot(o_ref[0], wao_ref[...], preferred_element_type=F32)
    merged = (ga_ref[0] * att + cp_ref[0]).astype(BF16)
    x1 = x_ref[0] + jnp.dot(merged, wo_ref[...], preferred_element_type=F32)
    x1_ref[0] = x1
    h2 = _rms(x1, n2_ref[...])
    h2_ref[0] = h2

    lt = lax.dot_general(wr_ref[...], h2.astype(BF16), (((1,), (1,)), ((), ())),
                         preferred_element_type=F32) + br_ref[:, 0:1]
    grow = lax.broadcasted_iota(jnp.int32, (8, TS), 0)
    gl = jnp.where(grow < N_EXPERT_GROUPS, lt[0:8], NEG)
    gmax = jnp.max(gl, axis=0, keepdims=True)
    grp = jnp.min(jnp.where(gl == gmax, grow, 8), axis=0, keepdims=True)
    grp_w = 1.0 / jnp.sum(jnp.exp(gl - gmax), axis=0, keepdims=True)
    erow = lax.broadcasted_iota(jnp.int32, (N_EXPERTS, TS), 0)
    el = jnp.where(erow // EXPERTS_PER_GROUP == grp, lt[8:8 + N_EXPERTS], NEG)
    v1 = jnp.max(el, axis=0, keepdims=True)
    i1 = jnp.min(jnp.where(el == v1, erow, N_EXPERTS), axis=0, keepdims=True)
    el2 = jnp.where(erow == i1, NEG, el)
    v2 = jnp.max(el2, axis=0, keepdims=True)
    i2 = jnp.min(jnp.where(el2 == v2, erow, N_EXPERTS), axis=0, keepdims=True)
    t = jnp.exp(v2 - v1)
    den = 1.0 + t
    gate1 = grp_w * (1.0 / den)
    gate2 = grp_w * (t / den)

    oh1 = (erow == i1).astype(F32)
    oh2 = (erow == i2).astype(F32)
    both = oh1 + oh2
    cum = jnp.dot(both.astype(BF16), tri_ref[...], preferred_element_type=F32)
    basec = carry[:, 0:1] + cum
    rank1 = jnp.sum(oh1 * basec, axis=0, keepdims=True)
    rank2 = jnp.sum(oh2 * basec, axis=0, keepdims=True)
    newc = carry[...] + jnp.sum(both, axis=1, keepdims=True)
    carry[...] = newc
    cnt_ref[...] = newc
    zero = jnp.zeros((1, TS), F32)
    rt_ref[0] = jnp.concatenate(
        [i1.astype(F32), i2.astype(F32), gate1, gate2, rank1, rank2, zero, zero], axis=0)


def _post(o_att, g_a, cpart, x, w_ao, w_o, n2, wr, br, tri):
    B, S, D = x.shape
    nj = S // TS
    const = lambda shape: pl.BlockSpec(shape, lambda b, j: (0,) * len(shape),
                                       pipeline_mode=pl.Buffered(1))
    tile = lambda w: pl.BlockSpec((1, TS, w), lambda b, j: (b, j, 0))
    return pl.pallas_call(
        _post_kernel,
        grid=(B, nj),
        in_specs=[tile(D_GRP), tile(D), tile(D), tile(D),
                  const((D_GRP, D)), const((D, D)), const((1, D)),
                  const((ROUTER_ROWS, D)), const((ROUTER_ROWS, 128)), const((TS, TS))],
        out_specs=[tile(D), tile(D),
                   pl.BlockSpec((1, 8, TS), lambda b, j: (b, 0, j)),
                   pl.BlockSpec((N_EXPERTS, 128), lambda b, j: (0, 0))],
        out_shape=[jax.ShapeDtypeStruct((B, S, D), F32), jax.ShapeDtypeStruct((B, S, D), F32),
                   jax.ShapeDtypeStruct((B, 8, S), F32),
                   jax.ShapeDtypeStruct((N_EXPERTS, 128), F32)],
        scratch_shapes=[pltpu.VMEM((N_EXPERTS, 128), F32)],
        compiler_params=pltpu.CompilerParams(
            dimension_semantics=("arbitrary", "arbitrary"), vmem_limit_bytes=VMEM_LIMIT),
        name="post",
    )(o_att, g_a, cpart, x, w_ao, w_o, n2, wr, br, tri)


def _row_gather_start(src_hbm, idx_ref, buf, sem, slot, count):
    def body(i, c):
        t = idx_ref[0, 0, i]
        pltpu.make_async_copy(src_hbm.at[pl.ds(t, 1)], buf.at[slot, pl.ds(i, 1)], sem.at[slot]).start()
        return c
    lax.fori_loop(0, count, body, 0, unroll=8)


def _row_gather_wait(src_hbm, buf, sem, slot, count):
    pltpu.make_async_copy(src_hbm.at[pl.ds(0, count)], buf.at[slot], sem.at[slot]).wait()


def _moe_kernel(be_ref, nu_ref, tok_ref, tokn_ref, w_ref, h2_hbm, wg_ref, wu_ref, wd_ref,
                out_ref, xbuf, sem):
    b = pl.program_id(0)
    nused = nu_ref[0]
    slot = b % 2

    @pl.when(b == 0)
    def _():
        _row_gather_start(h2_hbm, tok_ref, xbuf, sem, 0, MOE_BLK)

    @pl.when(b + 1 < nused)
    def _():
        _row_gather_start(h2_hbm, tokn_ref, xbuf, sem, 1 - slot, MOE_BLK)

    @pl.when(b < nused)
    def _():
        _row_gather_wait(h2_hbm, xbuf, sem, slot, MOE_BLK)
        x = xbuf[slot].astype(BF16)
        a = jnp.dot(x, wg_ref[0], preferred_element_type=F32)
        u = jnp.dot(x, wu_ref[0], preferred_element_type=F32)
        hm = (a * _sigmoid(a) * u).astype(BF16)
        y = jnp.dot(hm, wd_ref[0], preferred_element_type=F32)
        out_ref[...] = y * w_ref[0]

    @pl.when(b >= nused)
    def _():
        out_ref[...] = jnp.zeros_like(out_ref)


def _moe(block_e, nused, slot_tok, slot_w, h2, w_g, w_u, w_d):
    nb = block_e.shape[0]
    D = h2.shape[1]
    grid_spec = pltpu.PrefetchScalarGridSpec(
        num_scalar_prefetch=2,
        grid=(nb,),
        in_specs=[
            pl.BlockSpec((1, 1, MOE_BLK), lambda b, be, nu: (b, 0, 0), memory_space=pltpu.SMEM),
            pl.BlockSpec((1, 1, MOE_BLK), lambda b, be, nu: (jnp.minimum(b + 1, nb - 1), 0, 0),
                         memory_space=pltpu.SMEM),
            pl.BlockSpec((1, MOE_BLK, 1), lambda b, be, nu: (b, 0, 0)),
            pl.BlockSpec(memory_space=pl.ANY),
            pl.BlockSpec((1, D, D_EXPERT), lambda b, be, nu: (be[b], 0, 0)),
            pl.BlockSpec((1, D, D_EXPERT), lambda b, be, nu: (be[b], 0, 0)),
            pl.BlockSpec((1, D_EXPERT, D), lambda b, be, nu: (be[b], 0, 0)),
        ],
        out_specs=pl.BlockSpec((MOE_BLK, D), lambda b, be, nu: (b, 0)),
        scratch_shapes=[pltpu.VMEM((2, MOE_BLK, D), F32), pltpu.SemaphoreType.DMA((2,))],
    )
    return pl.pallas_call(
        _moe_kernel,
        grid_spec=grid_spec,
        out_shape=jax.ShapeDtypeStruct((nb * MOE_BLK, D), F32),
        compiler_params=pltpu.CompilerParams(
            dimension_semantics=("arbitrary",), vmem_limit_bytes=VMEM_LIMIT),
        name="moe",
    )(block_e, nused, slot_tok.reshape(nb, 1, MOE_BLK), slot_tok.reshape(nb, 1, MOE_BLK),
      slot_w.reshape(nb, MOE_BLK, 1), h2, w_g, w_u, w_d)


def _final_kernel(d_ref, dn_ref, x1_ref, gf_ref, yb_hbm, out_ref, ybuf, sem):
    i = pl.program_id(0)
    n = pl.num_programs(0)
    slot = i % 2

    @pl.when(i == 0)
    def _():
        _row_gather_start(yb_hbm, d_ref, ybuf, sem, 0, 2 * TF)

    @pl.when(i + 1 < n)
    def _():
        _row_gather_start(yb_hbm, dn_ref, ybuf, sem, 1 - slot, 2 * TF)

    _row_gather_wait(yb_hbm, ybuf, sem, slot, 2 * TF)
    y = ybuf[slot, 0:TF, :] + ybuf[slot, TF:2 * TF, :]
    out_ref[...] = _rms(x1_ref[...] + y, gf_ref[...])


def _final(dest, x1, gf, yb):
    T, D = x1.shape
    n = T // TF
    return pl.pallas_call(
        _final_kernel,
        grid=(n,),
        in_specs=[
            pl.BlockSpec((1, 1, 2 * TF), lambda i: (i, 0, 0), memory_space=pltpu.SMEM),
            pl.BlockSpec((1, 1, 2 * TF), lambda i: (jnp.minimum(i + 1, n - 1), 0, 0),
                         memory_space=pltpu.SMEM),
            pl.BlockSpec((TF, D), lambda i: (i, 0)),
            pl.BlockSpec((1, D), lambda i: (0, 0)),
            pl.BlockSpec(memory_space=pl.ANY),
        ],
        out_specs=pl.BlockSpec((TF, D), lambda i: (i, 0)),
        out_shape=jax.ShapeDtypeStruct((T, D), F32),
        scratch_shapes=[pltpu.VMEM((2, 2 * TF, D), F32), pltpu.SemaphoreType.DMA((2,))],
        compiler_params=pltpu.CompilerParams(
            dimension_semantics=("arbitrary",), vmem_limit_bytes=VMEM_LIMIT),
        name="final",
    )(dest, dest, x1, gf, yb)


def _rotary_tables(S):
    inv_freq = 1.0 / (ROPE_THETA ** (jnp.arange(0, HEAD_DIM, 2, dtype=F32) / HEAD_DIM))
    ang = jnp.arange(S, dtype=F32)[:, None] * inv_freq[None, :]
    cos, sin = jnp.cos(ang), jnp.sin(ang)
    z = jnp.zeros_like(sin)
    cos_t = jnp.concatenate([cos, cos, cos, cos], axis=1)
    sa_t = jnp.concatenate([-sin, z, -sin, z], axis=1)
    sb_t = jnp.concatenate([z, sin, z, sin], axis=1)
    return cos_t, sa_t, sb_t


def _trunk(x, p):
    B, S, D = x.shape
    T = B * S
    outs = _in_proj(x, p["n1"], p["w_in"], p["b_gate"], p["conv_w"], p["w_co"], *p["rot"])
    qkv, g_a, cpart = outs[:9], outs[9], outs[10]
    o_att = _attn(qkv)
    x1, h2, rt, cnt = _post(o_att, g_a, cpart, x, p["w_ao"], p["w_o"], p["n2"], p["wr"], p["br"], p["tri"])

    e1 = rt[:, 0, :].reshape(T).astype(jnp.int32)
    e2 = rt[:, 1, :].reshape(T).astype(jnp.int32)
    g1 = rt[:, 2, :].reshape(T)
    g2 = rt[:, 3, :].reshape(T)
    r1 = rt[:, 4, :].reshape(T).astype(jnp.int32)
    r2 = rt[:, 5, :].reshape(T).astype(jnp.int32)
    counts = cnt[:, 0].astype(jnp.int32)
    pcounts = (counts + MOE_BLK - 1) // MOE_BLK * MOE_BLK
    pends = jnp.cumsum(pcounts)
    pstarts = pends - pcounts
    d1 = pstarts[e1] + r1
    d2 = pstarts[e2] + r2
    nb = (2 * T) // MOE_BLK + N_EXPERTS
    tok = jnp.arange(T, dtype=jnp.int32)
    dd = jnp.concatenate([d1, d2])
    slot_tok = jnp.zeros((nb * MOE_BLK,), jnp.int32).at[dd].set(jnp.concatenate([tok, tok]))
    slot_w = jnp.zeros((nb * MOE_BLK,), F32).at[dd].set(jnp.concatenate([g1, g2]))
    block_e = jnp.minimum(
        jnp.searchsorted(pends, jnp.arange(nb, dtype=jnp.int32) * MOE_BLK, side="right"),
        N_EXPERTS - 1).astype(jnp.int32)
    nused = (pends[-1:] // MOE_BLK).astype(jnp.int32)

    yb = _moe(block_e, nused, slot_tok, slot_w, h2.reshape(T, D), p["w_g"], p["w_u"], p["w_d"])
    dest = jnp.concatenate([d1.reshape(T // TF, 1, TF), d2.reshape(T // TF, 1, TF)], axis=2)
    y = _final(dest, x1.reshape(T, D), p["gf"], yb)
    return y.reshape(B, S, D)


def kernel(x_prompt, x_sample, norm1_g, w_in, b_gate, conv_w, w_attn_out, w_conv_out, w_out, norm2_g,
           w_router_group, b_router_group, w_router_expert, b_router_expert, w_exp_gate, w_exp_up,
           w_exp_down, norm_f_g):
    assert norm1_g.shape[0] == 1, "single-layer trunk"
    S = x_prompt.shape[1]
    wr = jnp.zeros((ROUTER_ROWS, D_MODEL), F32)
    wr = wr.at[0:N_EXPERT_GROUPS].set(w_router_group[0].T).at[8:8 + N_EXPERTS].set(w_router_expert[0].T)
    br = jnp.zeros((ROUTER_ROWS,), F32)
    br = br.at[0:N_EXPERT_GROUPS].set(b_router_group[0]).at[8:8 + N_EXPERTS].set(b_router_expert[0])
    ti = jnp.arange(TS)
    p = dict(
        n1=norm1_g, w_in=w_in[0].astype(BF16), b_gate=b_gate, conv_w=conv_w[0],
        w_co=w_conv_out[0].astype(BF16), rot=_rotary_tables(S),
        w_ao=w_attn_out[0].astype(BF16), w_o=w_out[0].astype(BF16), n2=norm2_g,
        wr=wr.astype(BF16), br=jnp.broadcast_to(br[:, None], (ROUTER_ROWS, 128)),
        tri=(ti[:, None] < ti[None, :]).astype(BF16),
        w_g=w_exp_gate[0].astype(BF16), w_u=w_exp_up[0].astype(BF16), w_d=w_exp_down[0].astype(BF16),
        gf=norm_f_g.reshape(1, D_MODEL),
    )
    return _trunk(x_prompt, p), _trunk(x_sample, p)
```

```python
import functools

import jax
import jax.numpy as jnp
from jax import lax
from jax.experimental import pallas as pl
from jax.experimental.pallas import tpu as pltpu

D_MODEL = 1024
HEAD_DIM = 64
HEADS_PER_GROUP = 4
DILATIONS = (1, 4, 16)
HALF = 64
D_GRP = HEADS_PER_GROUP * HEAD_DIM
D_ATT = 3 * D_GRP
D_CONV = 768
N_EXPERT_GROUPS = 4
EXPERTS_PER_GROUP = 8
N_EXPERTS = 32
D_EXPERT = 512
RMS_EPS = 1e-6
NEG = -1e30
ROPE_THETA = 10000.0

C_Q, C_K, C_V = 0, D_ATT, 2 * D_ATT
C_CU, C_CB, C_CC = 3 * D_ATT, 3 * D_ATT + D_CONV, 3 * D_ATT + 2 * D_CONV
C_ZA = 3 * D_ATT + 3 * D_CONV
C_ZB = C_ZA + D_MODEL

TS = 512
TQ = 2048
QB = 128
KW = QB + 2 * HALF
MOE_BLK = 256
TF = 256
TD = 512
ROUTER_ROWS = 48
VMEM_LIMIT = 56 * 1024 * 1024

F32 = jnp.float32
BF16 = jnp.bfloat16


def _rms(xf, g):
    return xf * lax.rsqrt(jnp.mean(xf * xf, axis=-1, keepdims=True) + RMS_EPS) * g


def _sigmoid(x):
    return 1.0 / (1.0 + jnp.exp(-x))


def _in_proj_kernel(x_ref, xp_ref, xn_ref, n1_ref, w_ref, bg_ref, cw_ref, wco_ref,
                    cos_ref, sa_ref, sb_ref,
                    q1, k1, v1, q2, k2, v2, q3, k3, v3, ga_ref, cp_ref,
                    pbuf, dbuf):
    j = pl.program_id(1)
    nj = pl.num_programs(1)
    g1 = n1_ref[...]
    h = _rms(x_ref[0], g1).astype(BF16)

    def proj(hh, c0, width):
        return jnp.dot(hh, w_ref[:, c0:c0 + width], preferred_element_type=F32)

    cosv = cos_ref[...]
    sav = sa_ref[...]
    sbv = sb_ref[...]

    def rotary(z):
        return z * cosv + pltpu.roll(z, 96, 1) * sav + pltpu.roll(z, 32, 1) * sbv

    def emit(outs, c0, rot, scale):
        for g, d in enumerate(DILATIONS):
            z = proj(h, c0 + g * D_GRP, D_GRP)
            if rot:
                z = jnp.concatenate([rotary(z[:, :128]), rotary(z[:, 128:])], axis=1)
            if scale != 1.0:
                z = z * scale
            if d == 1:
                outs[g][0, 0] = z.astype(BF16)
            else:
                dbuf[0] = z[:, :128]
                dbuf[1] = z[:, 128:]
                for r in range(d):
                    for c in range(2):
                        outs[g][0, r, :, c * 128:(c + 1) * 128] = (
                            dbuf[c, pl.ds(r, TS // d, stride=d), :].astype(BF16))

    emit((q1, q2, q3), C_Q, True, HEAD_DIM ** -0.5)
    emit((k1, k2, k3), C_K, True, 1.0)
    emit((v1, v2, v3), C_V, False, 1.0)

    p = proj(h, C_CC, D_CONV) * proj(h, C_CU, D_CONV)
    pbuf[8:8 + TS, :] = p
    hp = _rms(xp_ref[0], g1).astype(BF16)
    pp = proj(hp, C_CC, D_CONV) * proj(hp, C_CU, D_CONV)
    pbuf[0:8, :] = jnp.where(j > 0, pp, 0.0)
    hn = _rms(xn_ref[0], g1).astype(BF16)
    pn = proj(hn, C_CC, D_CONV) * proj(hn, C_CU, D_CONV)
    pbuf[8 + TS:16 + TS, :] = jnp.where(j < nj - 1, pn, 0.0)
    conv = (cw_ref[0:1, :] * pbuf[7:7 + TS, :] + cw_ref[1:2, :] * p
            + cw_ref[2:3, :] * pbuf[9:9 + TS, :])
    mix = (proj(h, C_CB, D_CONV) * conv).astype(BF16)
    conv_branch = jnp.dot(mix, wco_ref[...], preferred_element_type=F32)
    g_b = _sigmoid(proj(h, C_ZB, D_MODEL) + bg_ref[:, D_MODEL:])
    cp_ref[0] = g_b * conv_branch
    ga_ref[0] = _sigmoid(proj(h, C_ZA, D_MODEL) + bg_ref[:, :D_MODEL])


def _in_proj(x, n1, w_in, b_gate, conv_w, w_co, cos_t, sa_t, sb_t):
    B, S, D = x.shape
    nj = S // TS
    const = lambda shape: pl.BlockSpec(shape, lambda b, j: (0,) * len(shape),
                                       pipeline_mode=pl.Buffered(1))
    in_specs = [
        pl.BlockSpec((1, TS, D), lambda b, j: (b, j, 0)),
        pl.BlockSpec((1, 8, D), lambda b, j: (b, jnp.maximum(j * (TS // 8) - 1, 0), 0)),
        pl.BlockSpec((1, 8, D), lambda b, j: (b, jnp.minimum((j + 1) * (TS // 8), S // 8 - 1), 0)),
        const((1, D)),
        const(w_in.shape),
        const((1, 2 * D)),
        const((3, D_CONV)),
        const((D_CONV, D)),
        pl.BlockSpec((TS, 128), lambda b, j: (j, 0)),
        pl.BlockSpec((TS, 128), lambda b, j: (j, 0)),
        pl.BlockSpec((TS, 128), lambda b, j: (j, 0)),
    ]
    out_shape, out_specs = [], []
    for _ in range(3):
        for d in DILATIONS:
            out_shape.append(jax.ShapeDtypeStruct((B, d, S // d, D_GRP), BF16))
            out_specs.append(pl.BlockSpec((1, d, TS // d, D_GRP), lambda b, j: (b, 0, j, 0)))
    perm = [0, 3, 6, 1, 4, 7, 2, 5, 8]
    out_shape = [out_shape[i] for i in perm]
    out_specs = [out_specs[i] for i in perm]
    out_shape += [jax.ShapeDtypeStruct((B, S, D), F32), jax.ShapeDtypeStruct((B, S, D), F32)]
    out_specs += [pl.BlockSpec((1, TS, D), lambda b, j: (b, j, 0))] * 2
    return pl.pallas_call(
        _in_proj_kernel,
        grid=(B, nj),
        in_specs=in_specs,
        out_specs=out_specs,
        out_shape=out_shape,
        scratch_shapes=[pltpu.VMEM((TS + 16, D_CONV), F32), pltpu.VMEM((2, TS, 128), F32)],
        compiler_params=pltpu.CompilerParams(
            dimension_semantics=("parallel", "parallel"), vmem_limit_bytes=VMEM_LIMIT),
        name="in_proj",
    )(x, x, x, n1, w_in, b_gate, conv_w, w_co, cos_t, sa_t, sb_t)


def _attn_kernel(*refs):
    ins = refs[:21]
    o_ref = refs[21]
    kbufs = refs[22:28]
    os_ref, ls_ref = refs[28], refs[29]
    j = pl.program_id(1)
    nj = pl.num_programs(1)

    lane = lax.broadcasted_iota(jnp.int32, (QB, D_GRP), 1)
    head_of_lane = lane // HEAD_DIM
    row = lax.broadcasted_iota(jnp.int32, (HEADS_PER_GROUP * QB, KW), 0) % QB
    col = lax.broadcasted_iota(jnp.int32, (HEADS_PER_GROUP * QB, KW), 1)
    band = (col >= row) & (col <= row + 2 * HALF)

    for g, d in enumerate(DILATIONS):
        q_ref, kc, kp, kn, vc, vp, vn = ins[7 * g:7 * g + 7]
        kb, vb = kbufs[2 * g], kbufs[2 * g + 1]
        n = TQ // d
        nblk = n // QB
        for buf, prev, cur, nxt in ((kb, kp, kc, kn), (vb, vp, vc, vn)):
            buf[:, 0:HALF, :] = prev[0]
            buf[:, HALF:HALF + n, :] = cur[0]
            buf[:, HALF + n:, :] = nxt[0]

        def block(idx, carry, q_ref=q_ref, kb=kb, vb=vb, d=d, nblk=nblk, g=g):
            r = idx // nblk
            jb = idx % nblk
            base = pl.multiple_of(jb * QB, QB)
            qb = q_ref[0, r, pl.ds(base, QB), :]
            kw = kb[r, pl.ds(base, KW), :]
            vw = vb[r, pl.ds(base, KW), :]
            qs = jnp.concatenate(
                [jnp.where(head_of_lane == hh, qb, jnp.zeros_like(qb)) for hh in range(HEADS_PER_GROUP)],
                axis=0)
            s = lax.dot_general(qs, kw, (((1,), (1,)), ((), ())), preferred_element_type=F32)
            lo = jnp.where((j == 0) & (jb == 0), HALF, 0)
            hi = jnp.where((j == nj - 1) & (jb == nblk - 1), KW - HALF, KW)
            s = jnp.where(band & (col >= lo) & (col < hi), s, NEG)
            m = jnp.max(s, axis=1, keepdims=True)
            p = jnp.exp(s - m)
            l = jnp.sum(p, axis=1, keepdims=True)
            pv = jnp.dot(p.astype(BF16), vw, preferred_element_type=F32)
            lse = m + jnp.log(l)
            o = jnp.zeros((QB, D_GRP), F32)
            ls = jnp.zeros((QB, D_GRP), F32)
            for hh in range(HEADS_PER_GROUP):
                sl = slice(hh * QB, (hh + 1) * QB)
                sel = head_of_lane == hh
                o = jnp.where(sel, pv[sl] / l[sl], o)
                ls = jnp.where(sel, lse[sl], ls)
            for c in range(2):
                cs = slice(c * 128, (c + 1) * 128)
                if d == 1:
                    os_ref[2 * g + c, pl.ds(base, QB), :] = o[:, cs]
                    ls_ref[2 * g + c, pl.ds(base, QB), :] = ls[:, cs]
                else:
                    start = r + base * d
                    os_ref[2 * g + c, pl.ds(start, QB, stride=d), :] = o[:, cs]
                    ls_ref[2 * g + c, pl.ds(start, QB, stride=d), :] = ls[:, cs]
            return carry

        lax.fori_loop(0, d * nblk, block, 0)

    def merge(c, carry):
        rows = pl.ds(pl.multiple_of(c * QB, QB), QB)
        for c in range(2):
            l0, l1, l2 = ls_ref[c, rows, :], ls_ref[2 + c, rows, :], ls_ref[4 + c, rows, :]
            mm = jnp.maximum(jnp.maximum(l0, l1), l2)
            w0, w1, w2 = jnp.exp(l0 - mm), jnp.exp(l1 - mm), jnp.exp(l2 - mm)
            o = (w0 * os_ref[c, rows, :] + w1 * os_ref[2 + c, rows, :]
                 + w2 * os_ref[4 + c, rows, :]) / (w0 + w1 + w2)
            o_ref[0, rows, c * 128:(c + 1) * 128] = o.astype(BF16)
        return carry

    lax.fori_loop(0, TQ // QB, merge, 0)


def _attn(qkv):
    B = qkv[0].shape[0]
    S = qkv[0].shape[2]
    nj = S // TQ
    ins, in_specs, scratch = [], [], []
    for g, d in enumerate(DILATIONS):
        q, k, v = qkv[3 * g:3 * g + 3]
        n = TQ // d
        L = S // d
        nh = n // HALF
        cur = pl.BlockSpec((1, d, n, D_GRP), lambda b, j: (b, 0, j, 0))
        prev = pl.BlockSpec((1, d, HALF, D_GRP),
                            lambda b, j, nh=nh: (b, 0, jnp.maximum(j * nh - 1, 0), 0))
        nxt = pl.BlockSpec((1, d, HALF, D_GRP),
                           lambda b, j, nh=nh, L=L: (b, 0, jnp.minimum((j + 1) * nh, L // HALF - 1), 0))
        ins += [q, k, k, k, v, v, v]
        in_specs += [cur, cur, prev, nxt, cur, prev, nxt]
        scratch += [pltpu.VMEM((d, n + 2 * HALF, D_GRP), BF16)] * 2
    scratch += [pltpu.VMEM((6, TQ, 128), F32), pltpu.VMEM((6, TQ, 128), F32)]
    return pl.pallas_call(
        _attn_kernel,
        grid=(B, nj),
        in_specs=in_specs,
        out_specs=pl.BlockSpec((1, TQ, D_GRP), lambda b, j: (b, j, 0)),
        out_shape=jax.ShapeDtypeStruct((B, S, D_GRP), BF16),
        scratch_shapes=scratch,
        compiler_params=pltpu.CompilerParams(
            dimension_semantics=("parallel", "parallel"), vmem_limit_bytes=VMEM_LIMIT),
        name="attn",
    )(*ins)


def _post_kernel(o_ref, ga_ref, cp_ref, x_ref, wao_ref, wo_ref, n2_ref, wr_ref, br_ref, tri_ref,
                 x1_ref, h2_ref, rt_ref, cnt_ref, carry):
    first = (pl.program_id(0) == 0) & (pl.program_id(1) == 0)

    @pl.when(first)
    def _():
        carry[...] = jnp.zeros_like(carry)

    att = jnp.dot(o_ref[0], wao_ref[...], preferred_element_type=F32)
    merged = (ga_ref[0] * att + cp_ref[0]).astype(BF16)
    x1 = x_ref[0] + jnp.dot(merged, wo_ref[...], preferred_element_type=F32)
    x1_ref[0] = x1
    h2 = _rms(x1, n2_ref[...])
    h2_ref[0] = h2

    lt = lax.dot_general(wr_ref[...], h2.astype(BF16), (((1,), (1,)), ((), ())),
                         preferred_element_type=F32) + br_ref[:, 0:1]
    grow = lax.broadcasted_iota(jnp.int32, (8, TS), 0)
    gl = jnp.where(grow < N_EXPERT_GROUPS, lt[0:8], NEG)
    gmax = jnp.max(gl, axis=0, keepdims=True)
    grp = jnp.min(jnp.where(gl == gmax, grow, 8), axis=0, keepdims=True)
    grp_w = 1.0 / jnp.sum(jnp.exp(gl - gmax), axis=0, keepdims=True)
    erow = lax.broadcasted_iota(jnp.int32, (N_EXPERTS, TS), 0)
    el = jnp.where(erow // EXPERTS_PER_GROUP == grp, lt[8:8 + N_EXPERTS], NEG)
    v1 = jnp.max(el, axis=0, keepdims=True)
    i1 = jnp.min(jnp.where(el == v1, erow, N_EXPERTS), axis=0, keepdims=True)
    el2 = jnp.where(erow == i1, NEG, el)
    v2 = jnp.max(el2, axis=0, keepdims=True)
    i2 = jnp.min(jnp.where(el2 == v2, erow, N_EXPERTS), axis=0, keepdims=True)
    t = jnp.exp(v2 - v1)
    den = 1.0 + t
    gate1 = grp_w * (1.0 / den)
    gate2 = grp_w * (t / den)

    oh1 = (erow == i1).astype(F32)
    oh2 = (erow == i2).astype(F32)
    both = oh1 + oh2
    cum = jnp.dot(both.astype(BF16), tri_ref[...], preferred_element_type=F32)
    basec = carry[:, 0:1] + cum
    rank1 = jnp.sum(oh1 * basec, axis=0, keepdims=True)
    rank2 = jnp.sum(oh2 * basec, axis=0, keepdims=True)
    newc = carry[...] + jnp.sum(both, axis=1, keepdims=True)
    carry[...] = newc
    cnt_ref[...] = newc
    zero = jnp.zeros((1, TS), F32)
    rt_ref[0] = jnp.concatenate(
        [i1.astype(F32), i2.astype(F32), gate1, gate2, rank1, rank2, zero, zero], axis=0)


def _post(o_att, g_a, cpart, x, w_ao, w_o, n2, wr, br, tri):
    B, S, D = x.shape
    nj = S // TS
    const = lambda shape: pl.BlockSpec(shape, lambda b, j: (0,) * len(shape),
                                       pipeline_mode=pl.Buffered(1))
    tile = lambda w: pl.BlockSpec((1, TS, w), lambda b, j: (b, j, 0))
    return pl.pallas_call(
        _post_kernel,
        grid=(B, nj),
        in_specs=[tile(D_GRP), tile(D), tile(D), tile(D),
                  const((D_GRP, D)), const((D, D)), const((1, D)),
                  const((ROUTER_ROWS, D)), const((ROUTER_ROWS, 128)), const((TS, TS))],
        out_specs=[tile(D), tile(D),
                   pl.BlockSpec((1, 8, TS), lambda b, j: (b, 0, j)),
                   pl.BlockSpec((N_EXPERTS, 128), lambda b, j: (0, 0))],
        out_shape=[jax.ShapeDtypeStruct((B, S, D), F32), jax.ShapeDtypeStruct((B, S, D), F32),
                   jax.ShapeDtypeStruct((B, 8, S), F32),
                   jax.ShapeDtypeStruct((N_EXPERTS, 128), F32)],
        scratch_shapes=[pltpu.VMEM((N_EXPERTS, 128), F32)],
        compiler_params=pltpu.CompilerParams(
            dimension_semantics=("arbitrary", "arbitrary"), vmem_limit_bytes=VMEM_LIMIT),
        name="post",
    )(o_att, g_a, cpart, x, w_ao, w_o, n2, wr, br, tri)


def _row_gather_start(src_hbm, idx_ref, buf, sem, slot, count):
    def body(i, c):
        t = idx_ref[0, 0, i]
        pltpu.make_async_copy(src_hbm.at[pl.ds(t, 1)], buf.at[slot, pl.ds(i, 1)], sem.at[slot]).start()
        return c
    lax.fori_loop(0, count, body, 0, unroll=8)


def _row_gather_wait(src_hbm, buf, sem, slot, count):
    pltpu.make_async_copy(src_hbm.at[pl.ds(0, count)], buf.at[slot], sem.at[slot]).wait()


def _dispatch_kernel(d_ref, h_ref, xs_init_hbm, xs_hbm, buf, sem):
    del xs_init_hbm
    i = pl.program_id(0)
    n = pl.num_programs(0)
    slot = i % 2

    def wait_slot(s):
        for _ in range(2):
            pltpu.make_async_copy(buf.at[s], xs_hbm.at[pl.ds(0, TD)], sem.at[s]).wait()

    @pl.when(i >= 2)
    def _():
        wait_slot(slot)

    buf[slot] = h_ref[...]

    for k in range(2):
        def body(r, c, k=k):
            dst = d_ref[0, 0, k * TD + r]
            pltpu.make_async_copy(buf.at[slot, pl.ds(r, 1)], xs_hbm.at[pl.ds(dst, 1)], sem.at[slot]).start()
            return c
        lax.fori_loop(0, TD, body, 0, unroll=8)

    @pl.when(i == n - 1)
    def _():
        wait_slot(slot)
        wait_slot(1 - slot)


def _dispatch(dest, h2, nslots):
    T, D = h2.shape
    n = T // TD
    assert n >= 2
    return pl.pallas_call(
        _dispatch_kernel,
        grid=(n,),
        in_specs=[
            pl.BlockSpec((1, 1, 2 * TD), lambda i: (i, 0, 0), memory_space=pltpu.SMEM),
            pl.BlockSpec((TD, D), lambda i: (i, 0)),
            pl.BlockSpec(memory_space=pl.ANY),
        ],
        out_specs=pl.BlockSpec(memory_space=pl.ANY),
        out_shape=jax.ShapeDtypeStruct((nslots, D), F32),
        input_output_aliases={2: 0},
        scratch_shapes=[pltpu.VMEM((2, TD, D), F32), pltpu.SemaphoreType.DMA((2,))],
        compiler_params=pltpu.CompilerParams(
            dimension_semantics=("arbitrary",), vmem_limit_bytes=VMEM_LIMIT),
        name="dispatch",
    )(dest, h2, jnp.zeros((nslots, D), F32))


def _moe_kernel(be_ref, nu_ref, x_ref, wg_ref, wu_ref, wd_ref, out_ref):
    b = pl.program_id(0)

    @pl.when(b < nu_ref[0])
    def _():
        x = x_ref[...].astype(BF16)
        a = jnp.dot(x, wg_ref[0], preferred_element_type=F32)
        u = jnp.dot(x, wu_ref[0], preferred_element_type=F32)
        hm = (a * _sigmoid(a) * u).astype(BF16)
        out_ref[...] = jnp.dot(hm, wd_ref[0], preferred_element_type=F32)

    @pl.when(b >= nu_ref[0])
    def _():
        out_ref[...] = jnp.zeros_like(out_ref)


def _moe(block_e, nused, xs, w_g, w_u, w_d):
    nb = block_e.shape[0]
    D = xs.shape[1]
    grid_spec = pltpu.PrefetchScalarGridSpec(
        num_scalar_prefetch=2,
        grid=(nb,),
        in_specs=[
            pl.BlockSpec((MOE_BLK, D), lambda b, be, nu: (b, 0)),
            pl.BlockSpec((1, D, D_EXPERT), lambda b, be, nu: (be[b], 0, 0)),
            pl.BlockSpec((1, D, D_EXPERT), lambda b, be, nu: (be[b], 0, 0)),
            pl.BlockSpec((1, D_EXPERT, D), lambda b, be, nu: (be[b], 0, 0)),
        ],
        out_specs=pl.BlockSpec((MOE_BLK, D), lambda b, be, nu: (b, 0)),
    )
    return pl.pallas_call(
        _moe_kernel,
        grid_spec=grid_spec,
        out_shape=jax.ShapeDtypeStruct((nb * MOE_BLK, D), F32),
        compiler_params=pltpu.CompilerParams(
            dimension_semantics=("arbitrary",), vmem_limit_bytes=VMEM_LIMIT),
        name="moe",
    )(block_e, nused, xs, w_g, w_u, w_d)


def _final_kernel(d_ref, dn_ref, x1_ref, g_ref, gf_ref, yb_hbm, out_ref, ybuf, sem):
    i = pl.program_id(0)
    n = pl.num_programs(0)
    slot = i % 2

    @pl.when(i == 0)
    def _():
        _row_gather_start(yb_hbm, d_ref, ybuf, sem, 0, 2 * TF)

    @pl.when(i + 1 < n)
    def _():
        _row_gather_start(yb_hbm, dn_ref, ybuf, sem, 1 - slot, 2 * TF)

    _row_gather_wait(yb_hbm, ybuf, sem, slot, 2 * TF)
    y = g_ref[:, 0:1] * ybuf[slot, 0:TF, :] + g_ref[:, 1:2] * ybuf[slot, TF:2 * TF, :]
    out_ref[...] = _rms(x1_ref[...] + y, gf_ref[...])


def _final(dest, gates, x1, gf, yb):
    T, D = x1.shape
    n = T // TF
    return pl.pallas_call(
        _final_kernel,
        grid=(n,),
        in_specs=[
            pl.BlockSpec((1, 1, 2 * TF), lambda i: (i, 0, 0), memory_space=pltpu.SMEM),
            pl.BlockSpec((1, 1, 2 * TF), lambda i: (jnp.minimum(i + 1, n - 1), 0, 0),
                         memory_space=pltpu.SMEM),
            pl.BlockSpec((TF, D), lambda i: (i, 0)),
            pl.BlockSpec((TF, 2), lambda i: (i, 0)),
            pl.BlockSpec((1, D), lambda i: (0, 0)),
            pl.BlockSpec(memory_space=pl.ANY),
        ],
        out_specs=pl.BlockSpec((TF, D), lambda i: (i, 0)),
        out_shape=jax.ShapeDtypeStruct((T, D), F32),
        scratch_shapes=[pltpu.VMEM((2, 2 * TF, D), F32), pltpu.SemaphoreType.DMA((2,))],
        compiler_params=pltpu.CompilerParams(
            dimension_semantics=("arbitrary",), vmem_limit_bytes=VMEM_LIMIT),
        name="final",
    )(dest, dest, x1, gates, gf, yb)


def _rotary_tables(S):
    inv_freq = 1.0 / (ROPE_THETA ** (jnp.arange(0, HEAD_DIM, 2, dtype=F32) / HEAD_DIM))
    ang = jnp.arange(S, dtype=F32)[:, None] * inv_freq[None, :]
    cos, sin = jnp.cos(ang), jnp.sin(ang)
    z = jnp.zeros_like(sin)
    cos_t = jnp.concatenate([cos, cos, cos, cos], axis=1)
    sa_t = jnp.concatenate([-sin, z, -sin, z], axis=1)
    sb_t = jnp.concatenate([z, sin, z, sin], axis=1)
    return cos_t, sa_t, sb_t


def _trunk(x, p):
    B, S, D = x.shape
    T = B * S
    outs = _in_proj(x, p["n1"], p["w_in"], p["b_gate"], p["conv_w"], p["w_co"], *p["rot"])
    qkv, g_a, cpart = outs[:9], outs[9], outs[10]
    o_att = _attn(qkv)
    x1, h2, rt, cnt = _post(o_att, g_a, cpart, x, p["w_ao"], p["w_o"], p["n2"], p["wr"], p["br"], p["tri"])

    e1 = rt[:, 0, :].reshape(T).astype(jnp.int32)
    e2 = rt[:, 1, :].reshape(T).astype(jnp.int32)
    gates = jnp.stack([rt[:, 2, :].reshape(T), rt[:, 3, :].reshape(T)], axis=1)
    r1 = rt[:, 4, :].reshape(T).astype(jnp.int32)
    r2 = rt[:, 5, :].reshape(T).astype(jnp.int32)
    counts = cnt[:, 0].astype(jnp.int32)
    pcounts = (counts + MOE_BLK - 1) // MOE_BLK * MOE_BLK
    pends = jnp.cumsum(pcounts)
    pstarts = pends - pcounts
    d1 = pstarts[e1] + r1
    d2 = pstarts[e2] + r2
    nb = (2 * T) // MOE_BLK + N_EXPERTS
    block_start = jnp.arange(nb, dtype=jnp.int32) * MOE_BLK
    block_e = jnp.minimum(jnp.sum((pends[None, :] <= block_start[:, None]).astype(jnp.int32), axis=1),
                          N_EXPERTS - 1)
    nused = (pends[-1:] // MOE_BLK).astype(jnp.int32)

    def tiles(t):
        return jnp.concatenate([d1.reshape(T // t, 1, t), d2.reshape(T // t, 1, t)], axis=2)

    xs = _dispatch(tiles(TD), h2.reshape(T, D), nb * MOE_BLK)
    yb = _moe(block_e, nused, xs, p["w_g"], p["w_u"], p["w_d"])
    y = _final(tiles(TF), gates, x1.reshape(T, D), p["gf"], yb)
    return y.reshape(B, S, D)


def kernel(x_prompt, x_sample, norm1_g, w_in, b_gate, conv_w, w_attn_out, w_conv_out, w_out, norm2_g,
           w_router_group, b_router_group, w_router_expert, b_router_expert, w_exp_gate, w_exp_up,
           w_exp_down, norm_f_g):
    assert norm1_g.shape[0] == 1, "single-layer trunk"
    S = x_prompt.shape[1]
    wr = jnp.zeros((ROUTER_ROWS, D_MODEL), F32)
    wr = wr.at[0:N_EXPERT_GROUPS].set(w_router_group[0].T).at[8:8 + N_EXPERTS].set(w_router_expert[0].T)
    br = jnp.zeros((ROUTER_ROWS,), F32)
    br = br.at[0:N_EXPERT_GROUPS].set(b_router_group[0]).at[8:8 + N_EXPERTS].set(b_router_expert[0])
    ti = jnp.arange(TS)
    p = dict(
        n1=norm1_g, w_in=w_in[0].astype(BF16), b_gate=b_gate, conv_w=conv_w[0],
        w_co=w_conv_out[0].astype(BF16), rot=_rotary_tables(S),
        w_ao=w_attn_out[0].astype(BF16), w_o=w_out[0].astype(BF16), n2=norm2_g,
        wr=wr.astype(BF16), br=jnp.broadcast_to(br[:, None], (ROUTER_ROWS, 128)),
        tri=(ti[:, None] < ti[None, :]).astype(BF16),
        w_g=w_exp_gate[0].astype(BF16), w_u=w_exp_up[0].astype(BF16), w_d=w_exp_down[0].astype(BF16),
        gf=norm_f_g.reshape(1, D_MODEL),
    )
    return _trunk(x_prompt, p), _trunk(x_sample, p)
```

```python
import functools

import jax
import jax.numpy as jnp
from jax import lax
from jax.experimental import pallas as pl
from jax.experimental.pallas import tpu as pltpu
from jax.experimental.pallas import tpu_sc as plsc

D_MODEL = 1024
HEAD_DIM = 64
HEADS_PER_GROUP = 4
DILATIONS = (1, 4, 16)
HALF = 64
D_GRP = HEADS_PER_GROUP * HEAD_DIM
D_ATT = 3 * D_GRP
D_CONV = 768
N_EXPERT_GROUPS = 4
EXPERTS_PER_GROUP = 8
N_EXPERTS = 32
D_EXPERT = 512
RMS_EPS = 1e-6
NEG = -1e30
ROPE_THETA = 10000.0

C_Q, C_K, C_V = 0, D_ATT, 2 * D_ATT
C_CU, C_CB, C_CC = 3 * D_ATT, 3 * D_ATT + D_CONV, 3 * D_ATT + 2 * D_CONV
C_ZA = 3 * D_ATT + 3 * D_CONV
C_ZB = C_ZA + D_MODEL

TS = 512
TQ = 2048
QB = 128
KW = QB + 2 * HALF
MOE_BLK = 256
TF = 256
TD = 512
ROUTER_ROWS = 48
VMEM_LIMIT = 56 * 1024 * 1024

F32 = jnp.float32
BF16 = jnp.bfloat16


def _rms(xf, g):
    return xf * lax.rsqrt(jnp.mean(xf * xf, axis=-1, keepdims=True) + RMS_EPS) * g


def _sigmoid(x):
    return 1.0 / (1.0 + jnp.exp(-x))


def _in_proj_kernel(x_ref, xp_ref, xn_ref, n1_ref, w_ref, bg_ref, cw_ref, wco_ref,
                    cos_ref, sa_ref, sb_ref,
                    q1, k1, v1, q2, k2, v2, q3, k3, v3, ga_ref, cp_ref,
                    pbuf, dbuf):
    j = pl.program_id(1)
    nj = pl.num_programs(1)
    g1 = n1_ref[...]
    h = _rms(x_ref[0], g1).astype(BF16)

    def proj(hh, c0, width):
        return jnp.dot(hh, w_ref[:, c0:c0 + width], preferred_element_type=F32)

    cosv = cos_ref[...]
    sav = sa_ref[...]
    sbv = sb_ref[...]

    def rotary(z):
        return z * cosv + pltpu.roll(z, 96, 1) * sav + pltpu.roll(z, 32, 1) * sbv

    def emit(outs, c0, rot, scale):
        for g, d in enumerate(DILATIONS):
            z = proj(h, c0 + g * D_GRP, D_GRP)
            if rot:
                z = jnp.concatenate([rotary(z[:, :128]), rotary(z[:, 128:])], axis=1)
            if scale != 1.0:
                z = z * scale
            if d == 1:
                outs[g][0, 0] = z.astype(BF16)
            else:
                dbuf[0] = z[:, :128]
                dbuf[1] = z[:, 128:]
                for r in range(d):
                    for c in range(2):
                        outs[g][0, r, :, c * 128:(c + 1) * 128] = (
                            dbuf[c, pl.ds(r, TS // d, stride=d), :].astype(BF16))

    emit((q1, q2, q3), C_Q, True, HEAD_DIM ** -0.5)
    emit((k1, k2, k3), C_K, True, 1.0)
    emit((v1, v2, v3), C_V, False, 1.0)

    p = proj(h, C_CC, D_CONV) * proj(h, C_CU, D_CONV)
    pbuf[8:8 + TS, :] = p
    hp = _rms(xp_ref[0], g1).astype(BF16)
    pp = proj(hp, C_CC, D_CONV) * proj(hp, C_CU, D_CONV)
    pbuf[0:8, :] = jnp.where(j > 0, pp, 0.0)
    hn = _rms(xn_ref[0], g1).astype(BF16)
    pn = proj(hn, C_CC, D_CONV) * proj(hn, C_CU, D_CONV)
    pbuf[8 + TS:16 + TS, :] = jnp.where(j < nj - 1, pn, 0.0)
    conv = (cw_ref[0:1, :] * pbuf[7:7 + TS, :] + cw_ref[1:2, :] * p
            + cw_ref[2:3, :] * pbuf[9:9 + TS, :])
    mix = (proj(h, C_CB, D_CONV) * conv).astype(BF16)
    conv_branch = jnp.dot(mix, wco_ref[...], preferred_element_type=F32)
    g_b = _sigmoid(proj(h, C_ZB, D_MODEL) + bg_ref[:, D_MODEL:])
    cp_ref[0] = g_b * conv_branch
    ga_ref[0] = _sigmoid(proj(h, C_ZA, D_MODEL) + bg_ref[:, :D_MODEL])


def _in_proj(x, n1, w_in, b_gate, conv_w, w_co, cos_t, sa_t, sb_t):
    B, S, D = x.shape
    nj = S // TS
    const = lambda shape: pl.BlockSpec(shape, lambda b, j: (0,) * len(shape),
                                       pipeline_mode=pl.Buffered(1))
    in_specs = [
        pl.BlockSpec((1, TS, D), lambda b, j: (b, j, 0)),
        pl.BlockSpec((1, 8, D), lambda b, j: (b, jnp.maximum(j * (TS // 8) - 1, 0), 0)),
        pl.BlockSpec((1, 8, D), lambda b, j: (b, jnp.minimum((j + 1) * (TS // 8), S // 8 - 1), 0)),
        const((1, D)),
        const(w_in.shape),
        const((1, 2 * D)),
        const((3, D_CONV)),
        const((D_CONV, D)),
        pl.BlockSpec((TS, 128), lambda b, j: (j, 0)),
        pl.BlockSpec((TS, 128), lambda b, j: (j, 0)),
        pl.BlockSpec((TS, 128), lambda b, j: (j, 0)),
    ]
    out_shape, out_specs = [], []
    for _ in range(3):
        for d in DILATIONS:
            out_shape.append(jax.ShapeDtypeStruct((B, d, S // d, D_GRP), BF16))
            out_specs.append(pl.BlockSpec((1, d, TS // d, D_GRP), lambda b, j: (b, 0, j, 0)))
    perm = [0, 3, 6, 1, 4, 7, 2, 5, 8]
    out_shape = [out_shape[i] for i in perm]
    out_specs = [out_specs[i] for i in perm]
    out_shape += [jax.ShapeDtypeStruct((B, S, D), F32), jax.ShapeDtypeStruct((B, S, D), F32)]
    out_specs += [pl.BlockSpec((1, TS, D), lambda b, j: (b, j, 0))] * 2
    return pl.pallas_call(
        _in_proj_kernel,
        grid=(B, nj),
        in_specs=in_specs,
        out_specs=out_specs,
        out_shape=out_shape,
        scratch_shapes=[pltpu.VMEM((TS + 16, D_CONV), F32), pltpu.VMEM((2, TS, 128), F32)],
        compiler_params=pltpu.CompilerParams(
            dimension_semantics=("parallel", "parallel"), vmem_limit_bytes=VMEM_LIMIT),
        name="in_proj",
    )(x, x, x, n1, w_in, b_gate, conv_w, w_co, cos_t, sa_t, sb_t)


def _attn_kernel(*refs):
    ins = refs[:21]
    o_ref = refs[21]
    kbufs = refs[22:28]
    os_ref, ls_ref = refs[28], refs[29]
    j = pl.program_id(1)
    nj = pl.num_programs(1)

    lane = lax.broadcasted_iota(jnp.int32, (QB, D_GRP), 1)
    head_of_lane = lane // HEAD_DIM
    row = lax.broadcasted_iota(jnp.int32, (HEADS_PER_GROUP * QB, KW), 0) % QB
    col = lax.broadcasted_iota(jnp.int32, (HEADS_PER_GROUP * QB, KW), 1)
    band = (col >= row) & (col <= row + 2 * HALF)

    for g, d in enumerate(DILATIONS):
        q_ref, kc, kp, kn, vc, vp, vn = ins[7 * g:7 * g + 7]
        kb, vb = kbufs[2 * g], kbufs[2 * g + 1]
        n = TQ // d
        nblk = n // QB
        for buf, prev, cur, nxt in ((kb, kp, kc, kn), (vb, vp, vc, vn)):
            buf[:, 0:HALF, :] = prev[0]
            buf[:, HALF:HALF + n, :] = cur[0]
            buf[:, HALF + n:, :] = nxt[0]

        def block(idx, carry, q_ref=q_ref, kb=kb, vb=vb, d=d, nblk=nblk, g=g):
            r = idx // nblk
            jb = idx % nblk
            base = pl.multiple_of(jb * QB, QB)
            qb = q_ref[0, r, pl.ds(base, QB), :]
            kw = kb[r, pl.ds(base, KW), :]
            vw = vb[r, pl.ds(base, KW), :]
            qs = jnp.concatenate(
                [jnp.where(head_of_lane == hh, qb, jnp.zeros_like(qb)) for hh in range(HEADS_PER_GROUP)],
                axis=0)
            s = lax.dot_general(qs, kw, (((1,), (1,)), ((), ())), preferred_element_type=F32)
            lo = jnp.where((j == 0) & (jb == 0), HALF, 0)
            hi = jnp.where((j == nj - 1) & (jb == nblk - 1), KW - HALF, KW)
            s = jnp.where(band & (col >= lo) & (col < hi), s, NEG)
            m = jnp.max(s, axis=1, keepdims=True)
            p = jnp.exp(s - m)
            l = jnp.sum(p, axis=1, keepdims=True)
            pv = jnp.dot(p.astype(BF16), vw, preferred_element_type=F32)
            lse = m + jnp.log(l)
            o = jnp.zeros((QB, D_GRP), F32)
            ls = jnp.zeros((QB, D_GRP), F32)
            for hh in range(HEADS_PER_GROUP):
                sl = slice(hh * QB, (hh + 1) * QB)
                sel = head_of_lane == hh
                o = jnp.where(sel, pv[sl] / l[sl], o)
                ls = jnp.where(sel, lse[sl], ls)
            for c in range(2):
                cs = slice(c * 128, (c + 1) * 128)
                if d == 1:
                    os_ref[2 * g + c, pl.ds(base, QB), :] = o[:, cs]
                    ls_ref[2 * g + c, pl.ds(base, QB), :] = ls[:, cs]
                else:
                    start = r + base * d
                    os_ref[2 * g + c, pl.ds(start, QB, stride=d), :] = o[:, cs]
                    ls_ref[2 * g + c, pl.ds(start, QB, stride=d), :] = ls[:, cs]
            return carry

        lax.fori_loop(0, d * nblk, block, 0)

    def merge(c, carry):
        rows = pl.ds(pl.multiple_of(c * QB, QB), QB)
        for c in range(2):
            l0, l1, l2 = ls_ref[c, rows, :], ls_ref[2 + c, rows, :], ls_ref[4 + c, rows, :]
            mm = jnp.maximum(jnp.maximum(l0, l1), l2)
            w0, w1, w2 = jnp.exp(l0 - mm), jnp.exp(l1 - mm), jnp.exp(l2 - mm)
            o = (w0 * os_ref[c, rows, :] + w1 * os_ref[2 + c, rows, :]
                 + w2 * os_ref[4 + c, rows, :]) / (w0 + w1 + w2)
            o_ref[0, rows, c * 128:(c + 1) * 128] = o.astype(BF16)
        return carry

    lax.fori_loop(0, TQ // QB, merge, 0)


def _attn(qkv):
    B = qkv[0].shape[0]
    S = qkv[0].shape[2]
    nj = S // TQ
    ins, in_specs, scratch = [], [], []
    for g, d in enumerate(DILATIONS):
        q, k, v = qkv[3 * g:3 * g + 3]
        n = TQ // d
        L = S // d
        nh = n // HALF
        cur = pl.BlockSpec((1, d, n, D_GRP), lambda b, j: (b, 0, j, 0))
        prev = pl.BlockSpec((1, d, HALF, D_GRP),
                            lambda b, j, nh=nh: (b, 0, jnp.maximum(j * nh - 1, 0), 0))
        nxt = pl.BlockSpec((1, d, HALF, D_GRP),
                           lambda b, j, nh=nh, L=L: (b, 0, jnp.minimum((j + 1) * nh, L // HALF - 1), 0))
        ins += [q, k, k, k, v, v, v]
        in_specs += [cur, cur, prev, nxt, cur, prev, nxt]
        scratch += [pltpu.VMEM((d, n + 2 * HALF, D_GRP), BF16)] * 2
    scratch += [pltpu.VMEM((6, TQ, 128), F32), pltpu.VMEM((6, TQ, 128), F32)]
    return pl.pallas_call(
        _attn_kernel,
        grid=(B, nj),
        in_specs=in_specs,
        out_specs=pl.BlockSpec((1, TQ, D_GRP), lambda b, j: (b, j, 0)),
        out_shape=jax.ShapeDtypeStruct((B, S, D_GRP), BF16),
        scratch_shapes=scratch,
        compiler_params=pltpu.CompilerParams(
            dimension_semantics=("parallel", "parallel"), vmem_limit_bytes=VMEM_LIMIT),
        name="attn",
    )(*ins)


def _post_kernel(o_ref, ga_ref, cp_ref, x_ref, wao_ref, wo_ref, n2_ref, wr_ref, br_ref, tri_ref,
                 x1_ref, h2_ref, rt_ref, cnt_ref, carry):
    first = (pl.program_id(0) == 0) & (pl.program_id(1) == 0)

    @pl.when(first)
    def _():
        carry[...] = jnp.zeros_like(carry)

    att = jnp.dot(o_ref[0], wao_ref[...], preferred_element_type=F32)
    merged = (ga_ref[0] * att + cp_ref[0]).astype(BF16)
    x1 = x_ref[0] + jnp.dot(merged, wo_ref[...], preferred_element_type=F32)
    x1_ref[0] = x1
    h2 = _rms(x1, n2_ref[...])
    h2_ref[0] = h2

    lt = lax.dot_general(wr_ref[...], h2.astype(BF16), (((1,), (1,)), ((), ())),
                         preferred_element_type=F32) + br_ref[:, 0:1]
    grow = lax.broadcasted_iota(jnp.int32, (8, TS), 0)
    gl = jnp.where(grow < N_EXPERT_GROUPS, lt[0:8], NEG)
    gmax = jnp.max(gl, axis=0, keepdims=True)
    grp = jnp.min(jnp.where(gl == gmax, grow, 8), axis=0, keepdims=True)
    grp_w = 1.0 / jnp.sum(jnp.exp(gl - gmax), axis=0, keepdims=True)
    erow = lax.broadcasted_iota(jnp.int32, (N_EXPERTS, TS), 0)
    el = jnp.where(erow // EXPERTS_PER_GROUP == grp, lt[8:8 + N_EXPERTS], NEG)
    v1 = jnp.max(el, axis=0, keepdims=True)
    i1 = jnp.min(jnp.where(el == v1, erow, N_EXPERTS), axis=0, keepdims=True)
    el2 = jnp.where(erow == i1, NEG, el)
    v2 = jnp.max(el2, axis=0, keepdims=True)
    i2 = jnp.min(jnp.where(el2 == v2, erow, N_EXPERTS), axis=0, keepdims=True)
    t = jnp.exp(v2 - v1)
    den = 1.0 + t
    gate1 = grp_w * (1.0 / den)
    gate2 = grp_w * (t / den)

    oh1 = (erow == i1).astype(F32)
    oh2 = (erow == i2).astype(F32)
    both = oh1 + oh2
    cum = jnp.dot(both.astype(BF16), tri_ref[...], preferred_element_type=F32)
    basec = carry[:, 0:1] + cum
    rank1 = jnp.sum(oh1 * basec, axis=0, keepdims=True)
    rank2 = jnp.sum(oh2 * basec, axis=0, keepdims=True)
    newc = carry[...] + jnp.sum(both, axis=1, keepdims=True)
    carry[...] = newc
    cnt_ref[...] = newc
    zero = jnp.zeros((1, TS), F32)
    rt_ref[0] = jnp.concatenate(
        [i1.astype(F32), i2.astype(F32), gate1, gate2, rank1, rank2, zero, zero], axis=0)


def _post(o_att, g_a, cpart, x, w_ao, w_o, n2, wr, br, tri):
    B, S, D = x.shape
    nj = S // TS
    const = lambda shape: pl.BlockSpec(shape, lambda b, j: (0,) * len(shape),
                                       pipeline_mode=pl.Buffered(1))
    tile = lambda w: pl.BlockSpec((1, TS, w), lambda b, j: (b, j, 0))
    return pl.pallas_call(
        _post_kernel,
        grid=(B, nj),
        in_specs=[tile(D_GRP), tile(D), tile(D), tile(D),
                  const((D_GRP, D)), const((D, D)), const((1, D)),
                  const((ROUTER_ROWS, D)), const((ROUTER_ROWS, 128)), const((TS, TS))],
        out_specs=[tile(D), tile(D),
                   pl.BlockSpec((1, 8, TS), lambda b, j: (b, 0, j)),
                   pl.BlockSpec((N_EXPERTS, 128), lambda b, j: (0, 0))],
        out_shape=[jax.ShapeDtypeStruct((B, S, D), F32), jax.ShapeDtypeStruct((B, S, D), F32),
                   jax.ShapeDtypeStruct((B, 8, S), F32),
                   jax.ShapeDtypeStruct((N_EXPERTS, 128), F32)],
        scratch_shapes=[pltpu.VMEM((N_EXPERTS, 128), F32)],
        compiler_params=pltpu.CompilerParams(
            dimension_semantics=("arbitrary", "arbitrary"), vmem_limit_bytes=VMEM_LIMIT),
        name="post",
    )(o_att, g_a, cpart, x, w_ao, w_o, n2, wr, br, tri)


def _row_gather_start(src_hbm, idx_ref, buf, sem, slot, count):
    def body(i, c):
        t = idx_ref[0, 0, i]
        pltpu.make_async_copy(src_hbm.at[pl.ds(t, 1)], buf.at[slot, pl.ds(i, 1)], sem.at[slot]).start()
        return c
    lax.fori_loop(0, count, body, 0, unroll=8)


def _row_gather_wait(src_hbm, buf, sem, slot, count):
    pltpu.make_async_copy(src_hbm.at[pl.ds(0, count)], buf.at[slot], sem.at[slot]).wait()


def _dispatch_kernel(d_ref, h_ref, xs_init_hbm, xs_hbm, buf, sem):
    del xs_init_hbm
    i = pl.program_id(0)
    n = pl.num_programs(0)
    slot = i % 2

    def wait_slot(s):
        for _ in range(2):
            pltpu.make_async_copy(buf.at[s], xs_hbm.at[pl.ds(0, TD)], sem.at[s]).wait()

    @pl.when(i >= 2)
    def _():
        wait_slot(slot)

    buf[slot] = h_ref[...]

    for k in range(2):
        def body(r, c, k=k):
            dst = d_ref[0, 0, k * TD + r]
            pltpu.make_async_copy(buf.at[slot, pl.ds(r, 1)], xs_hbm.at[pl.ds(dst, 1)], sem.at[slot]).start()
            return c
        lax.fori_loop(0, TD, body, 0, unroll=8)

    @pl.when(i == n - 1)
    def _():
        wait_slot(slot)
        wait_slot(1 - slot)


def _dispatch(dest, h2, nslots):
    T, D = h2.shape
    n = T // TD
    assert n >= 2
    return pl.pallas_call(
        _dispatch_kernel,
        grid=(n,),
        in_specs=[
            pl.BlockSpec((1, 1, 2 * TD), lambda i: (i, 0, 0), memory_space=pltpu.SMEM),
            pl.BlockSpec((TD, D), lambda i: (i, 0)),
            pl.BlockSpec(memory_space=pl.ANY),
        ],
        out_specs=pl.BlockSpec(memory_space=pl.ANY),
        out_shape=jax.ShapeDtypeStruct((nslots, D), F32),
        input_output_aliases={2: 0},
        scratch_shapes=[pltpu.VMEM((2, TD, D), F32), pltpu.SemaphoreType.DMA((2,))],
        compiler_params=pltpu.CompilerParams(
            dimension_semantics=("arbitrary",), vmem_limit_bytes=VMEM_LIMIT),
        name="dispatch",
    )(dest, h2, jnp.zeros((nslots, D), F32))


def _moe_kernel(be_ref, nu_ref, nv_ref, x_ref, wg_ref, wu_ref, wd_ref, out_ref):
    b = pl.program_id(0)

    @pl.when(b < nu_ref[0])
    def _():
        live = lax.broadcasted_iota(jnp.int32, (MOE_BLK, 1), 0) < nv_ref[b]
        x = jnp.where(live, x_ref[...], 0.0).astype(BF16)
        a = jnp.dot(x, wg_ref[0], preferred_element_type=F32)
        u = jnp.dot(x, wu_ref[0], preferred_element_type=F32)
        hm = (a * _sigmoid(a) * u).astype(BF16)
        out_ref[...] = jnp.dot(hm, wd_ref[0], preferred_element_type=F32)

    @pl.when(b >= nu_ref[0])
    def _():
        out_ref[...] = jnp.zeros_like(out_ref)


def _moe(block_e, nused, nvalid, xs, w_g, w_u, w_d):
    nb = block_e.shape[0]
    D = xs.shape[1]
    grid_spec = pltpu.PrefetchScalarGridSpec(
        num_scalar_prefetch=3,
        grid=(nb,),
        in_specs=[
            pl.BlockSpec((MOE_BLK, D), lambda b, be, nu, nv: (b, 0)),
            pl.BlockSpec((1, D, D_EXPERT), lambda b, be, nu, nv: (be[b], 0, 0)),
            pl.BlockSpec((1, D, D_EXPERT), lambda b, be, nu, nv: (be[b], 0, 0)),
            pl.BlockSpec((1, D_EXPERT, D), lambda b, be, nu, nv: (be[b], 0, 0)),
        ],
        out_specs=pl.BlockSpec((MOE_BLK, D), lambda b, be, nu, nv: (b, 0)),
    )
    return pl.pallas_call(
        _moe_kernel,
        grid_spec=grid_spec,
        out_shape=jax.ShapeDtypeStruct((nb * MOE_BLK, D), F32),
        compiler_params=pltpu.CompilerParams(
            dimension_semantics=("arbitrary",), vmem_limit_bytes=VMEM_LIMIT),
        name="moe",
    )(block_e, nused, nvalid, xs, w_g, w_u, w_d)


def _final_kernel(d_ref, dn_ref, x1_ref, g_ref, gf_ref, yb_hbm, out_ref, ybuf, sem):
    i = pl.program_id(0)
    n = pl.num_programs(0)
    slot = i % 2

    @pl.when(i == 0)
    def _():
        _row_gather_start(yb_hbm, d_ref, ybuf, sem, 0, 2 * TF)

    @pl.when(i + 1 < n)
    def _():
        _row_gather_start(yb_hbm, dn_ref, ybuf, sem, 1 - slot, 2 * TF)

    _row_gather_wait(yb_hbm, ybuf, sem, slot, 2 * TF)
    y = g_ref[:, 0:1] * ybuf[slot, 0:TF, :] + g_ref[:, 1:2] * ybuf[slot, TF:2 * TF, :]
    out_ref[...] = _rms(x1_ref[...] + y, gf_ref[...])


def _final(dest, gates, x1, gf, yb):
    T, D = x1.shape
    n = T // TF
    return pl.pallas_call(
        _final_kernel,
        grid=(n,),
        in_specs=[
            pl.BlockSpec((1, 1, 2 * TF), lambda i: (i, 0, 0), memory_space=pltpu.SMEM),
            pl.BlockSpec((1, 1, 2 * TF), lambda i: (jnp.minimum(i + 1, n - 1), 0, 0),
                         memory_space=pltpu.SMEM),
            pl.BlockSpec((TF, D), lambda i: (i, 0)),
            pl.BlockSpec((TF, 2), lambda i: (i, 0)),
            pl.BlockSpec((1, D), lambda i: (0, 0)),
            pl.BlockSpec(memory_space=pl.ANY),
        ],
        out_specs=pl.BlockSpec((TF, D), lambda i: (i, 0)),
        out_shape=jax.ShapeDtypeStruct((T, D), F32),
        scratch_shapes=[pltpu.VMEM((2, 2 * TF, D), F32), pltpu.SemaphoreType.DMA((2,))],
        compiler_params=pltpu.CompilerParams(
            dimension_semantics=("arbitrary",), vmem_limit_bytes=VMEM_LIMIT),
        name="final",
    )(dest, dest, x1, gates, gf, yb)


SC_ROWS = 32


def _sc_mesh():
    return plsc.VectorSubcoreMesh(core_axis_name="c", subcore_axis_name="s")


def _sc_worker_range(total):
    info = plsc.get_sparse_core_info()
    nw = info.num_cores * info.num_subcores
    wid = lax.axis_index("s") * info.num_cores + lax.axis_index("c")
    per_w = total // nw
    assert per_w * nw == total and per_w % SC_ROWS == 0
    return wid * per_w, per_w // SC_ROWS


def _sc_scatter_rows(src, idx_a, idx_b, nslots):
    T, D = src.shape

    @functools.partial(
        pl.kernel, out_type=jax.ShapeDtypeStruct((nslots, D), src.dtype), mesh=_sc_mesh(),
        scratch_types=[pltpu.VMEM((SC_ROWS,), jnp.int32), pltpu.VMEM((SC_ROWS,), jnp.int32),
                       pltpu.VMEM((SC_ROWS, D), src.dtype), pltpu.SemaphoreType.DMA],
        name="sc_dispatch")
    def k(x_hbm, ia_hbm, ib_hbm, o_hbm, ia_v, ib_v, rows_v, sem):
        start, nchunks = _sc_worker_range(T)

        @pl.loop(0, nchunks)
        def _(c):
            rows = pl.ds(start + c * SC_ROWS, SC_ROWS)
            pltpu.sync_copy(ia_hbm.at[rows], ia_v)
            pltpu.sync_copy(ib_hbm.at[rows], ib_v)
            pltpu.sync_copy(x_hbm.at[rows], rows_v)
            ca = pltpu.async_copy(rows_v, o_hbm.at[ia_v], sem)
            cb = pltpu.async_copy(rows_v, o_hbm.at[ib_v], sem)
            ca.wait()
            cb.wait()

    return k(src, idx_a, idx_b)


def _sc_gather_rows(src, idx_a, idx_b):
    D = src.shape[1]
    T = idx_a.shape[0]
    out = jax.ShapeDtypeStruct((T, D), src.dtype)

    @functools.partial(
        pl.kernel, out_type=(out, out), mesh=_sc_mesh(),
        scratch_types=[pltpu.VMEM((SC_ROWS,), jnp.int32), pltpu.VMEM((SC_ROWS,), jnp.int32),
                       pltpu.VMEM((SC_ROWS, D), src.dtype), pltpu.VMEM((SC_ROWS, D), src.dtype),
                       pltpu.SemaphoreType.DMA],
        name="sc_combine")
    def k(x_hbm, ia_hbm, ib_hbm, oa_hbm, ob_hbm, ia_v, ib_v, ra_v, rb_v, sem):
        start, nchunks = _sc_worker_range(T)

        @pl.loop(0, nchunks)
        def _(c):
            rows = pl.ds(start + c * SC_ROWS, SC_ROWS)
            pltpu.sync_copy(ia_hbm.at[rows], ia_v)
            pltpu.sync_copy(ib_hbm.at[rows], ib_v)
            ca = pltpu.async_copy(x_hbm.at[ia_v], ra_v, sem)
            cb = pltpu.async_copy(x_hbm.at[ib_v], rb_v, sem)
            ca.wait()
            cb.wait()
            pltpu.sync_copy(ra_v, oa_hbm.at[rows])
            pltpu.sync_copy(rb_v, ob_hbm.at[rows])

    return k(src, idx_a, idx_b)


def _combine_kernel(x1_ref, ya_ref, yb_ref, g_ref, gf_ref, out_ref):
    y = g_ref[:, 0:1] * ya_ref[...] + g_ref[:, 1:2] * yb_ref[...]
    out_ref[...] = _rms(x1_ref[...] + y, gf_ref[...])


def _combine(x1, ya, yb, gates, gf):
    T, D = x1.shape
    tile = pl.BlockSpec((TS, D), lambda i: (i, 0))
    return pl.pallas_call(
        _combine_kernel,
        grid=(T // TS,),
        in_specs=[tile, tile, tile, pl.BlockSpec((TS, 2), lambda i: (i, 0)),
                  pl.BlockSpec((1, D), lambda i: (0, 0))],
        out_specs=tile,
        out_shape=jax.ShapeDtypeStruct((T, D), F32),
        compiler_params=pltpu.CompilerParams(
            dimension_semantics=("parallel",), vmem_limit_bytes=VMEM_LIMIT),
        name="combine",
    )(x1, ya, yb, gates, gf)


def _rotary_tables(S):
    inv_freq = 1.0 / (ROPE_THETA ** (jnp.arange(0, HEAD_DIM, 2, dtype=F32) / HEAD_DIM))
    ang = jnp.arange(S, dtype=F32)[:, None] * inv_freq[None, :]
    cos, sin = jnp.cos(ang), jnp.sin(ang)
    z = jnp.zeros_like(sin)
    cos_t = jnp.concatenate([cos, cos, cos, cos], axis=1)
    sa_t = jnp.concatenate([-sin, z, -sin, z], axis=1)
    sb_t = jnp.concatenate([z, sin, z, sin], axis=1)
    return cos_t, sa_t, sb_t


def _trunk(x, p):
    B, S, D = x.shape
    T = B * S
    outs = _in_proj(x, p["n1"], p["w_in"], p["b_gate"], p["conv_w"], p["w_co"], *p["rot"])
    qkv, g_a, cpart = outs[:9], outs[9], outs[10]
    o_att = _attn(qkv)
    x1, h2, rt, cnt = _post(o_att, g_a, cpart, x, p["w_ao"], p["w_o"], p["n2"], p["wr"], p["br"], p["tri"])

    e1 = rt[:, 0, :].reshape(T).astype(jnp.int32)
    e2 = rt[:, 1, :].reshape(T).astype(jnp.int32)
    gates = jnp.stack([rt[:, 2, :].reshape(T), rt[:, 3, :].reshape(T)], axis=1)
    r1 = rt[:, 4, :].reshape(T).astype(jnp.int32)
    r2 = rt[:, 5, :].reshape(T).astype(jnp.int32)
    counts = cnt[:, 0].astype(jnp.int32)
    pcounts = (counts + MOE_BLK - 1) // MOE_BLK * MOE_BLK
    pends = jnp.cumsum(pcounts)
    pstarts = pends - pcounts
    d1 = pstarts[e1] + r1
    d2 = pstarts[e2] + r2
    nb = (2 * T) // MOE_BLK + N_EXPERTS
    block_start = jnp.arange(nb, dtype=jnp.int32) * MOE_BLK
    block_e = jnp.minimum(jnp.sum((pends[None, :] <= block_start[:, None]).astype(jnp.int32), axis=1),
                          N_EXPERTS - 1)
    nused = (pends[-1:] // MOE_BLK).astype(jnp.int32)

    nvalid = jnp.clip(counts[block_e] - (block_start - pstarts[block_e]), 0, MOE_BLK)

    xs = _sc_scatter_rows(h2.reshape(T, D), d1, d2, nb * MOE_BLK)
    yb = _moe(block_e, nused, nvalid, xs, p["w_g"], p["w_u"], p["w_d"])
    ya, yc = _sc_gather_rows(yb, d1, d2)
    y = _combine(x1.reshape(T, D), ya, yc, gates, p["gf"])
    return y.reshape(B, S, D)


def kernel(x_prompt, x_sample, norm1_g, w_in, b_gate, conv_w, w_attn_out, w_conv_out, w_out, norm2_g,
           w_router_group, b_router_group, w_router_expert, b_router_expert, w_exp_gate, w_exp_up,
           w_exp_down, norm_f_g):
    assert norm1_g.shape[0] == 1, "single-layer trunk"
    S = x_prompt.shape[1]
    wr = jnp.zeros((ROUTER_ROWS, D_MODEL), F32)
    wr = wr.at[0:N_EXPERT_GROUPS].set(w_router_group[0].T).at[8:8 + N_EXPERTS].set(w_router_expert[0].T)
    br = jnp.zeros((ROUTER_ROWS,), F32)
    br = br.at[0:N_EXPERT_GROUPS].set(b_router_group[0]).at[8:8 + N_EXPERTS].set(b_router_expert[0])
    ti = jnp.arange(TS)
    p = dict(
        n1=norm1_g, w_in=w_in[0].astype(BF16), b_gate=b_gate, conv_w=conv_w[0],
        w_co=w_conv_out[0].astype(BF16), rot=_rotary_tables(S),
        w_ao=w_attn_out[0].astype(BF16), w_o=w_out[0].astype(BF16), n2=norm2_g,
        wr=wr.astype(BF16), br=jnp.broadcast_to(br[:, None], (ROUTER_ROWS, 128)),
        tri=(ti[:, None] < ti[None, :]).astype(BF16),
        w_g=w_exp_gate[0].astype(BF16), w_u=w_exp_up[0].astype(BF16), w_d=w_exp_down[0].astype(BF16),
        gf=norm_f_g.reshape(1, D_MODEL),
    )
    return _trunk(x_prompt, p), _trunk(x_sample, p)
```

```python
import functools

import jax
import jax.numpy as jnp
from jax import lax
from jax.experimental import pallas as pl
from jax.experimental.pallas import tpu as pltpu
from jax.experimental.pallas import tpu_sc as plsc

D_MODEL = 1024
HEAD_DIM = 64
HEADS_PER_GROUP = 4
DILATIONS = (1, 4, 16)
HALF = 64
D_GRP = HEADS_PER_GROUP * HEAD_DIM
D_ATT = 3 * D_GRP
D_CONV = 768
N_EXPERT_GROUPS = 4
EXPERTS_PER_GROUP = 8
N_EXPERTS = 32
D_EXPERT = 512
RMS_EPS = 1e-6
NEG = -1e30
ROPE_THETA = 10000.0

C_Q, C_K, C_V = 0, D_ATT, 2 * D_ATT
C_CU, C_CB, C_CC = 3 * D_ATT, 3 * D_ATT + D_CONV, 3 * D_ATT + 2 * D_CONV
C_ZA = 3 * D_ATT + 3 * D_CONV

TSI = 1024
TS = 1024
TQ = 2048
QB = 128
KW = QB + 2 * HALF
MOE_BLK = 512
SC_ROWS = 32
ROUTER_ROWS = 48
VMEM_LIMIT = 56 * 1024 * 1024

F32 = jnp.float32
BF16 = jnp.bfloat16


def _rms(xf, g):
    return xf * lax.rsqrt(jnp.mean(xf * xf, axis=-1, keepdims=True) + RMS_EPS) * g


def _sigmoid(x):
    return 1.0 / (1.0 + jnp.exp(-x))


def _const_spec(shape):
    return pl.BlockSpec(shape, lambda *_: (0,) * len(shape), pipeline_mode=pl.Buffered(1))


def _in_proj_kernel(x_ref, xp_ref, xn_ref, n1_ref, w_ref, cw_ref, cos_ref, sa_ref, sb_ref,
                    q1, k1, v1, q2, k2, v2, q3, k3, v3, mix_ref, pbuf, dbuf):
    j = pl.program_id(1)
    nj = pl.num_programs(1)
    g1 = n1_ref[...]
    h = _rms(x_ref[0], g1).astype(BF16)

    def proj(hh, c0, width):
        return jnp.dot(hh, w_ref[:, c0:c0 + width], preferred_element_type=F32)

    cosv = cos_ref[...]
    sav = sa_ref[...]
    sbv = sb_ref[...]

    def rotary(z):
        return z * cosv + pltpu.roll(z, 96, 1) * sav + pltpu.roll(z, 32, 1) * sbv

    def emit(outs, c0, rot, scale):
        for g, d in enumerate(DILATIONS):
            z = proj(h, c0 + g * D_GRP, D_GRP)
            if rot:
                z = jnp.concatenate([rotary(z[:, :128]), rotary(z[:, 128:])], axis=1)
            if scale != 1.0:
                z = z * scale
            if d == 1:
                outs[g][0, 0] = z.astype(BF16)
            else:
                dbuf[0] = z[:, :128]
                dbuf[1] = z[:, 128:]
                for r in range(d):
                    for c in range(2):
                        outs[g][0, r, :, c * 128:(c + 1) * 128] = (
                            dbuf[c, pl.ds(r, TSI // d, stride=d), :].astype(BF16))

    emit((q1, q2, q3), C_Q, True, HEAD_DIM ** -0.5)
    emit((k1, k2, k3), C_K, True, 1.0)
    emit((v1, v2, v3), C_V, False, 1.0)

    p = proj(h, C_CC, D_CONV) * proj(h, C_CU, D_CONV)
    pbuf[8:8 + TSI, :] = p
    hp = _rms(xp_ref[0], g1).astype(BF16)
    pp = proj(hp, C_CC, D_CONV) * proj(hp, C_CU, D_CONV)
    pbuf[0:8, :] = jnp.where(j > 0, pp, 0.0)
    hn = _rms(xn_ref[0], g1).astype(BF16)
    pn = proj(hn, C_CC, D_CONV) * proj(hn, C_CU, D_CONV)
    pbuf[8 + TSI:16 + TSI, :] = jnp.where(j < nj - 1, pn, 0.0)
    conv = (cw_ref[0:1, :] * pbuf[7:7 + TSI, :] + cw_ref[1:2, :] * p
            + cw_ref[2:3, :] * pbuf[9:9 + TSI, :])
    mix_ref[0] = (proj(h, C_CB, D_CONV) * conv).astype(BF16)


def _in_proj(x, n1, w_qkvc, conv_w, cos_t, sa_t, sb_t):
    B, S, D = x.shape
    nj = S // TSI
    rot_spec = pl.BlockSpec((TSI, 128), lambda b, j: (j, 0))
    in_specs = [
        pl.BlockSpec((1, TSI, D), lambda b, j: (b, j, 0)),
        pl.BlockSpec((1, 8, D), lambda b, j: (b, jnp.maximum(j * (TSI // 8) - 1, 0), 0)),
        pl.BlockSpec((1, 8, D), lambda b, j: (b, jnp.minimum((j + 1) * (TSI // 8), S // 8 - 1), 0)),
        _const_spec((1, D)),
        _const_spec(w_qkvc.shape),
        _const_spec((3, D_CONV)),
        rot_spec, rot_spec, rot_spec,
    ]
    out_shape, out_specs = [], []
    for d in DILATIONS:
        for _ in range(3):
            out_shape.append(jax.ShapeDtypeStruct((B, d, S // d, D_GRP), BF16))
            out_specs.append(pl.BlockSpec((1, d, TSI // d, D_GRP), lambda b, j: (b, 0, j, 0)))
    out_shape.append(jax.ShapeDtypeStruct((B, S, D_CONV), BF16))
    out_specs.append(pl.BlockSpec((1, TSI, D_CONV), lambda b, j: (b, j, 0)))
    return pl.pallas_call(
        _in_proj_kernel,
        grid=(B, nj),
        in_specs=in_specs,
        out_specs=out_specs,
        out_shape=out_shape,
        scratch_shapes=[pltpu.VMEM((TSI + 16, D_CONV), F32), pltpu.VMEM((2, TSI, 128), F32)],
        compiler_params=pltpu.CompilerParams(
            dimension_semantics=("parallel", "parallel"), vmem_limit_bytes=VMEM_LIMIT),
        name="in_proj",
    )(x, x, x, n1, w_qkvc, conv_w, cos_t, sa_t, sb_t)


def _attn_kernel(*refs):
    ins = refs[:21]
    bias_ref = refs[21]
    o_ref = refs[22]
    kbufs = refs[23:29]
    os_ref, ls_ref = refs[29], refs[30]
    j = pl.program_id(1)
    nj = pl.num_programs(1)

    lane = lax.broadcasted_iota(jnp.int32, (QB, D_GRP), 1)
    head_of_lane = lane // HEAD_DIM

    for g, d in enumerate(DILATIONS):
        q_ref, kc, kp, kn, vc, vp, vn = ins[7 * g:7 * g + 7]
        kb, vb = kbufs[2 * g], kbufs[2 * g + 1]
        n = TQ // d
        nblk = n // QB
        for buf, prev, cur, nxt in ((kb, kp, kc, kn), (vb, vp, vc, vn)):
            buf[:, 0:HALF, :] = prev[0]
            buf[:, HALF:HALF + n, :] = cur[0]
            buf[:, HALF + n:, :] = nxt[0]

        def block(idx, carry, q_ref=q_ref, kb=kb, vb=vb, d=d, nblk=nblk, g=g):
            r = idx // nblk
            jb = idx % nblk
            base = pl.multiple_of(jb * QB, QB)
            qb = q_ref[0, r, pl.ds(base, QB), :]
            kw = kb[r, pl.ds(base, KW), :]
            vw = vb[r, pl.ds(base, KW), :]
            qs = jnp.concatenate(
                [jnp.where(head_of_lane == hh, qb, jnp.zeros_like(qb)) for hh in range(HEADS_PER_GROUP)],
                axis=0)
            s = lax.dot_general(qs, kw, (((1,), (1,)), ((), ())), preferred_element_type=F32)
            variant = jnp.where((j == 0) & (jb == 0), 1, jnp.where((j == nj - 1) & (jb == nblk - 1), 2, 0))
            s = s + bias_ref[variant]
            m = jnp.max(s, axis=1, keepdims=True)
            p = jnp.exp(s - m)
            l = jnp.sum(p, axis=1, keepdims=True)
            pv = jnp.dot(p.astype(BF16), vw, preferred_element_type=F32)
            lse = m + jnp.log(l)
            o = jnp.zeros((QB, D_GRP), F32)
            ls = jnp.zeros((QB, D_GRP), F32)
            for hh in range(HEADS_PER_GROUP):
                sl = slice(hh * QB, (hh + 1) * QB)
                sel = head_of_lane == hh
                o = jnp.where(sel, pv[sl] / l[sl], o)
                ls = jnp.where(sel, lse[sl], ls)
            for c in range(2):
                cs = slice(c * 128, (c + 1) * 128)
                if d == 1:
                    os_ref[2 * g + c, pl.ds(base, QB), :] = o[:, cs]
                    ls_ref[2 * g + c, pl.ds(base, QB), :] = ls[:, cs]
                else:
                    start = r + base * d
                    os_ref[2 * g + c, pl.ds(start, QB, stride=d), :] = o[:, cs]
                    ls_ref[2 * g + c, pl.ds(start, QB, stride=d), :] = ls[:, cs]
            return carry

        lax.fori_loop(0, d * nblk, block, 0, unroll=2)

    def merge(c, carry):
        rows = pl.ds(pl.multiple_of(c * QB, QB), QB)
        for c in range(2):
            l0, l1, l2 = ls_ref[c, rows, :], ls_ref[2 + c, rows, :], ls_ref[4 + c, rows, :]
            mm = jnp.maximum(jnp.maximum(l0, l1), l2)
            w0, w1, w2 = jnp.exp(l0 - mm), jnp.exp(l1 - mm), jnp.exp(l2 - mm)
            o = (w0 * os_ref[c, rows, :] + w1 * os_ref[2 + c, rows, :]
                 + w2 * os_ref[4 + c, rows, :]) / (w0 + w1 + w2)
            o_ref[0, rows, c * 128:(c + 1) * 128] = o.astype(BF16)
        return carry

    lax.fori_loop(0, TQ // QB, merge, 0)


def _band_bias():
    row = jnp.arange(HEADS_PER_GROUP * QB)[:, None] % QB
    col = jnp.arange(KW)[None, :]
    band = (col >= row) & (col <= row + 2 * HALF)
    variants = (band, band & (col >= HALF), band & (col < KW - HALF))
    return jnp.stack([jnp.where(v, 0.0, NEG).astype(F32) for v in variants])


def _attn(qkv):
    B = qkv[0].shape[0]
    S = qkv[0].shape[2]
    nj = S // TQ
    assert nj > 1
    ins, in_specs, scratch = [], [], []
    for g, d in enumerate(DILATIONS):
        q, k, v = qkv[3 * g:3 * g + 3]
        n = TQ // d
        L = S // d
        nh = n // HALF
        cur = pl.BlockSpec((1, d, n, D_GRP), lambda b, j: (b, 0, j, 0))
        prev = pl.BlockSpec((1, d, HALF, D_GRP),
                            lambda b, j, nh=nh: (b, 0, jnp.maximum(j * nh - 1, 0), 0))
        nxt = pl.BlockSpec((1, d, HALF, D_GRP),
                           lambda b, j, nh=nh, L=L: (b, 0, jnp.minimum((j + 1) * nh, L // HALF - 1), 0))
        ins += [q, k, k, k, v, v, v]
        in_specs += [cur, cur, prev, nxt, cur, prev, nxt]
        scratch += [pltpu.VMEM((d, n + 2 * HALF, D_GRP), BF16)] * 2
    scratch += [pltpu.VMEM((6, TQ, 128), F32), pltpu.VMEM((6, TQ, 128), F32)]
    ins.append(_band_bias())
    in_specs.append(_const_spec((3, HEADS_PER_GROUP * QB, KW)))
    return pl.pallas_call(
        _attn_kernel,
        grid=(B, nj),
        in_specs=in_specs,
        out_specs=pl.BlockSpec((1, TQ, D_GRP), lambda b, j: (b, j, 0)),
        out_shape=jax.ShapeDtypeStruct((B, S, D_GRP), BF16),
        scratch_shapes=scratch,
        compiler_params=pltpu.CompilerParams(
            dimension_semantics=("parallel", "parallel"), vmem_limit_bytes=VMEM_LIMIT),
        name="attn",
    )(*ins)


def _post_kernel(o_ref, mix_ref, x_ref, n1_ref, wz_ref, bg_ref, wco_ref, wao_ref, wo_ref, n2_ref,
                 wr_ref, br_ref, tri_ref, x1_ref, h2_ref, rt_ref, cnt_ref, carry):
    first = (pl.program_id(0) == 0) & (pl.program_id(1) == 0)

    @pl.when(first)
    def _():
        carry[...] = jnp.zeros_like(carry)

    x = x_ref[0]
    h = _rms(x, n1_ref[...]).astype(BF16)
    g_a = _sigmoid(jnp.dot(h, wz_ref[:, :D_MODEL], preferred_element_type=F32) + bg_ref[:, :D_MODEL])
    g_b = _sigmoid(jnp.dot(h, wz_ref[:, D_MODEL:], preferred_element_type=F32) + bg_ref[:, D_MODEL:])
    att = jnp.dot(o_ref[0], wao_ref[...], preferred_element_type=F32)
    cvb = jnp.dot(mix_ref[0], wco_ref[...], preferred_element_type=F32)
    merged = (g_a * att + g_b * cvb).astype(BF16)
    x1 = x + jnp.dot(merged, wo_ref[...], preferred_element_type=F32)
    x1_ref[0] = x1
    h2 = _rms(x1, n2_ref[...])
    h2_ref[0] = h2

    lt = lax.dot_general(wr_ref[...], h2.astype(BF16), (((1,), (1,)), ((), ())),
                         preferred_element_type=F32) + br_ref[:, 0:1]
    grow = lax.broadcasted_iota(jnp.int32, (8, TS), 0)
    gl = jnp.where(grow < N_EXPERT_GROUPS, lt[0:8], NEG)
    gmax = jnp.max(gl, axis=0, keepdims=True)
    grp = jnp.min(jnp.where(gl == gmax, grow, 8), axis=0, keepdims=True)
    grp_w = 1.0 / jnp.sum(jnp.exp(gl - gmax), axis=0, keepdims=True)
    erow = lax.broadcasted_iota(jnp.int32, (N_EXPERTS, TS), 0)
    el = jnp.where(erow // EXPERTS_PER_GROUP == grp, lt[8:8 + N_EXPERTS], NEG)
    v1 = jnp.max(el, axis=0, keepdims=True)
    i1 = jnp.min(jnp.where(el == v1, erow, N_EXPERTS), axis=0, keepdims=True)
    el2 = jnp.where(erow == i1, NEG, el)
    v2 = jnp.max(el2, axis=0, keepdims=True)
    i2 = jnp.min(jnp.where(el2 == v2, erow, N_EXPERTS), axis=0, keepdims=True)
    t = jnp.exp(v2 - v1)
    den = 1.0 + t
    gate1 = grp_w * (1.0 / den)
    gate2 = grp_w * (t / den)

    oh1 = (erow == i1).astype(F32)
    oh2 = (erow == i2).astype(F32)
    both = oh1 + oh2
    cum = jnp.dot(both.astype(BF16), tri_ref[...], preferred_element_type=F32)
    basec = carry[:, 0:1] + cum
    rank1 = jnp.sum(oh1 * basec, axis=0, keepdims=True)
    rank2 = jnp.sum(oh2 * basec, axis=0, keepdims=True)
    newc = carry[...] + jnp.sum(both, axis=1, keepdims=True)
    carry[...] = newc
    cnt_ref[...] = newc
    zero = jnp.zeros((1, TS), F32)
    rt_ref[0] = jnp.concatenate(
        [i1.astype(F32), i2.astype(F32), gate1, gate2, rank1, rank2, zero, zero], axis=0)


def _post(o_att, mix, x, p):
    B, S, D = x.shape
    nj = S // TS
    tile = lambda w: pl.BlockSpec((1, TS, w), lambda b, j: (b, j, 0))
    return pl.pallas_call(
        _post_kernel,
        grid=(B, nj),
        in_specs=[tile(D_GRP), tile(D_CONV), tile(D),
                  _const_spec((1, D)), _const_spec((D, 2 * D)), _const_spec((1, 2 * D)),
                  _const_spec((D_CONV, D)), _const_spec((D_GRP, D)), _const_spec((D, D)),
                  _const_spec((1, D)), _const_spec((ROUTER_ROWS, D)), _const_spec((ROUTER_ROWS, 128)),
                  _const_spec((TS, TS))],
        out_specs=[tile(D), tile(D),
                   pl.BlockSpec((1, 8, TS), lambda b, j: (b, 0, j)),
                   pl.BlockSpec((N_EXPERTS, 128), lambda b, j: (0, 0))],
        out_shape=[jax.ShapeDtypeStruct((B, S, D), F32), jax.ShapeDtypeStruct((B, S, D), F32),
                   jax.ShapeDtypeStruct((B, 8, S), F32),
                   jax.ShapeDtypeStruct((N_EXPERTS, 128), F32)],
        scratch_shapes=[pltpu.VMEM((N_EXPERTS, 128), F32)],
        compiler_params=pltpu.CompilerParams(
            dimension_semantics=("arbitrary", "arbitrary"), vmem_limit_bytes=VMEM_LIMIT),
        name="post",
    )(o_att, mix, x, p["n1"], p["w_gate"], p["b_gate"], p["w_co"], p["w_ao"], p["w_o"], p["n2"],
      p["wr"], p["br"], p["tri"])


def _moe_kernel(be_ref, nu_ref, nv_ref, x_ref, wg_ref, wu_ref, wd_ref, out_ref, wgb, wub, wdb):
    b = pl.program_id(0)

    @pl.when(b < nu_ref[0])
    def _():
        @pl.when((b == 0) | (be_ref[b] != be_ref[jnp.maximum(b - 1, 0)]))
        def _():
            wgb[...] = wg_ref[0].astype(BF16)
            wub[...] = wu_ref[0].astype(BF16)
            wdb[...] = wd_ref[0].astype(BF16)

        live = lax.broadcasted_iota(jnp.int32, (MOE_BLK, 1), 0) < nv_ref[b]
        x = jnp.where(live, x_ref[...], 0.0).astype(BF16)
        a = jnp.dot(x, wgb[...], preferred_element_type=F32)
        u = jnp.dot(x, wub[...], preferred_element_type=F32)
        hm = (a * _sigmoid(a) * u).astype(BF16)
        out_ref[...] = jnp.dot(hm, wdb[...], preferred_element_type=F32)

    @pl.when(b >= nu_ref[0])
    def _():
        out_ref[...] = jnp.zeros_like(out_ref)


def _moe(block_e, nused, nvalid, xs, w_g, w_u, w_d):
    nb = block_e.shape[0]
    D = xs.shape[1]
    grid_spec = pltpu.PrefetchScalarGridSpec(
        num_scalar_prefetch=3,
        grid=(nb,),
        in_specs=[
            pl.BlockSpec((MOE_BLK, D), lambda b, be, nu, nv: (b, 0)),
            pl.BlockSpec((1, D, D_EXPERT), lambda b, be, nu, nv: (be[b], 0, 0)),
            pl.BlockSpec((1, D, D_EXPERT), lambda b, be, nu, nv: (be[b], 0, 0)),
            pl.BlockSpec((1, D_EXPERT, D), lambda b, be, nu, nv: (be[b], 0, 0)),
        ],
        out_specs=pl.BlockSpec((MOE_BLK, D), lambda b, be, nu, nv: (b, 0)),
        scratch_shapes=[pltpu.VMEM((D, D_EXPERT), BF16), pltpu.VMEM((D, D_EXPERT), BF16),
                        pltpu.VMEM((D_EXPERT, D), BF16)],
    )
    return pl.pallas_call(
        _moe_kernel,
        grid_spec=grid_spec,
        out_shape=jax.ShapeDtypeStruct((nb * MOE_BLK, D), F32),
        compiler_params=pltpu.CompilerParams(
            dimension_semantics=("arbitrary",), vmem_limit_bytes=VMEM_LIMIT),
        name="moe",
    )(block_e, nused, nvalid, xs, w_g, w_u, w_d)


def _sc_mesh():
    return plsc.VectorSubcoreMesh(core_axis_name="c", subcore_axis_name="s")


def _sc_worker_range(total):
    info = plsc.get_sparse_core_info()
    nw = info.num_cores * info.num_subcores
    wid = lax.axis_index("s") * info.num_cores + lax.axis_index("c")
    per_w = total // nw
    assert per_w * nw == total and per_w % SC_ROWS == 0
    return wid * per_w, per_w // SC_ROWS


def _sc_scatter_rows(src, idx_a, idx_b, nslots):
    T, D = src.shape

    @functools.partial(
        pl.kernel, out_type=jax.ShapeDtypeStruct((nslots, D), src.dtype), mesh=_sc_mesh(),
        scratch_types=[pltpu.VMEM((SC_ROWS,), jnp.int32), pltpu.VMEM((SC_ROWS,), jnp.int32),
                       pltpu.VMEM((SC_ROWS, D), src.dtype), pltpu.SemaphoreType.DMA],
        name="sc_dispatch")
    def k(x_hbm, ia_hbm, ib_hbm, o_hbm, ia_v, ib_v, rows_v, sem):
        start, nchunks = _sc_worker_range(T)

        @pl.loop(0, nchunks)
        def _(c):
            rows = pl.ds(start + c * SC_ROWS, SC_ROWS)
            pltpu.sync_copy(ia_hbm.at[rows], ia_v)
            pltpu.sync_copy(ib_hbm.at[rows], ib_v)
            pltpu.sync_copy(x_hbm.at[rows], rows_v)
            ca = pltpu.async_copy(rows_v, o_hbm.at[ia_v], sem)
            cb = pltpu.async_copy(rows_v, o_hbm.at[ib_v], sem)
            ca.wait()
            cb.wait()

    return k(src, idx_a, idx_b)


def _sc_gather_rows(src, idx_a, idx_b):
    D = src.shape[1]
    T = idx_a.shape[0]
    out = jax.ShapeDtypeStruct((T, D), src.dtype)

    @functools.partial(
        pl.kernel, out_type=(out, out), mesh=_sc_mesh(),
        scratch_types=[pltpu.VMEM((SC_ROWS,), jnp.int32), pltpu.VMEM((SC_ROWS,), jnp.int32),
                       pltpu.VMEM((SC_ROWS, D), src.dtype), pltpu.VMEM((SC_ROWS, D), src.dtype),
                       pltpu.SemaphoreType.DMA],
        name="sc_combine")
    def k(x_hbm, ia_hbm, ib_hbm, oa_hbm, ob_hbm, ia_v, ib_v, ra_v, rb_v, sem):
        start, nchunks = _sc_worker_range(T)

        @pl.loop(0, nchunks)
        def _(c):
            rows = pl.ds(start + c * SC_ROWS, SC_ROWS)
            pltpu.sync_copy(ia_hbm.at[rows], ia_v)
            pltpu.sync_copy(ib_hbm.at[rows], ib_v)
            ca = pltpu.async_copy(x_hbm.at[ia_v], ra_v, sem)
            cb = pltpu.async_copy(x_hbm.at[ib_v], rb_v, sem)
            ca.wait()
            cb.wait()
            pltpu.sync_copy(ra_v, oa_hbm.at[rows])
            pltpu.sync_copy(rb_v, ob_hbm.at[rows])

    return k(src, idx_a, idx_b)


def _combine_kernel(x1_ref, ya_ref, yb_ref, g_ref, gf_ref, out_ref):
    y = g_ref[:, 0:1] * ya_ref[...] + g_ref[:, 1:2] * yb_ref[...]
    out_ref[...] = _rms(x1_ref[...] + y, gf_ref[...])


def _combine(x1, ya, yb, gates, gf):
    T, D = x1.shape
    tile = pl.BlockSpec((TS, D), lambda i: (i, 0))
    return pl.pallas_call(
        _combine_kernel,
        grid=(T // TS,),
        in_specs=[tile, tile, tile, pl.BlockSpec((TS, 2), lambda i: (i, 0)), _const_spec((1, D))],
        out_specs=tile,
        out_shape=jax.ShapeDtypeStruct((T, D), F32),
        compiler_params=pltpu.CompilerParams(
            dimension_semantics=("parallel",), vmem_limit_bytes=VMEM_LIMIT),
        name="combine",
    )(x1, ya, yb, gates, gf)


def _rotary_tables(S):
    inv_freq = 1.0 / (ROPE_THETA ** (jnp.arange(0, HEAD_DIM, 2, dtype=F32) / HEAD_DIM))
    ang = jnp.arange(S, dtype=F32)[:, None] * inv_freq[None, :]
    cos, sin = jnp.cos(ang), jnp.sin(ang)
    z = jnp.zeros_like(sin)
    cos_t = jnp.concatenate([cos, cos, cos, cos], axis=1)
    sa_t = jnp.concatenate([-sin, z, -sin, z], axis=1)
    sb_t = jnp.concatenate([z, sin, z, sin], axis=1)
    return cos_t, sa_t, sb_t


def _trunk(x, p):
    B, S, D = x.shape
    T = B * S
    outs = _in_proj(x, p["n1"], p["w_qkvc"], p["conv_w"], *p["rot"])
    o_att = _attn(outs[:9])
    x1, h2, rt, cnt = _post(o_att, outs[9], x, p)

    e1 = rt[:, 0, :].reshape(T).astype(jnp.int32)
    e2 = rt[:, 1, :].reshape(T).astype(jnp.int32)
    gates = jnp.stack([rt[:, 2, :].reshape(T), rt[:, 3, :].reshape(T)], axis=1)
    r1 = rt[:, 4, :].reshape(T).astype(jnp.int32)
    r2 = rt[:, 5, :].reshape(T).astype(jnp.int32)
    counts = cnt[:, 0].astype(jnp.int32)
    pcounts = (counts + MOE_BLK - 1) // MOE_BLK * MOE_BLK
    pends = jnp.cumsum(pcounts)
    pstarts = pends - pcounts
    d1 = pstarts[e1] + r1
    d2 = pstarts[e2] + r2
    nb = (2 * T) // MOE_BLK + N_EXPERTS
    block_start = jnp.arange(nb, dtype=jnp.int32) * MOE_BLK
    block_e = jnp.minimum(jnp.sum((pends[None, :] <= block_start[:, None]).astype(jnp.int32), axis=1),
                          N_EXPERTS - 1)
    nused = (pends[-1:] // MOE_BLK).astype(jnp.int32)
    nvalid = jnp.clip(counts[block_e] - (block_start - pstarts[block_e]), 0, MOE_BLK)

    xs = _sc_scatter_rows(h2.reshape(T, D), d1, d2, nb * MOE_BLK)
    yb = _moe(block_e, nused, nvalid, xs, p["w_g"], p["w_u"], p["w_d"])
    ya, yc = _sc_gather_rows(yb, d1, d2)
    y = _combine(x1.reshape(T, D), ya, yc, gates, p["gf"])
    return y.reshape(B, S, D)


def kernel(x_prompt, x_sample, norm1_g, w_in, b_gate, conv_w, w_attn_out, w_conv_out, w_out, norm2_g,
           w_router_group, b_router_group, w_router_expert, b_router_expert, w_exp_gate, w_exp_up,
           w_exp_down, norm_f_g):
    assert norm1_g.shape[0] == 1, "single-layer trunk"
    S = x_prompt.shape[1]
    wr = jnp.zeros((ROUTER_ROWS, D_MODEL), F32)
    wr = wr.at[0:N_EXPERT_GROUPS].set(w_router_group[0].T).at[8:8 + N_EXPERTS].set(w_router_expert[0].T)
    br = jnp.zeros((ROUTER_ROWS,), F32)
    br = br.at[0:N_EXPERT_GROUPS].set(b_router_group[0]).at[8:8 + N_EXPERTS].set(b_router_expert[0])
    ti = jnp.arange(TS)
    p = dict(
        n1=norm1_g, w_qkvc=w_in[0, :, :C_ZA].astype(BF16), w_gate=w_in[0, :, C_ZA:].astype(BF16),
        b_gate=b_gate, conv_w=conv_w[0], w_co=w_conv_out[0].astype(BF16), rot=_rotary_tables(S),
        w_ao=w_attn_out[0].astype(BF16), w_o=w_out[0].astype(BF16), n2=norm2_g,
        wr=wr.astype(BF16), br=jnp.broadcast_to(br[:, None], (ROUTER_ROWS, 128)),
        tri=(ti[:, None] < ti[None, :]).astype(BF16),
        w_g=w_exp_gate[0], w_u=w_exp_up[0], w_d=w_exp_down[0],
        gf=norm_f_g.reshape(1, D_MODEL),
    )
    return _trunk(x_prompt, p), _trunk(x_sample, p)
```

```python
import functools

import jax
import jax.numpy as jnp
from jax import lax
from jax.experimental import pallas as pl
from jax.experimental.pallas import tpu as pltpu
from jax.experimental.pallas import tpu_sc as plsc

D_MODEL = 1024
HEAD_DIM = 64
HEADS_PER_GROUP = 4
DILATIONS = (1, 4, 16)
HALF = 64
D_GRP = HEADS_PER_GROUP * HEAD_DIM
D_ATT = 3 * D_GRP
D_CONV = 768
N_EXPERT_GROUPS = 4
EXPERTS_PER_GROUP = 8
N_EXPERTS = 32
D_EXPERT = 512
RMS_EPS = 1e-6
NEG = -1e30
ROPE_THETA = 10000.0

C_Q, C_K, C_V = 0, D_ATT, 2 * D_ATT
C_CU, C_CB, C_CC = 3 * D_ATT, 3 * D_ATT + D_CONV, 3 * D_ATT + 2 * D_CONV
C_ZA = 3 * D_ATT + 3 * D_CONV

TSI = 1024
TS = 1024
TQ = 2048
QB = 128
KW = QB + 2 * HALF
MOE_BLK = 512
SC_ROWS = 32
ROUTER_ROWS = 48
VMEM_LIMIT = 56 * 1024 * 1024

F32 = jnp.float32
BF16 = jnp.bfloat16


def _rms(xf, g):
    return xf * lax.rsqrt(jnp.mean(xf * xf, axis=-1, keepdims=True) + RMS_EPS) * g


def _sigmoid(x):
    return 1.0 / (1.0 + jnp.exp(-x))


def _const_spec(shape):
    return pl.BlockSpec(shape, lambda *_: (0,) * len(shape), pipeline_mode=pl.Buffered(1))


def _in_proj_kernel(x_ref, xp_ref, xn_ref, n1_ref, w_ref, cw_ref, cos_ref, sa_ref, sb_ref,
                    q1, k1, v1, q2, k2, v2, q3, k3, v3, mix_ref, pbuf, dbuf):
    j = pl.program_id(1)
    nj = pl.num_programs(1)
    g1 = n1_ref[...]
    h = _rms(x_ref[0], g1).astype(BF16)

    def proj(hh, c0, width):
        return jnp.dot(hh, w_ref[:, c0:c0 + width], preferred_element_type=F32)

    cosv = cos_ref[...]
    sav = sa_ref[...]
    sbv = sb_ref[...]

    def rotary(z):
        return z * cosv + pltpu.roll(z, 96, 1) * sav + pltpu.roll(z, 32, 1) * sbv

    def emit(outs, c0, rot, scale):
        for g, d in enumerate(DILATIONS):
            z = proj(h, c0 + g * D_GRP, D_GRP)
            if rot:
                z = jnp.concatenate([rotary(z[:, :128]), rotary(z[:, 128:])], axis=1)
            if scale != 1.0:
                z = z * scale
            if d == 1:
                outs[g][0, 0] = z.astype(BF16)
            else:
                dbuf[0] = z[:, :128]
                dbuf[1] = z[:, 128:]
                for r in range(d):
                    for c in range(2):
                        outs[g][0, r, :, c * 128:(c + 1) * 128] = (
                            dbuf[c, pl.ds(r, TSI // d, stride=d), :].astype(BF16))

    emit((q1, q2, q3), C_Q, True, HEAD_DIM ** -0.5)
    emit((k1, k2, k3), C_K, True, 1.0)
    emit((v1, v2, v3), C_V, False, 1.0)

    p = proj(h, C_CC, D_CONV) * proj(h, C_CU, D_CONV)
    pbuf[8:8 + TSI, :] = p
    hp = _rms(xp_ref[0], g1).astype(BF16)
    pp = proj(hp, C_CC, D_CONV) * proj(hp, C_CU, D_CONV)
    pbuf[0:8, :] = jnp.where(j > 0, pp, 0.0)
    hn = _rms(xn_ref[0], g1).astype(BF16)
    pn = proj(hn, C_CC, D_CONV) * proj(hn, C_CU, D_CONV)
    pbuf[8 + TSI:16 + TSI, :] = jnp.where(j < nj - 1, pn, 0.0)
    conv = (cw_ref[0:1, :] * pbuf[7:7 + TSI, :] + cw_ref[1:2, :] * p
            + cw_ref[2:3, :] * pbuf[9:9 + TSI, :])
    mix_ref[0] = (proj(h, C_CB, D_CONV) * conv).astype(BF16)


def _in_proj(x, n1, w_qkvc, conv_w, cos_t, sa_t, sb_t):
    B, S, D = x.shape
    nj = S // TSI
    rot_spec = pl.BlockSpec((TSI, 128), lambda b, j: (j, 0))
    in_specs = [
        pl.BlockSpec((1, TSI, D), lambda b, j: (b, j, 0)),
        pl.BlockSpec((1, 8, D), lambda b, j: (b, jnp.maximum(j * (TSI // 8) - 1, 0), 0)),
        pl.BlockSpec((1, 8, D), lambda b, j: (b, jnp.minimum((j + 1) * (TSI // 8), S // 8 - 1), 0)),
        _const_spec((1, D)),
        _const_spec(w_qkvc.shape),
        _const_spec((3, D_CONV)),
        rot_spec, rot_spec, rot_spec,
    ]
    out_shape, out_specs = [], []
    for d in DILATIONS:
        for _ in range(3):
            out_shape.append(jax.ShapeDtypeStruct((B, d, S // d, D_GRP), BF16))
            out_specs.append(pl.BlockSpec((1, d, TSI // d, D_GRP), lambda b, j: (b, 0, j, 0)))
    out_shape.append(jax.ShapeDtypeStruct((B, S, D_CONV), BF16))
    out_specs.append(pl.BlockSpec((1, TSI, D_CONV), lambda b, j: (b, j, 0)))
    return pl.pallas_call(
        _in_proj_kernel,
        grid=(B, nj),
        in_specs=in_specs,
        out_specs=out_specs,
        out_shape=out_shape,
        scratch_shapes=[pltpu.VMEM((TSI + 16, D_CONV), F32), pltpu.VMEM((2, TSI, 128), F32)],
        compiler_params=pltpu.CompilerParams(
            dimension_semantics=("parallel", "parallel"), vmem_limit_bytes=VMEM_LIMIT),
        name="in_proj",
    )(x, x, x, n1, w_qkvc, conv_w, cos_t, sa_t, sb_t)


def _attn_kernel(*refs):
    ins = refs[:21]
    bias_ref = refs[21]
    o_ref = refs[22]
    kbufs = refs[23:29]
    os_ref, ls_ref = refs[29], refs[30]
    j = pl.program_id(1)
    nj = pl.num_programs(1)

    lane = lax.broadcasted_iota(jnp.int32, (QB, D_GRP), 1)
    head_of_lane = lane // HEAD_DIM

    for g, d in enumerate(DILATIONS):
        q_ref, kc, kp, kn, vc, vp, vn = ins[7 * g:7 * g + 7]
        kb, vb = kbufs[2 * g], kbufs[2 * g + 1]
        n = TQ // d
        nblk = n // QB
        for buf, prev, cur, nxt in ((kb, kp, kc, kn), (vb, vp, vc, vn)):
            buf[:, 0:HALF, :] = prev[0]
            buf[:, HALF:HALF + n, :] = cur[0]
            buf[:, HALF + n:, :] = nxt[0]

        def scores(idx, q_ref=q_ref, kb=kb, nblk=nblk):
            r, jb = divmod(idx, nblk)
            qb = q_ref[0, r, jb * QB:(jb + 1) * QB, :]
            kw = kb[r, jb * QB:jb * QB + KW, :]
            qs = jnp.concatenate(
                [jnp.where(head_of_lane == hh, qb, jnp.zeros_like(qb)) for hh in range(HEADS_PER_GROUP)],
                axis=0)
            s = lax.dot_general(qs, kw, (((1,), (1,)), ((), ())), preferred_element_type=F32)
            variant = 0
            if jb == 0:
                variant = jnp.where(j == 0, 1, variant)
            if jb == nblk - 1:
                variant = jnp.where(j == nj - 1, 2, variant)
            return s + bias_ref[variant]

        def softmax(s):
            m = jnp.max(s, axis=1, keepdims=True)
            p = jnp.exp(s - m)
            l = jnp.sum(p, axis=1, keepdims=True)
            return p.astype(BF16), 1.0 / l, m + jnp.log(l)

        def finish(idx, pb, inv_l, lse, vb=vb, d=d, nblk=nblk, g=g):
            r, jb = divmod(idx, nblk)
            pv = jnp.dot(pb, vb[r, jb * QB:jb * QB + KW, :], preferred_element_type=F32)
            o = jnp.zeros((QB, D_GRP), F32)
            ls = jnp.zeros((QB, D_GRP), F32)
            for hh in range(HEADS_PER_GROUP):
                sl = slice(hh * QB, (hh + 1) * QB)
                sel = head_of_lane == hh
                o = jnp.where(sel, pv[sl] * inv_l[sl], o)
                ls = jnp.where(sel, lse[sl], ls)
            for c in range(2):
                cs = slice(c * 128, (c + 1) * 128)
                rows = pl.ds(jb * QB, QB) if d == 1 else pl.ds(r + jb * QB * d, QB, stride=d)
                os_ref[2 * g + c, rows, :] = o[:, cs]
                ls_ref[2 * g + c, rows, :] = ls[:, cs]

        nb = d * nblk
        s_of, sm_of = {}, {}
        for i in range(nb + 2):
            if i >= 2:
                finish(i - 2, *sm_of.pop(i - 2))
            if i < nb:
                s_of[i] = scores(i)
            if 1 <= i <= nb:
                sm_of[i - 1] = softmax(s_of.pop(i - 1))

    def merge(c, carry):
        rows = pl.ds(pl.multiple_of(c * QB, QB), QB)
        for c in range(2):
            l0, l1, l2 = ls_ref[c, rows, :], ls_ref[2 + c, rows, :], ls_ref[4 + c, rows, :]
            mm = jnp.maximum(jnp.maximum(l0, l1), l2)
            w0, w1, w2 = jnp.exp(l0 - mm), jnp.exp(l1 - mm), jnp.exp(l2 - mm)
            o = (w0 * os_ref[c, rows, :] + w1 * os_ref[2 + c, rows, :]
                 + w2 * os_ref[4 + c, rows, :]) / (w0 + w1 + w2)
            o_ref[0, rows, c * 128:(c + 1) * 128] = o.astype(BF16)
        return carry

    lax.fori_loop(0, TQ // QB, merge, 0)


def _band_bias():
    row = jnp.arange(HEADS_PER_GROUP * QB)[:, None] % QB
    col = jnp.arange(KW)[None, :]
    band = (col >= row) & (col <= row + 2 * HALF)
    variants = (band, band & (col >= HALF), band & (col < KW - HALF))
    return jnp.stack([jnp.where(v, 0.0, NEG).astype(F32) for v in variants])


def _attn(qkv):
    B = qkv[0].shape[0]
    S = qkv[0].shape[2]
    nj = S // TQ
    assert nj > 1
    ins, in_specs, scratch = [], [], []
    for g, d in enumerate(DILATIONS):
        q, k, v = qkv[3 * g:3 * g + 3]
        n = TQ // d
        L = S // d
        nh = n // HALF
        cur = pl.BlockSpec((1, d, n, D_GRP), lambda b, j: (b, 0, j, 0))
        prev = pl.BlockSpec((1, d, HALF, D_GRP),
                            lambda b, j, nh=nh: (b, 0, jnp.maximum(j * nh - 1, 0), 0))
        nxt = pl.BlockSpec((1, d, HALF, D_GRP),
                           lambda b, j, nh=nh, L=L: (b, 0, jnp.minimum((j + 1) * nh, L // HALF - 1), 0))
        ins += [q, k, k, k, v, v, v]
        in_specs += [cur, cur, prev, nxt, cur, prev, nxt]
        scratch += [pltpu.VMEM((d, n + 2 * HALF, D_GRP), BF16)] * 2
    scratch += [pltpu.VMEM((6, TQ, 128), F32), pltpu.VMEM((6, TQ, 128), F32)]
    ins.append(_band_bias())
    in_specs.append(_const_spec((3, HEADS_PER_GROUP * QB, KW)))
    return pl.pallas_call(
        _attn_kernel,
        grid=(B, nj),
        in_specs=in_specs,
        out_specs=pl.BlockSpec((1, TQ, D_GRP), lambda b, j: (b, j, 0)),
        out_shape=jax.ShapeDtypeStruct((B, S, D_GRP), BF16),
        scratch_shapes=scratch,
        compiler_params=pltpu.CompilerParams(
            dimension_semantics=("parallel", "parallel"), vmem_limit_bytes=VMEM_LIMIT),
        name="attn",
    )(*ins)


def _post_kernel(o_ref, mix_ref, x_ref, n1_ref, wz_ref, bg_ref, wco_ref, wao_ref, wo_ref, n2_ref,
                 wr_ref, br_ref, tri_ref, x1_ref, h2_ref, rt_ref, cnt_ref, carry):
    first = (pl.program_id(0) == 0) & (pl.program_id(1) == 0)

    @pl.when(first)
    def _():
        carry[...] = jnp.zeros_like(carry)

    x = x_ref[0]
    h = _rms(x, n1_ref[...]).astype(BF16)
    g_a = _sigmoid(jnp.dot(h, wz_ref[:, :D_MODEL], preferred_element_type=F32) + bg_ref[:, :D_MODEL])
    g_b = _sigmoid(jnp.dot(h, wz_ref[:, D_MODEL:], preferred_element_type=F32) + bg_ref[:, D_MODEL:])
    att = jnp.dot(o_ref[0], wao_ref[...], preferred_element_type=F32)
    cvb = jnp.dot(mix_ref[0], wco_ref[...], preferred_element_type=F32)
    merged = (g_a * att + g_b * cvb).astype(BF16)
    x1 = x + jnp.dot(merged, wo_ref[...], preferred_element_type=F32)
    x1_ref[0] = x1
    h2 = _rms(x1, n2_ref[...])
    h2_ref[0] = h2

    lt = lax.dot_general(wr_ref[...], h2.astype(BF16), (((1,), (1,)), ((), ())),
                         preferred_element_type=F32) + br_ref[:, 0:1]
    grow = lax.broadcasted_iota(jnp.int32, (8, TS), 0)
    gl = jnp.where(grow < N_EXPERT_GROUPS, lt[0:8], NEG)
    gmax = jnp.max(gl, axis=0, keepdims=True)
    grp = jnp.min(jnp.where(gl == gmax, grow, 8), axis=0, keepdims=True)
    grp_w = 1.0 / jnp.sum(jnp.exp(gl - gmax), axis=0, keepdims=True)
    erow = lax.broadcasted_iota(jnp.int32, (N_EXPERTS, TS), 0)
    el = jnp.where(erow // EXPERTS_PER_GROUP == grp, lt[8:8 + N_EXPERTS], NEG)
    v1 = jnp.max(el, axis=0, keepdims=True)
    i1 = jnp.min(jnp.where(el == v1, erow, N_EXPERTS), axis=0, keepdims=True)
    el2 = jnp.where(erow == i1, NEG, el)
    v2 = jnp.max(el2, axis=0, keepdims=True)
    i2 = jnp.min(jnp.where(el2 == v2, erow, N_EXPERTS), axis=0, keepdims=True)
    t = jnp.exp(v2 - v1)
    den = 1.0 + t
    gate1 = grp_w * (1.0 / den)
    gate2 = grp_w * (t / den)

    oh1 = (erow == i1).astype(F32)
    oh2 = (erow == i2).astype(F32)
    both = oh1 + oh2
    cum = jnp.dot(both.astype(BF16), tri_ref[...], preferred_element_type=F32)
    basec = carry[:, 0:1] + cum
    rank1 = jnp.sum(oh1 * basec, axis=0, keepdims=True)
    rank2 = jnp.sum(oh2 * basec, axis=0, keepdims=True)
    newc = carry[...] + jnp.sum(both, axis=1, keepdims=True)
    carry[...] = newc
    cnt_ref[...] = newc
    zero = jnp.zeros((1, TS), F32)
    rt_ref[0] = jnp.concatenate(
        [i1.astype(F32), i2.astype(F32), gate1, gate2, rank1, rank2, zero, zero], axis=0)


def _post(o_att, mix, x, p):
    B, S, D = x.shape
    nj = S // TS
    tile = lambda w: pl.BlockSpec((1, TS, w), lambda b, j: (b, j, 0))
    return pl.pallas_call(
        _post_kernel,
        grid=(B, nj),
        in_specs=[tile(D_GRP), tile(D_CONV), tile(D),
                  _const_spec((1, D)), _const_spec((D, 2 * D)), _const_spec((1, 2 * D)),
                  _const_spec((D_CONV, D)), _const_spec((D_GRP, D)), _const_spec((D, D)),
                  _const_spec((1, D)), _const_spec((ROUTER_ROWS, D)), _const_spec((ROUTER_ROWS, 128)),
                  _const_spec((TS, TS))],
        out_specs=[tile(D), tile(D),
                   pl.BlockSpec((1, 8, TS), lambda b, j: (b, 0, j)),
                   pl.BlockSpec((N_EXPERTS, 128), lambda b, j: (0, 0))],
        out_shape=[jax.ShapeDtypeStruct((B, S, D), F32), jax.ShapeDtypeStruct((B, S, D), F32),
                   jax.ShapeDtypeStruct((B, 8, S), F32),
                   jax.ShapeDtypeStruct((N_EXPERTS, 128), F32)],
        scratch_shapes=[pltpu.VMEM((N_EXPERTS, 128), F32)],
        compiler_params=pltpu.CompilerParams(
            dimension_semantics=("arbitrary", "arbitrary"), vmem_limit_bytes=VMEM_LIMIT),
        name="post",
    )(o_att, mix, x, p["n1"], p["w_gate"], p["b_gate"], p["w_co"], p["w_ao"], p["w_o"], p["n2"],
      p["wr"], p["br"], p["tri"])


def _moe_kernel(be_ref, nu_ref, nv_ref, x_ref, wg_ref, wu_ref, wd_ref, out_ref, wgb, wub, wdb):
    b = pl.program_id(0)

    @pl.when(b < nu_ref[0])
    def _():
        @pl.when((b == 0) | (be_ref[b] != be_ref[jnp.maximum(b - 1, 0)]))
        def _():
            wgb[...] = wg_ref[0].astype(BF16)
            wub[...] = wu_ref[0].astype(BF16)
            wdb[...] = wd_ref[0].astype(BF16)

        live = lax.broadcasted_iota(jnp.int32, (MOE_BLK, 1), 0) < nv_ref[b]
        x = jnp.where(live, x_ref[...], 0.0).astype(BF16)
        a = jnp.dot(x, wgb[...], preferred_element_type=F32)
        u = jnp.dot(x, wub[...], preferred_element_type=F32)
        hm = (a * _sigmoid(a) * u).astype(BF16)
        out_ref[...] = jnp.dot(hm, wdb[...], preferred_element_type=F32)

    @pl.when(b >= nu_ref[0])
    def _():
        out_ref[...] = jnp.zeros_like(out_ref)


def _moe(block_e, nused, nvalid, xs, w_g, w_u, w_d):
    nb = block_e.shape[0]
    D = xs.shape[1]
    grid_spec = pltpu.PrefetchScalarGridSpec(
        num_scalar_prefetch=3,
        grid=(nb,),
        in_specs=[
            pl.BlockSpec((MOE_BLK, D), lambda b, be, nu, nv: (b, 0)),
            pl.BlockSpec((1, D, D_EXPERT), lambda b, be, nu, nv: (be[b], 0, 0)),
            pl.BlockSpec((1, D, D_EXPERT), lambda b, be, nu, nv: (be[b], 0, 0)),
            pl.BlockSpec((1, D_EXPERT, D), lambda b, be, nu, nv: (be[b], 0, 0)),
        ],
        out_specs=pl.BlockSpec((MOE_BLK, D), lambda b, be, nu, nv: (b, 0)),
        scratch_shapes=[pltpu.VMEM((D, D_EXPERT), BF16), pltpu.VMEM((D, D_EXPERT), BF16),
                        pltpu.VMEM((D_EXPERT, D), BF16)],
    )
    return pl.pallas_call(
        _moe_kernel,
        grid_spec=grid_spec,
        out_shape=jax.ShapeDtypeStruct((nb * MOE_BLK, D), F32),
        compiler_params=pltpu.CompilerParams(
            dimension_semantics=("arbitrary",), vmem_limit_bytes=VMEM_LIMIT),
        name="moe",
    )(block_e, nused, nvalid, xs, w_g, w_u, w_d)


def _sc_mesh():
    return plsc.VectorSubcoreMesh(core_axis_name="c", subcore_axis_name="s")


def _sc_worker_range(total):
    info = plsc.get_sparse_core_info()
    nw = info.num_cores * info.num_subcores
    wid = lax.axis_index("s") * info.num_cores + lax.axis_index("c")
    per_w = total // nw
    assert per_w * nw == total and per_w % SC_ROWS == 0
    return wid * per_w, per_w // SC_ROWS


def _sc_scatter_rows(src, idx_a, idx_b, nslots):
    T, D = src.shape

    @functools.partial(
        pl.kernel, out_type=jax.ShapeDtypeStruct((nslots, D), src.dtype), mesh=_sc_mesh(),
        scratch_types=[pltpu.VMEM((SC_ROWS,), jnp.int32), pltpu.VMEM((SC_ROWS,), jnp.int32),
                       pltpu.VMEM((SC_ROWS, D), src.dtype), pltpu.SemaphoreType.DMA],
        name="sc_dispatch")
    def k(x_hbm, ia_hbm, ib_hbm, o_hbm, ia_v, ib_v, rows_v, sem):
        start, nchunks = _sc_worker_range(T)

        @pl.loop(0, nchunks)
        def _(c):
            rows = pl.ds(start + c * SC_ROWS, SC_ROWS)
            pltpu.sync_copy(ia_hbm.at[rows], ia_v)
            pltpu.sync_copy(ib_hbm.at[rows], ib_v)
            pltpu.sync_copy(x_hbm.at[rows], rows_v)
            ca = pltpu.async_copy(rows_v, o_hbm.at[ia_v], sem)
            cb = pltpu.async_copy(rows_v, o_hbm.at[ib_v], sem)
            ca.wait()
            cb.wait()

    return k(src, idx_a, idx_b)


def _sc_gather_rows(src, idx_a, idx_b):
    D = src.shape[1]
    T = idx_a.shape[0]
    out = jax.ShapeDtypeStruct((T, D), src.dtype)

    @functools.partial(
        pl.kernel, out_type=(out, out), mesh=_sc_mesh(),
        scratch_types=[pltpu.VMEM((SC_ROWS,), jnp.int32), pltpu.VMEM((SC_ROWS,), jnp.int32),
                       pltpu.VMEM((SC_ROWS, D), src.dtype), pltpu.VMEM((SC_ROWS, D), src.dtype),
                       pltpu.SemaphoreType.DMA],
        name="sc_combine")
    def k(x_hbm, ia_hbm, ib_hbm, oa_hbm, ob_hbm, ia_v, ib_v, ra_v, rb_v, sem):
        start, nchunks = _sc_worker_range(T)

        @pl.loop(0, nchunks)
        def _(c):
            rows = pl.ds(start + c * SC_ROWS, SC_ROWS)
            pltpu.sync_copy(ia_hbm.at[rows], ia_v)
            pltpu.sync_copy(ib_hbm.at[rows], ib_v)
            ca = pltpu.async_copy(x_hbm.at[ia_v], ra_v, sem)
            cb = pltpu.async_copy(x_hbm.at[ib_v], rb_v, sem)
            ca.wait()
            cb.wait()
            pltpu.sync_copy(ra_v, oa_hbm.at[rows])
            pltpu.sync_copy(rb_v, ob_hbm.at[rows])

    return k(src, idx_a, idx_b)


def _combine_kernel(x1_ref, ya_ref, yb_ref, rt_ref, gf_ref, out_ref):
    gt = jnp.transpose(rt_ref[0])
    y = gt[:, 2:3] * ya_ref[...] + gt[:, 3:4] * yb_ref[...]
    out_ref[...] = _rms(x1_ref[...] + y, gf_ref[...])


def _combine(x1, ya, yb, rt, gf):
    T, D = x1.shape
    per_seq = rt.shape[2] // TS
    tile = pl.BlockSpec((TS, D), lambda i: (i, 0))
    return pl.pallas_call(
        _combine_kernel,
        grid=(T // TS,),
        in_specs=[tile, tile, tile,
                  pl.BlockSpec((1, 8, TS), lambda i: (i // per_seq, 0, i % per_seq)),
                  _const_spec((1, D))],
        out_specs=tile,
        out_shape=jax.ShapeDtypeStruct((T, D), F32),
        compiler_params=pltpu.CompilerParams(
            dimension_semantics=("parallel",), vmem_limit_bytes=VMEM_LIMIT),
        name="combine",
    )(x1, ya, yb, rt, gf)


def _slots_kernel(ps_ref, rt_ref, d_ref):
    v = rt_ref[0]
    start = jnp.zeros(v.shape, F32)
    for e in range(N_EXPERTS):
        start = jnp.where(v == float(e), ps_ref[e].astype(F32), start)
    d_ref[0] = (start[0:2] + v[4:6]).astype(jnp.int32)


def _slots(pstarts, rt):
    B, _, S = rt.shape
    grid_spec = pltpu.PrefetchScalarGridSpec(
        num_scalar_prefetch=1,
        grid=(B, S // TS),
        in_specs=[pl.BlockSpec((1, 8, TS), lambda b, j, ps: (b, 0, j))],
        out_specs=pl.BlockSpec((1, 2, TS), lambda b, j, ps: (b, 0, j)),
    )
    return pl.pallas_call(
        _slots_kernel,
        grid_spec=grid_spec,
        out_shape=jax.ShapeDtypeStruct((B, 2, S), jnp.int32),
        compiler_params=pltpu.CompilerParams(dimension_semantics=("parallel", "parallel")),
        name="slots",
    )(pstarts, rt)


def _rotary_tables(S):
    inv_freq = 1.0 / (ROPE_THETA ** (jnp.arange(0, HEAD_DIM, 2, dtype=F32) / HEAD_DIM))
    ang = jnp.arange(S, dtype=F32)[:, None] * inv_freq[None, :]
    cos, sin = jnp.cos(ang), jnp.sin(ang)
    z = jnp.zeros_like(sin)
    cos_t = jnp.concatenate([cos, cos, cos, cos], axis=1)
    sa_t = jnp.concatenate([-sin, z, -sin, z], axis=1)
    sb_t = jnp.concatenate([z, sin, z, sin], axis=1)
    return cos_t, sa_t, sb_t


def _trunk(x, p):
    B, S, D = x.shape
    T = B * S
    outs = _in_proj(x, p["n1"], p["w_qkvc"], p["conv_w"], *p["rot"])
    o_att = _attn(outs[:9])
    x1, h2, rt, cnt = _post(o_att, outs[9], x, p)

    counts = cnt[:, 0].astype(jnp.int32)
    pcounts = (counts + MOE_BLK - 1) // MOE_BLK * MOE_BLK
    pends = jnp.cumsum(pcounts)
    pstarts = pends - pcounts
    dest = _slots(pstarts, rt)
    d1 = dest[:, 0, :].reshape(T)
    d2 = dest[:, 1, :].reshape(T)
    nb = (2 * T) // MOE_BLK + N_EXPERTS
    block_start = jnp.arange(nb, dtype=jnp.int32) * MOE_BLK
    block_e = jnp.minimum(jnp.sum((pends[None, :] <= block_start[:, None]).astype(jnp.int32), axis=1),
                          N_EXPERTS - 1)
    nused = (pends[-1:] // MOE_BLK).astype(jnp.int32)
    nvalid = jnp.clip(counts[block_e] - (block_start - pstarts[block_e]), 0, MOE_BLK)

    xs = _sc_scatter_rows(h2.reshape(T, D), d1, d2, nb * MOE_BLK)
    yb = _moe(block_e, nused, nvalid, xs, p["w_g"], p["w_u"], p["w_d"])
    ya, yc = _sc_gather_rows(yb, d1, d2)
    y = _combine(x1.reshape(T, D), ya, yc, rt, p["gf"])
    return y.reshape(B, S, D)


def kernel(x_prompt, x_sample, norm1_g, w_in, b_gate, conv_w, w_attn_out, w_conv_out, w_out, norm2_g,
           w_router_group, b_router_group, w_router_expert, b_router_expert, w_exp_gate, w_exp_up,
           w_exp_down, norm_f_g):
    assert norm1_g.shape[0] == 1, "single-layer trunk"
    S = x_prompt.shape[1]
    wr = jnp.zeros((ROUTER_ROWS, D_MODEL), F32)
    wr = wr.at[0:N_EXPERT_GROUPS].set(w_router_group[0].T).at[8:8 + N_EXPERTS].set(w_router_expert[0].T)
    br = jnp.zeros((ROUTER_ROWS,), F32)
    br = br.at[0:N_EXPERT_GROUPS].set(b_router_group[0]).at[8:8 + N_EXPERTS].set(b_router_expert[0])
    ti = jnp.arange(TS)
    p = dict(
        n1=norm1_g, w_qkvc=w_in[0, :, :C_ZA].astype(BF16), w_gate=w_in[0, :, C_ZA:].astype(BF16),
        b_gate=b_gate, conv_w=conv_w[0], w_co=w_conv_out[0].astype(BF16), rot=_rotary_tables(S),
        w_ao=w_attn_out[0].astype(BF16), w_o=w_out[0].astype(BF16), n2=norm2_g,
        wr=wr.astype(BF16), br=jnp.broadcast_to(br[:, None], (ROUTER_ROWS, 128)),
        tri=(ti[:, None] < ti[None, :]).astype(BF16),
        w_g=w_exp_gate[0], w_u=w_exp_up[0], w_d=w_exp_down[0],
        gf=norm_f_g.reshape(1, D_MODEL),
    )
    return _trunk(x_prompt, p), _trunk(x_sample, p)
```

```python
import functools

import jax
import jax.numpy as jnp
from jax import lax
from jax.experimental import pallas as pl
from jax.experimental.pallas import tpu as pltpu
from jax.experimental.pallas import tpu_sc as plsc

D_MODEL = 1024
HEAD_DIM = 64
HEADS_PER_GROUP = 4
DILATIONS = (1, 4, 16)
HALF = 64
D_GRP = HEADS_PER_GROUP * HEAD_DIM
D_ATT = 3 * D_GRP
D_CONV = 768
N_EXPERT_GROUPS = 4
EXPERTS_PER_GROUP = 8
N_EXPERTS = 32
D_EXPERT = 512
RMS_EPS = 1e-6
NEG = -1e30
ROPE_THETA = 10000.0

C_Q, C_K, C_V = 0, D_ATT, 2 * D_ATT
C_CU, C_CB, C_CC = 3 * D_ATT, 3 * D_ATT + D_CONV, 3 * D_ATT + 2 * D_CONV
C_ZA = 3 * D_ATT + 3 * D_CONV

TSI = 1024
TS = 1024
TQ = 2048
QB = 128
KW = QB + 2 * HALF
MOE_BLK = 512
SEG_SEQS = 4
SC_ROWS = 32
ROUTER_ROWS = 48
CUM_CHUNK = 256
VMEM_LIMIT = 56 * 1024 * 1024

F32 = jnp.float32
BF16 = jnp.bfloat16


def _rms(xf, g):
    return xf * lax.rsqrt(jnp.mean(xf * xf, axis=-1, keepdims=True) + RMS_EPS) * g


def _sigmoid(x):
    return 1.0 / (1.0 + jnp.exp(-x))


def _const_spec(shape):
    return pl.BlockSpec(shape, lambda *_: (0,) * len(shape), pipeline_mode=pl.Buffered(1))


def _in_proj_kernel(x_ref, xp_ref, xn_ref, n1_ref, w_ref, cw_ref, cos_ref, sa_ref, sb_ref,
                    q1, k1, v1, q2, k2, v2, q3, k3, v3, mix_ref, pbuf, dbuf):
    j = pl.program_id(1)
    nj = pl.num_programs(1)
    g1 = n1_ref[...]
    hf = _rms(x_ref[0], g1)
    h = hf.astype(BF16)

    def proj(hh, c0, width):
        return jnp.dot(hh, w_ref[:, c0:c0 + width], preferred_element_type=F32)

    cosv = cos_ref[...]
    sav = sa_ref[...]
    sbv = sb_ref[...]

    def rotary(z):
        return z * cosv + pltpu.roll(z, 96, 1) * sav + pltpu.roll(z, 32, 1) * sbv

    def emit(outs, c0, rot, scale):
        for g, d in enumerate(DILATIONS):
            z = proj(h, c0 + g * D_GRP, D_GRP)
            if rot:
                z = jnp.concatenate([rotary(z[:, :128]), rotary(z[:, 128:])], axis=1)
            if scale != 1.0:
                z = z * scale
            if d == 1:
                outs[g][0, 0] = z.astype(BF16)
            else:
                dbuf[0] = z[:, :128]
                dbuf[1] = z[:, 128:]
                for r in range(d):
                    for c in range(2):
                        outs[g][0, r, :, c * 128:(c + 1) * 128] = (
                            dbuf[c, pl.ds(r, TSI // d, stride=d), :].astype(BF16))

    emit((q1, q2, q3), C_Q, True, HEAD_DIM ** -0.5)
    emit((k1, k2, k3), C_K, True, 1.0)
    emit((v1, v2, v3), C_V, False, 1.0)

    hp = _rms(xp_ref[0], g1)
    hn = _rms(xn_ref[0], g1)
    he = jnp.concatenate([hp, hf, hn], axis=0).astype(BF16)
    pe = proj(he, C_CC, D_CONV) * proj(he, C_CU, D_CONV)
    erow = lax.broadcasted_iota(jnp.int32, (TSI + 16, 1), 0)
    outside = ((erow < 8) & (j == 0)) | ((erow >= 8 + TSI) & (j == nj - 1))
    pbuf[...] = jnp.where(outside, 0.0, pe)
    conv = (cw_ref[0:1, :] * pbuf[7:7 + TSI, :] + cw_ref[1:2, :] * pbuf[8:8 + TSI, :]
            + cw_ref[2:3, :] * pbuf[9:9 + TSI, :])
    mix_ref[0] = (proj(h, C_CB, D_CONV) * conv).astype(BF16)


def _in_proj(x, n1, w_qkvc, conv_w, cos_t, sa_t, sb_t):
    B, S, D = x.shape
    nj = S // TSI
    rot_spec = pl.BlockSpec((TSI, 128), lambda b, j: (j, 0))
    in_specs = [
        pl.BlockSpec((1, TSI, D), lambda b, j: (b, j, 0)),
        pl.BlockSpec((1, 8, D), lambda b, j: (b, jnp.maximum(j * (TSI // 8) - 1, 0), 0)),
        pl.BlockSpec((1, 8, D), lambda b, j: (b, jnp.minimum((j + 1) * (TSI // 8), S // 8 - 1), 0)),
        _const_spec((1, D)),
        _const_spec(w_qkvc.shape),
        _const_spec((3, D_CONV)),
        rot_spec, rot_spec, rot_spec,
    ]
    out_shape, out_specs = [], []
    for d in DILATIONS:
        for _ in range(3):
            out_shape.append(jax.ShapeDtypeStruct((B, d, S // d, D_GRP), BF16))
            out_specs.append(pl.BlockSpec((1, d, TSI // d, D_GRP), lambda b, j: (b, 0, j, 0)))
    out_shape.append(jax.ShapeDtypeStruct((B, S, D_CONV), BF16))
    out_specs.append(pl.BlockSpec((1, TSI, D_CONV), lambda b, j: (b, j, 0)))
    return pl.pallas_call(
        _in_proj_kernel,
        grid=(B, nj),
        in_specs=in_specs,
        out_specs=out_specs,
        out_shape=out_shape,
        scratch_shapes=[pltpu.VMEM((TSI + 16, D_CONV), F32), pltpu.VMEM((2, TSI, 128), F32)],
        compiler_params=pltpu.CompilerParams(
            dimension_semantics=("parallel", "parallel"), vmem_limit_bytes=VMEM_LIMIT),
        name="in_proj",
    )(x, x, x, n1, w_qkvc, conv_w, cos_t, sa_t, sb_t)


def _attn_kernel(*refs):
    ins = refs[:21]
    bias_ref = refs[21]
    o_ref = refs[22]
    kbufs = refs[23:29]
    os_ref, ls_ref = refs[29], refs[30]
    j = pl.program_id(1)
    nj = pl.num_programs(1)

    lane = lax.broadcasted_iota(jnp.int32, (QB, D_GRP), 1)
    head_of_lane = lane // HEAD_DIM

    for g, d in enumerate(DILATIONS):
        q_ref, kc, kp, kn, vc, vp, vn = ins[7 * g:7 * g + 7]
        kb, vb = kbufs[2 * g], kbufs[2 * g + 1]
        n = TQ // d
        nblk = n // QB
        for buf, prev, cur, nxt in ((kb, kp, kc, kn), (vb, vp, vc, vn)):
            buf[:, 0:HALF, :] = prev[0]
            buf[:, HALF:HALF + n, :] = cur[0]
            buf[:, HALF + n:, :] = nxt[0]

        def scores(idx, q_ref=q_ref, kb=kb, nblk=nblk):
            r, jb = divmod(idx, nblk)
            qb = q_ref[0, r, jb * QB:(jb + 1) * QB, :]
            kw = kb[r, jb * QB:jb * QB + KW, :]
            qs = jnp.concatenate(
                [jnp.where(head_of_lane == hh, qb, jnp.zeros_like(qb)) for hh in range(HEADS_PER_GROUP)],
                axis=0)
            s = lax.dot_general(qs, kw, (((1,), (1,)), ((), ())), preferred_element_type=F32)
            variant = 0
            if jb == 0:
                variant = jnp.where(j == 0, 1, variant)
            if jb == nblk - 1:
                variant = jnp.where(j == nj - 1, 2, variant)
            return s + bias_ref[variant]

        def softmax(s):
            m = jnp.max(s, axis=1, keepdims=True)
            p = jnp.exp(s - m)
            l = jnp.sum(p, axis=1, keepdims=True)
            return p.astype(BF16), 1.0 / l, m + jnp.log(l)

        def finish(idx, pb, inv_l, lse, vb=vb, d=d, nblk=nblk, g=g):
            r, jb = divmod(idx, nblk)
            pv = jnp.dot(pb, vb[r, jb * QB:jb * QB + KW, :], preferred_element_type=F32)
            o = jnp.zeros((QB, D_GRP), F32)
            ls = jnp.zeros((QB, D_GRP), F32)
            for hh in range(HEADS_PER_GROUP):
                sl = slice(hh * QB, (hh + 1) * QB)
                sel = head_of_lane == hh
                o = jnp.where(sel, pv[sl] * inv_l[sl], o)
                ls = jnp.where(sel, lse[sl], ls)
            for c in range(2):
                cs = slice(c * 128, (c + 1) * 128)
                rows = pl.ds(jb * QB, QB) if d == 1 else pl.ds(r + jb * QB * d, QB, stride=d)
                os_ref[2 * g + c, rows, :] = o[:, cs]
                ls_ref[2 * g + c, rows, :] = ls[:, cs]

        nb = d * nblk
        s_of, sm_of = {}, {}
        for i in range(nb + 2):
            if i >= 2:
                finish(i - 2, *sm_of.pop(i - 2))
            if i < nb:
                s_of[i] = scores(i)
            if 1 <= i <= nb:
                sm_of[i - 1] = softmax(s_of.pop(i - 1))

    def merge(c, carry):
        rows = pl.ds(pl.multiple_of(c * QB, QB), QB)
        for c in range(2):
            l0, l1, l2 = ls_ref[c, rows, :], ls_ref[2 + c, rows, :], ls_ref[4 + c, rows, :]
            mm = jnp.maximum(jnp.maximum(l0, l1), l2)
            w0, w1, w2 = jnp.exp(l0 - mm), jnp.exp(l1 - mm), jnp.exp(l2 - mm)
            o = (w0 * os_ref[c, rows, :] + w1 * os_ref[2 + c, rows, :]
                 + w2 * os_ref[4 + c, rows, :]) / (w0 + w1 + w2)
            o_ref[0, rows, c * 128:(c + 1) * 128] = o.astype(BF16)
        return carry

    lax.fori_loop(0, TQ // QB, merge, 0)


def _band_bias():
    row = jnp.arange(HEADS_PER_GROUP * QB)[:, None] % QB
    col = jnp.arange(KW)[None, :]
    band = (col >= row) & (col <= row + 2 * HALF)
    variants = (band, band & (col >= HALF), band & (col < KW - HALF))
    return jnp.stack([jnp.where(v, 0.0, NEG).astype(F32) for v in variants])


def _attn(qkv):
    B = qkv[0].shape[0]
    S = qkv[0].shape[2]
    nj = S // TQ
    assert nj > 1
    ins, in_specs, scratch = [], [], []
    for g, d in enumerate(DILATIONS):
        q, k, v = qkv[3 * g:3 * g + 3]
        n = TQ // d
        L = S // d
        nh = n // HALF
        cur = pl.BlockSpec((1, d, n, D_GRP), lambda b, j: (b, 0, j, 0))
        prev = pl.BlockSpec((1, d, HALF, D_GRP),
                            lambda b, j, nh=nh: (b, 0, jnp.maximum(j * nh - 1, 0), 0))
        nxt = pl.BlockSpec((1, d, HALF, D_GRP),
                           lambda b, j, nh=nh, L=L: (b, 0, jnp.minimum((j + 1) * nh, L // HALF - 1), 0))
        ins += [q, k, k, k, v, v, v]
        in_specs += [cur, cur, prev, nxt, cur, prev, nxt]
        scratch += [pltpu.VMEM((d, n + 2 * HALF, D_GRP), BF16)] * 2
    scratch += [pltpu.VMEM((6, TQ, 128), F32), pltpu.VMEM((6, TQ, 128), F32)]
    ins.append(_band_bias())
    in_specs.append(_const_spec((3, HEADS_PER_GROUP * QB, KW)))
    return pl.pallas_call(
        _attn_kernel,
        grid=(B, nj),
        in_specs=in_specs,
        out_specs=pl.BlockSpec((1, TQ, D_GRP), lambda b, j: (b, j, 0)),
        out_shape=jax.ShapeDtypeStruct((B, S, D_GRP), BF16),
        scratch_shapes=scratch,
        compiler_params=pltpu.CompilerParams(
            dimension_semantics=("parallel", "parallel"), vmem_limit_bytes=VMEM_LIMIT),
        name="attn",
    )(*ins)


def _post_kernel(o_ref, mix_ref, x_ref, n1_ref, wz_ref, bg_ref, wco_ref, wao_ref, wo_ref, n2_ref,
                 wr_ref, br_ref, tri_ref, x1_ref, h2_ref, rt_ref, cnt_ref, carry):
    first = (pl.program_id(0) == 0) & (pl.program_id(1) == 0)

    @pl.when(first)
    def _():
        carry[...] = jnp.zeros_like(carry)

    x = x_ref[0]
    h = _rms(x, n1_ref[...]).astype(BF16)
    g_a = _sigmoid(jnp.dot(h, wz_ref[:, :D_MODEL], preferred_element_type=F32) + bg_ref[:, :D_MODEL])
    g_b = _sigmoid(jnp.dot(h, wz_ref[:, D_MODEL:], preferred_element_type=F32) + bg_ref[:, D_MODEL:])
    att = jnp.dot(o_ref[0], wao_ref[...], preferred_element_type=F32)
    cvb = jnp.dot(mix_ref[0], wco_ref[...], preferred_element_type=F32)
    merged = (g_a * att + g_b * cvb).astype(BF16)
    x1 = x + jnp.dot(merged, wo_ref[...], preferred_element_type=F32)
    x1_ref[0] = x1
    h2 = _rms(x1, n2_ref[...])
    h2_ref[0] = h2

    lt = lax.dot_general(wr_ref[...], h2.astype(BF16), (((1,), (1,)), ((), ())),
                         preferred_element_type=F32) + br_ref[:, 0:1]
    grow = lax.broadcasted_iota(jnp.int32, (8, TS), 0)
    gl = jnp.where(grow < N_EXPERT_GROUPS, lt[0:8], NEG)
    gmax = jnp.max(gl, axis=0, keepdims=True)
    grp = jnp.min(jnp.where(gl == gmax, grow, 8), axis=0, keepdims=True)
    grp_w = 1.0 / jnp.sum(jnp.exp(gl - gmax), axis=0, keepdims=True)
    erow = lax.broadcasted_iota(jnp.int32, (N_EXPERTS, TS), 0)
    el = jnp.where(erow // EXPERTS_PER_GROUP == grp, lt[8:8 + N_EXPERTS], NEG)
    v1 = jnp.max(el, axis=0, keepdims=True)
    i1 = jnp.min(jnp.where(el == v1, erow, N_EXPERTS), axis=0, keepdims=True)
    el2 = jnp.where(erow == i1, NEG, el)
    v2 = jnp.max(el2, axis=0, keepdims=True)
    i2 = jnp.min(jnp.where(el2 == v2, erow, N_EXPERTS), axis=0, keepdims=True)
    t = jnp.exp(v2 - v1)
    den = 1.0 + t
    gate1 = grp_w * (1.0 / den)
    gate2 = grp_w * (t / den)

    oh1 = (erow == i1).astype(F32)
    oh2 = (erow == i2).astype(F32)
    both = oh1 + oh2
    bothb = both.astype(BF16)
    offset = carry[:, 0:1]
    chunks = []
    for c in range(TS // CUM_CHUNK):
        bc = bothb[:, c * CUM_CHUNK:(c + 1) * CUM_CHUNK]
        chunks.append(offset + jnp.dot(bc, tri_ref[...], preferred_element_type=F32))
        offset = offset + jnp.sum(both[:, c * CUM_CHUNK:(c + 1) * CUM_CHUNK], axis=1, keepdims=True)
    basec = jnp.concatenate(chunks, axis=1)
    rank1 = jnp.sum(oh1 * basec, axis=0, keepdims=True)
    rank2 = jnp.sum(oh2 * basec, axis=0, keepdims=True)
    newc = carry[...] + (offset - carry[:, 0:1])
    carry[...] = newc
    cnt_ref[...] = newc
    zero = jnp.zeros((1, TS), F32)
    rt_ref[0] = jnp.concatenate(
        [i1.astype(F32), i2.astype(F32), gate1, gate2, rank1, rank2, zero, zero], axis=0)


def _post(o_att, mix, x, p, b0, nseq):
    _, S, D = x.shape
    nj = S // TS
    src = lambda w: pl.BlockSpec((1, TS, w), lambda b, j: (b + b0, j, 0))
    tile = lambda w: pl.BlockSpec((1, TS, w), lambda b, j: (b, j, 0))
    return pl.pallas_call(
        _post_kernel,
        grid=(nseq, nj),
        in_specs=[src(D_GRP), src(D_CONV), src(D),
                  _const_spec((1, D)), _const_spec((D, 2 * D)), _const_spec((1, 2 * D)),
                  _const_spec((D_CONV, D)), _const_spec((D_GRP, D)), _const_spec((D, D)),
                  _const_spec((1, D)), _const_spec((ROUTER_ROWS, D)), _const_spec((ROUTER_ROWS, 128)),
                  _const_spec((CUM_CHUNK, CUM_CHUNK))],
        out_specs=[tile(D), tile(D),
                   pl.BlockSpec((1, 8, TS), lambda b, j: (b, 0, j)),
                   pl.BlockSpec((N_EXPERTS, 128), lambda b, j: (0, 0))],
        out_shape=[jax.ShapeDtypeStruct((nseq, S, D), F32), jax.ShapeDtypeStruct((nseq, S, D), F32),
                   jax.ShapeDtypeStruct((nseq, 8, S), F32),
                   jax.ShapeDtypeStruct((N_EXPERTS, 128), F32)],
        scratch_shapes=[pltpu.VMEM((N_EXPERTS, 128), F32)],
        compiler_params=pltpu.CompilerParams(
            dimension_semantics=("arbitrary", "arbitrary"), vmem_limit_bytes=VMEM_LIMIT),
        name="post",
    )(o_att, mix, x, p["n1"], p["w_gate"], p["b_gate"], p["w_co"], p["w_ao"], p["w_o"], p["n2"],
      p["wr"], p["br"], p["tri"])


def _moe_kernel(be_ref, nu_ref, nv_ref, x_ref, wg_ref, wu_ref, wd_ref, out_ref, wgb, wub, wdb):
    b = pl.program_id(0)

    @pl.when(b < nu_ref[0])
    def _():
        @pl.when((b == 0) | (be_ref[b] != be_ref[jnp.maximum(b - 1, 0)]))
        def _():
            wgb[...] = wg_ref[0].astype(BF16)
            wub[...] = wu_ref[0].astype(BF16)
            wdb[...] = wd_ref[0].astype(BF16)

        live = lax.broadcasted_iota(jnp.int32, (MOE_BLK, 1), 0) < nv_ref[b]
        x = jnp.where(live, x_ref[...], 0.0).astype(BF16)
        a = jnp.dot(x, wgb[...], preferred_element_type=F32)
        u = jnp.dot(x, wub[...], preferred_element_type=F32)
        hm = (a * _sigmoid(a) * u).astype(BF16)
        out_ref[...] = jnp.dot(hm, wdb[...], preferred_element_type=F32)

    @pl.when(b >= nu_ref[0])
    def _():
        out_ref[...] = jnp.zeros_like(out_ref)


def _moe(block_e, nused, nvalid, xs, w_g, w_u, w_d):
    nb = block_e.shape[0]
    D = xs.shape[1]
    grid_spec = pltpu.PrefetchScalarGridSpec(
        num_scalar_prefetch=3,
        grid=(nb,),
        in_specs=[
            pl.BlockSpec((MOE_BLK, D), lambda b, be, nu, nv: (b, 0)),
            pl.BlockSpec((1, D, D_EXPERT), lambda b, be, nu, nv: (be[b], 0, 0)),
            pl.BlockSpec((1, D, D_EXPERT), lambda b, be, nu, nv: (be[b], 0, 0)),
            pl.BlockSpec((1, D_EXPERT, D), lambda b, be, nu, nv: (be[b], 0, 0)),
        ],
        out_specs=pl.BlockSpec((MOE_BLK, D), lambda b, be, nu, nv: (b, 0)),
        scratch_shapes=[pltpu.VMEM((D, D_EXPERT), BF16), pltpu.VMEM((D, D_EXPERT), BF16),
                        pltpu.VMEM((D_EXPERT, D), BF16)],
    )
    return pl.pallas_call(
        _moe_kernel,
        grid_spec=grid_spec,
        out_shape=jax.ShapeDtypeStruct((nb * MOE_BLK, D), F32),
        compiler_params=pltpu.CompilerParams(
            dimension_semantics=("arbitrary",), vmem_limit_bytes=VMEM_LIMIT),
        name="moe",
    )(block_e, nused, nvalid, xs, w_g, w_u, w_d)


def _sc_mesh():
    return plsc.VectorSubcoreMesh(core_axis_name="c", subcore_axis_name="s")


def _sc_worker_range(total):
    info = plsc.get_sparse_core_info()
    nw = info.num_cores * info.num_subcores
    wid = lax.axis_index("s") * info.num_cores + lax.axis_index("c")
    per_w = total // nw
    assert per_w * nw == total and per_w % SC_ROWS == 0
    return wid * per_w, per_w // SC_ROWS


def _sc_scatter_rows(src, idx_a, idx_b, nslots):
    T, D = src.shape

    @functools.partial(
        pl.kernel, out_type=jax.ShapeDtypeStruct((nslots, D), src.dtype), mesh=_sc_mesh(),
        scratch_types=[pltpu.VMEM((SC_ROWS,), jnp.int32), pltpu.VMEM((SC_ROWS,), jnp.int32),
                       pltpu.VMEM((SC_ROWS, D), src.dtype), pltpu.SemaphoreType.DMA],
        name="sc_dispatch")
    def k(x_hbm, ia_hbm, ib_hbm, o_hbm, ia_v, ib_v, rows_v, sem):
        start, nchunks = _sc_worker_range(T)

        @pl.loop(0, nchunks)
        def _(c):
            rows = pl.ds(start + c * SC_ROWS, SC_ROWS)
            pltpu.sync_copy(ia_hbm.at[rows], ia_v)
            pltpu.sync_copy(ib_hbm.at[rows], ib_v)
            pltpu.sync_copy(x_hbm.at[rows], rows_v)
            ca = pltpu.async_copy(rows_v, o_hbm.at[ia_v], sem)
            cb = pltpu.async_copy(rows_v, o_hbm.at[ib_v], sem)
            ca.wait()
            cb.wait()

    return k(src, idx_a, idx_b)


def _sc_gather_rows(src, idx_a, idx_b):
    D = src.shape[1]
    T = idx_a.shape[0]
    out = jax.ShapeDtypeStruct((T, D), src.dtype)

    @functools.partial(
        pl.kernel, out_type=(out, out), mesh=_sc_mesh(),
        scratch_types=[pltpu.VMEM((SC_ROWS,), jnp.int32), pltpu.VMEM((SC_ROWS,), jnp.int32),
                       pltpu.VMEM((SC_ROWS, D), src.dtype), pltpu.VMEM((SC_ROWS, D), src.dtype),
                       pltpu.SemaphoreType.DMA],
        name="sc_combine")
    def k(x_hbm, ia_hbm, ib_hbm, oa_hbm, ob_hbm, ia_v, ib_v, ra_v, rb_v, sem):
        start, nchunks = _sc_worker_range(T)

        @pl.loop(0, nchunks)
        def _(c):
            rows = pl.ds(start + c * SC_ROWS, SC_ROWS)
            pltpu.sync_copy(ia_hbm.at[rows], ia_v)
            pltpu.sync_copy(ib_hbm.at[rows], ib_v)
            ca = pltpu.async_copy(x_hbm.at[ia_v], ra_v, sem)
            cb = pltpu.async_copy(x_hbm.at[ib_v], rb_v, sem)
            ca.wait()
            cb.wait()
            pltpu.sync_copy(ra_v, oa_hbm.at[rows])
            pltpu.sync_copy(rb_v, ob_hbm.at[rows])

    return k(src, idx_a, idx_b)


def _combine_kernel(x1_ref, ya_ref, yb_ref, rt_ref, gf_ref, out_ref):
    gt = jnp.transpose(rt_ref[0])
    y = gt[:, 2:3] * ya_ref[...] + gt[:, 3:4] * yb_ref[...]
    out_ref[...] = _rms(x1_ref[...] + y, gf_ref[...])


def _combine(x1, ya, yb, rt, gf, out, row0, total_rows):
    T, D = x1.shape
    per_seq = rt.shape[2] // TS
    tile = pl.BlockSpec((TS, D), lambda i: (i, 0))

    def body(x1_ref, ya_ref, yb_ref, rt_ref, gf_ref, *rest):
        _combine_kernel(x1_ref, ya_ref, yb_ref, rt_ref, gf_ref, rest[-1])

    in_specs = [tile, tile, tile,
                pl.BlockSpec((1, 8, TS), lambda i: (i // per_seq, 0, i % per_seq)),
                _const_spec((1, D))]
    args = [x1, ya, yb, rt, gf]
    aliases = {}
    if out is not None:
        in_specs.append(pl.BlockSpec(memory_space=pl.ANY))
        args.append(out)
        aliases = {5: 0}
    return pl.pallas_call(
        body,
        grid=(T // TS,),
        in_specs=in_specs,
        out_specs=pl.BlockSpec((TS, D), lambda i: (i + row0 // TS, 0)),
        out_shape=jax.ShapeDtypeStruct((total_rows, D), F32),
        input_output_aliases=aliases,
        compiler_params=pltpu.CompilerParams(
            dimension_semantics=("parallel",), vmem_limit_bytes=VMEM_LIMIT),
        name="combine",
    )(*args)


def _slots_kernel(ps_ref, rt_ref, d_ref):
    v = rt_ref[0]
    start = jnp.zeros(v.shape, F32)
    for e in range(N_EXPERTS):
        start = jnp.where(v == float(e), ps_ref[e].astype(F32), start)
    d_ref[0] = (start[0:2] + v[4:6]).astype(jnp.int32)


def _slots(pstarts, rt):
    B, _, S = rt.shape
    grid_spec = pltpu.PrefetchScalarGridSpec(
        num_scalar_prefetch=1,
        grid=(B,),
        in_specs=[pl.BlockSpec((1, 8, S), lambda b, ps: (b, 0, 0))],
        out_specs=pl.BlockSpec((1, 2, S), lambda b, ps: (b, 0, 0)),
    )
    return pl.pallas_call(
        _slots_kernel,
        grid_spec=grid_spec,
        out_shape=jax.ShapeDtypeStruct((B, 2, S), jnp.int32),
        compiler_params=pltpu.CompilerParams(dimension_semantics=("parallel",)),
        name="slots",
    )(pstarts, rt)


def _rotary_tables(S):
    inv_freq = 1.0 / (ROPE_THETA ** (jnp.arange(0, HEAD_DIM, 2, dtype=F32) / HEAD_DIM))
    ang = jnp.arange(S, dtype=F32)[:, None] * inv_freq[None, :]
    cos, sin = jnp.cos(ang), jnp.sin(ang)
    z = jnp.zeros_like(sin)
    cos_t = jnp.concatenate([cos, cos, cos, cos], axis=1)
    sa_t = jnp.concatenate([-sin, z, -sin, z], axis=1)
    sb_t = jnp.concatenate([z, sin, z, sin], axis=1)
    return cos_t, sa_t, sb_t


def _moe_segment(o_att, mix, x, p, b0, nseq, out):
    _, S, D = x.shape
    T = nseq * S
    x1, h2, rt, cnt = _post(o_att, mix, x, p, b0, nseq)

    counts = cnt[:, 0].astype(jnp.int32)
    pcounts = (counts + MOE_BLK - 1) // MOE_BLK * MOE_BLK
    pends = jnp.cumsum(pcounts)
    pstarts = pends - pcounts
    dest = _slots(pstarts, rt)
    d1 = dest[:, 0, :].reshape(T)
    d2 = dest[:, 1, :].reshape(T)
    nb = (2 * T) // MOE_BLK + N_EXPERTS
    block_start = jnp.arange(nb, dtype=jnp.int32) * MOE_BLK
    block_e = jnp.minimum(jnp.sum((pends[None, :] <= block_start[:, None]).astype(jnp.int32), axis=1),
                          N_EXPERTS - 1)
    nused = (pends[-1:] // MOE_BLK).astype(jnp.int32)
    mine = block_e[:, None] == jnp.arange(N_EXPERTS, dtype=jnp.int32)[None, :]
    valid_end = jnp.sum(jnp.where(mine, (pstarts + counts)[None, :], 0), axis=1)
    nvalid = jnp.clip(valid_end - block_start, 0, MOE_BLK)

    xs = _sc_scatter_rows(h2.reshape(T, D), d1, d2, nb * MOE_BLK)
    yb = _moe(block_e, nused, nvalid, xs, p["w_g"], p["w_u"], p["w_d"])
    ya, yc = _sc_gather_rows(yb, d1, d2)
    return _combine(x1.reshape(T, D), ya, yc, rt, p["gf"], out, b0 * S, x.shape[0] * S)


def _trunk(x, p):
    B, S, D = x.shape
    outs = _in_proj(x, p["n1"], p["w_qkvc"], p["conv_w"], *p["rot"])
    o_att = _attn(outs[:9])
    nseg = max(1, B // SEG_SEQS)
    out = None
    for seg in range(nseg):
        out = _moe_segment(o_att, outs[9], x, p, seg * (B // nseg), B // nseg, out)
    return out.reshape(B, S, D)


def kernel(x_prompt, x_sample, norm1_g, w_in, b_gate, conv_w, w_attn_out, w_conv_out, w_out, norm2_g,
           w_router_group, b_router_group, w_router_expert, b_router_expert, w_exp_gate, w_exp_up,
           w_exp_down, norm_f_g):
    assert norm1_g.shape[0] == 1, "single-layer trunk"
    S = x_prompt.shape[1]
    wr = jnp.zeros((ROUTER_ROWS, D_MODEL), F32)
    wr = wr.at[0:N_EXPERT_GROUPS].set(w_router_group[0].T).at[8:8 + N_EXPERTS].set(w_router_expert[0].T)
    br = jnp.zeros((ROUTER_ROWS,), F32)
    br = br.at[0:N_EXPERT_GROUPS].set(b_router_group[0]).at[8:8 + N_EXPERTS].set(b_router_expert[0])
    ti = jnp.arange(CUM_CHUNK)
    p = dict(
        n1=norm1_g, w_qkvc=w_in[0, :, :C_ZA].astype(BF16), w_gate=w_in[0, :, C_ZA:].astype(BF16),
        b_gate=b_gate, conv_w=conv_w[0], w_co=w_conv_out[0].astype(BF16), rot=_rotary_tables(S),
        w_ao=w_attn_out[0].astype(BF16), w_o=w_out[0].astype(BF16), n2=norm2_g,
        wr=wr.astype(BF16), br=jnp.broadcast_to(br[:, None], (ROUTER_ROWS, 128)),
        tri=(ti[:, None] < ti[None, :]).astype(BF16),
        w_g=w_exp_gate[0], w_u=w_exp_up[0], w_d=w_exp_down[0],
        gf=norm_f_g.reshape(1, D_MODEL),
    )
    return _trunk(x_prompt, p), _trunk(x_sample, p)
```

```python
import functools

import jax
import jax.numpy as jnp
from jax import lax
from jax.experimental import pallas as pl
from jax.experimental.pallas import tpu as pltpu
from jax.experimental.pallas import tpu_sc as plsc

D_MODEL = 1024
HEAD_DIM = 64
HEADS_PER_GROUP = 4
DILATIONS = (1, 4, 16)
HALF = 64
D_GRP = HEADS_PER_GROUP * HEAD_DIM
D_ATT = 3 * D_GRP
D_CONV = 768
N_EXPERT_GROUPS = 4
EXPERTS_PER_GROUP = 8
N_EXPERTS = 32
D_EXPERT = 512
RMS_EPS = 1e-6
NEG = -1e30
ROPE_THETA = 10000.0

C_Q, C_K, C_V = 0, D_ATT, 2 * D_ATT
C_CU, C_CB, C_CC = 3 * D_ATT, 3 * D_ATT + D_CONV, 3 * D_ATT + 2 * D_CONV
C_ZA = 3 * D_ATT + 3 * D_CONV

TSI = 1024
TS = 1024
TQ = 2048
QB = 128
KW = QB + 2 * HALF
MOE_BLK = 512
SEG_SEQS = 4
SC_ROWS = 32
ROUTER_ROWS = 48
CUM_CHUNK = 256
VMEM_LIMIT = 56 * 1024 * 1024

F32 = jnp.float32
BF16 = jnp.bfloat16


def _rms(xf, g):
    return xf * lax.rsqrt(jnp.mean(xf * xf, axis=-1, keepdims=True) + RMS_EPS) * g


def _sigmoid(x):
    return 1.0 / (1.0 + jnp.exp(-x))


def _const_spec(shape):
    return pl.BlockSpec(shape, lambda *_: (0,) * len(shape), pipeline_mode=pl.Buffered(1))


def _in_proj_kernel(x_ref, xp_ref, xn_ref, n1_ref, w_ref, cw_ref, cos_ref, sa_ref, sb_ref,
                    q1, k1, v1, q2, k2, v2, q3, k3, v3, mix_ref, pbuf, dbuf):
    j = pl.program_id(1)
    nj = pl.num_programs(1)
    g1 = n1_ref[...]
    hf = _rms(x_ref[0], g1)
    h = hf.astype(BF16)

    def proj(hh, c0, width):
        return jnp.dot(hh, w_ref[:, c0:c0 + width], preferred_element_type=F32)

    cosv = cos_ref[...]
    sav = sa_ref[...]
    sbv = sb_ref[...]

    def rotary(z):
        return z * cosv + pltpu.roll(z, 96, 1) * sav + pltpu.roll(z, 32, 1) * sbv

    def emit(outs, c0, rot, scale):
        for g, d in enumerate(DILATIONS):
            z = proj(h, c0 + g * D_GRP, D_GRP)
            if rot:
                z = jnp.concatenate([rotary(z[:, :128]), rotary(z[:, 128:])], axis=1)
            if scale != 1.0:
                z = z * scale
            if d == 1:
                outs[g][0, 0] = z.astype(BF16)
            else:
                dbuf[0] = z[:, :128]
                dbuf[1] = z[:, 128:]
                for r in range(d):
                    for c in range(2):
                        outs[g][0, r, :, c * 128:(c + 1) * 128] = (
                            dbuf[c, pl.ds(r, TSI // d, stride=d), :].astype(BF16))

    emit((q1, q2, q3), C_Q, True, HEAD_DIM ** -0.5)
    emit((k1, k2, k3), C_K, True, 1.0)
    emit((v1, v2, v3), C_V, False, 1.0)

    hp = _rms(xp_ref[0], g1)
    hn = _rms(xn_ref[0], g1)
    he = jnp.concatenate([hp, hf, hn], axis=0).astype(BF16)
    pe = proj(he, C_CC, D_CONV) * proj(he, C_CU, D_CONV)
    erow = lax.broadcasted_iota(jnp.int32, (TSI + 16, 1), 0)
    outside = ((erow < 8) & (j == 0)) | ((erow >= 8 + TSI) & (j == nj - 1))
    pbuf[...] = jnp.where(outside, 0.0, pe)
    conv = (cw_ref[0:1, :] * pbuf[7:7 + TSI, :] + cw_ref[1:2, :] * pbuf[8:8 + TSI, :]
            + cw_ref[2:3, :] * pbuf[9:9 + TSI, :])
    mix_ref[0] = (proj(h, C_CB, D_CONV) * conv).astype(BF16)


def _in_proj(x, n1, w_qkvc, conv_w, cos_t, sa_t, sb_t):
    B, S, D = x.shape
    nj = S // TSI
    rot_spec = pl.BlockSpec((TSI, 128), lambda b, j: (j, 0))
    in_specs = [
        pl.BlockSpec((1, TSI, D), lambda b, j: (b, j, 0)),
        pl.BlockSpec((1, 8, D), lambda b, j: (b, jnp.maximum(j * (TSI // 8) - 1, 0), 0)),
        pl.BlockSpec((1, 8, D), lambda b, j: (b, jnp.minimum((j + 1) * (TSI // 8), S // 8 - 1), 0)),
        _const_spec((1, D)),
        _const_spec(w_qkvc.shape),
        _const_spec((3, D_CONV)),
        rot_spec, rot_spec, rot_spec,
    ]
    out_shape, out_specs = [], []
    for d in DILATIONS:
        for _ in range(3):
            out_shape.append(jax.ShapeDtypeStruct((B, d, S // d, D_GRP), BF16))
            out_specs.append(pl.BlockSpec((1, d, TSI // d, D_GRP), lambda b, j: (b, 0, j, 0)))
    out_shape.append(jax.ShapeDtypeStruct((B, S, D_CONV), BF16))
    out_specs.append(pl.BlockSpec((1, TSI, D_CONV), lambda b, j: (b, j, 0)))
    return pl.pallas_call(
        _in_proj_kernel,
        grid=(B, nj),
        in_specs=in_specs,
        out_specs=out_specs,
        out_shape=out_shape,
        scratch_shapes=[pltpu.VMEM((TSI + 16, D_CONV), F32), pltpu.VMEM((2, TSI, 128), F32)],
        compiler_params=pltpu.CompilerParams(
            dimension_semantics=("parallel", "parallel"), vmem_limit_bytes=VMEM_LIMIT),
        name="in_proj",
    )(x, x, x, n1, w_qkvc, conv_w, cos_t, sa_t, sb_t)


def _attn_kernel(*refs):
    ins = refs[:21]
    bias_ref = refs[21]
    o_ref = refs[22]
    kbufs = refs[23:29]
    os_ref, ls_ref = refs[29], refs[30]
    j = pl.program_id(1)
    nj = pl.num_programs(1)

    lane = lax.broadcasted_iota(jnp.int32, (QB, D_GRP), 1)
    head_of_lane = lane // HEAD_DIM

    for g, d in enumerate(DILATIONS):
        q_ref, kc, kp, kn, vc, vp, vn = ins[7 * g:7 * g + 7]
        kb, vb = kbufs[2 * g], kbufs[2 * g + 1]
        n = TQ // d
        nblk = n // QB
        for buf, prev, cur, nxt in ((kb, kp, kc, kn), (vb, vp, vc, vn)):
            buf[:, 0:HALF, :] = prev[0]
            buf[:, HALF:HALF + n, :] = cur[0]
            buf[:, HALF + n:, :] = nxt[0]

        def scores(idx, q_ref=q_ref, kb=kb, nblk=nblk):
            r, jb = divmod(idx, nblk)
            qb = q_ref[0, r, jb * QB:(jb + 1) * QB, :]
            kw = kb[r, jb * QB:jb * QB + KW, :]
            qs = jnp.concatenate(
                [jnp.where(head_of_lane == hh, qb, jnp.zeros_like(qb)) for hh in range(HEADS_PER_GROUP)],
                axis=0)
            s = lax.dot_general(qs, kw, (((1,), (1,)), ((), ())), preferred_element_type=F32)
            variant = 0
            if jb == 0:
                variant = jnp.where(j == 0, 1, variant)
            if jb == nblk - 1:
                variant = jnp.where(j == nj - 1, 2, variant)
            return s + bias_ref[variant]

        def softmax(s):
            m = jnp.max(s, axis=1, keepdims=True)
            p = jnp.exp(s - m)
            l = jnp.sum(p, axis=1, keepdims=True)
            return p.astype(BF16), 1.0 / l, m + jnp.log(l)

        def finish(idx, pb, inv_l, lse, vb=vb, d=d, nblk=nblk, g=g):
            r, jb = divmod(idx, nblk)
            pv = jnp.dot(pb, vb[r, jb * QB:jb * QB + KW, :], preferred_element_type=F32)
            o = jnp.zeros((QB, D_GRP), F32)
            ls = jnp.zeros((QB, D_GRP), F32)
            for hh in range(HEADS_PER_GROUP):
                sl = slice(hh * QB, (hh + 1) * QB)
                sel = head_of_lane == hh
                o = jnp.where(sel, pv[sl] * inv_l[sl], o)
                ls = jnp.where(sel, lse[sl], ls)
            for c in range(2):
                cs = slice(c * 128, (c + 1) * 128)
                rows = pl.ds(jb * QB, QB) if d == 1 else pl.ds(r + jb * QB * d, QB, stride=d)
                os_ref[2 * g + c, rows, :] = o[:, cs]
                ls_ref[2 * g + c, rows, :] = ls[:, cs]

        nb = d * nblk
        s_of, sm_of = {}, {}
        for i in range(nb + 2):
            if i >= 2:
                finish(i - 2, *sm_of.pop(i - 2))
            if i < nb:
                s_of[i] = scores(i)
            if 1 <= i <= nb:
                sm_of[i - 1] = softmax(s_of.pop(i - 1))

    def merge(c, carry):
        rows = pl.ds(pl.multiple_of(c * QB, QB), QB)
        for c in range(2):
            l0, l1, l2 = ls_ref[c, rows, :], ls_ref[2 + c, rows, :], ls_ref[4 + c, rows, :]
            mm = jnp.maximum(jnp.maximum(l0, l1), l2)
            w0, w1, w2 = jnp.exp(l0 - mm), jnp.exp(l1 - mm), jnp.exp(l2 - mm)
            o = (w0 * os_ref[c, rows, :] + w1 * os_ref[2 + c, rows, :]
                 + w2 * os_ref[4 + c, rows, :]) / (w0 + w1 + w2)
            o_ref[0, rows, c * 128:(c + 1) * 128] = o.astype(BF16)
        return carry

    lax.fori_loop(0, TQ // QB, merge, 0)


def _band_bias():
    row = jnp.arange(HEADS_PER_GROUP * QB)[:, None] % QB
    col = jnp.arange(KW)[None, :]
    band = (col >= row) & (col <= row + 2 * HALF)
    variants = (band, band & (col >= HALF), band & (col < KW - HALF))
    return jnp.stack([jnp.where(v, 0.0, NEG).astype(F32) for v in variants])


def _attn(qkv):
    B = qkv[0].shape[0]
    S = qkv[0].shape[2]
    nj = S // TQ
    assert nj > 1
    ins, in_specs, scratch = [], [], []
    for g, d in enumerate(DILATIONS):
        q, k, v = qkv[3 * g:3 * g + 3]
        n = TQ // d
        L = S // d
        nh = n // HALF
        cur = pl.BlockSpec((1, d, n, D_GRP), lambda b, j: (b, 0, j, 0))
        prev = pl.BlockSpec((1, d, HALF, D_GRP),
                            lambda b, j, nh=nh: (b, 0, jnp.maximum(j * nh - 1, 0), 0))
        nxt = pl.BlockSpec((1, d, HALF, D_GRP),
                           lambda b, j, nh=nh, L=L: (b, 0, jnp.minimum((j + 1) * nh, L // HALF - 1), 0))
        ins += [q, k, k, k, v, v, v]
        in_specs += [cur, cur, prev, nxt, cur, prev, nxt]
        scratch += [pltpu.VMEM((d, n + 2 * HALF, D_GRP), BF16)] * 2
    scratch += [pltpu.VMEM((6, TQ, 128), F32), pltpu.VMEM((6, TQ, 128), F32)]
    ins.append(_band_bias())
    in_specs.append(_const_spec((3, HEADS_PER_GROUP * QB, KW)))
    return pl.pallas_call(
        _attn_kernel,
        grid=(B, nj),
        in_specs=in_specs,
        out_specs=pl.BlockSpec((1, TQ, D_GRP), lambda b, j: (b, j, 0)),
        out_shape=jax.ShapeDtypeStruct((B, S, D_GRP), BF16),
        scratch_shapes=scratch,
        compiler_params=pltpu.CompilerParams(
            dimension_semantics=("parallel", "parallel"), vmem_limit_bytes=VMEM_LIMIT),
        name="attn",
    )(*ins)


def _post_kernel(o_ref, mix_ref, x_ref, n1_ref, wz_ref, bg_ref, wco_ref, wao_ref, wo_ref, n2_ref,
                 wr_ref, br_ref, tri_ref, x1_ref, h2_ref, rt_ref, cnt_ref, carry):
    first = (pl.program_id(0) == 0) & (pl.program_id(1) == 0)

    @pl.when(first)
    def _():
        carry[...] = jnp.zeros_like(carry)

    x = x_ref[0]
    h = _rms(x, n1_ref[...]).astype(BF16)
    g_a = _sigmoid(jnp.dot(h, wz_ref[:, :D_MODEL], preferred_element_type=F32) + bg_ref[:, :D_MODEL])
    g_b = _sigmoid(jnp.dot(h, wz_ref[:, D_MODEL:], preferred_element_type=F32) + bg_ref[:, D_MODEL:])
    att = jnp.dot(o_ref[0], wao_ref[...], preferred_element_type=F32)
    cvb = jnp.dot(mix_ref[0], wco_ref[...], preferred_element_type=F32)
    merged = (g_a * att + g_b * cvb).astype(BF16)
    x1 = x + jnp.dot(merged, wo_ref[...], preferred_element_type=F32)
    x1_ref[0] = x1
    h2 = _rms(x1, n2_ref[...])
    h2_ref[0] = h2

    lt = lax.dot_general(wr_ref[...], h2.astype(BF16), (((1,), (1,)), ((), ())),
                         preferred_element_type=F32) + br_ref[:, 0:1]
    grow = lax.broadcasted_iota(jnp.int32, (8, TS), 0)
    gl = jnp.where(grow < N_EXPERT_GROUPS, lt[0:8], NEG)
    gmax = jnp.max(gl, axis=0, keepdims=True)
    grp = jnp.min(jnp.where(gl == gmax, grow, 8), axis=0, keepdims=True)
    grp_w = 1.0 / jnp.sum(jnp.exp(gl - gmax), axis=0, keepdims=True)
    erow = lax.broadcasted_iota(jnp.int32, (N_EXPERTS, TS), 0)
    el = jnp.where(erow // EXPERTS_PER_GROUP == grp, lt[8:8 + N_EXPERTS], NEG)
    v1 = jnp.max(el, axis=0, keepdims=True)
    i1 = jnp.min(jnp.where(el == v1, erow, N_EXPERTS), axis=0, keepdims=True)
    el2 = jnp.where(erow == i1, NEG, el)
    v2 = jnp.max(el2, axis=0, keepdims=True)
    i2 = jnp.min(jnp.where(el2 == v2, erow, N_EXPERTS), axis=0, keepdims=True)
    t = jnp.exp(v2 - v1)
    den = 1.0 + t
    gate1 = grp_w * (1.0 / den)
    gate2 = grp_w * (t / den)

    oh1 = (erow == i1).astype(F32)
    oh2 = (erow == i2).astype(F32)
    both = oh1 + oh2
    bothb = both.astype(BF16)
    offset = carry[:, 0:1]
    chunks = []
    for c in range(TS // CUM_CHUNK):
        bc = bothb[:, c * CUM_CHUNK:(c + 1) * CUM_CHUNK]
        chunks.append(offset + jnp.dot(bc, tri_ref[...], preferred_element_type=F32))
        offset = offset + jnp.sum(both[:, c * CUM_CHUNK:(c + 1) * CUM_CHUNK], axis=1, keepdims=True)
    basec = jnp.concatenate(chunks, axis=1)
    rank1 = jnp.sum(oh1 * basec, axis=0, keepdims=True)
    rank2 = jnp.sum(oh2 * basec, axis=0, keepdims=True)
    newc = carry[...] + (offset - carry[:, 0:1])
    carry[...] = newc
    cnt_ref[...] = newc
    zero = jnp.zeros((1, TS), F32)
    rt_ref[0] = jnp.concatenate(
        [i1.astype(F32), i2.astype(F32), gate1, gate2, rank1, rank2, zero, zero], axis=0)


def _post(o_att, mix, x, p, b0, nseq):
    _, S, D = x.shape
    nj = S // TS
    src = lambda w: pl.BlockSpec((1, TS, w), lambda b, j: (b + b0, j, 0))
    tile = lambda w: pl.BlockSpec((1, TS, w), lambda b, j: (b, j, 0))
    return pl.pallas_call(
        _post_kernel,
        grid=(nseq, nj),
        in_specs=[src(D_GRP), src(D_CONV), src(D),
                  _const_spec((1, D)), _const_spec((D, 2 * D)), _const_spec((1, 2 * D)),
                  _const_spec((D_CONV, D)), _const_spec((D_GRP, D)), _const_spec((D, D)),
                  _const_spec((1, D)), _const_spec((ROUTER_ROWS, D)), _const_spec((ROUTER_ROWS, 128)),
                  _const_spec((CUM_CHUNK, CUM_CHUNK))],
        out_specs=[tile(D), tile(D),
                   pl.BlockSpec((1, 8, TS), lambda b, j: (b, 0, j)),
                   pl.BlockSpec((N_EXPERTS, 128), lambda b, j: (0, 0))],
        out_shape=[jax.ShapeDtypeStruct((nseq, S, D), F32), jax.ShapeDtypeStruct((nseq, S, D), F32),
                   jax.ShapeDtypeStruct((nseq, 8, S), F32),
                   jax.ShapeDtypeStruct((N_EXPERTS, 128), F32)],
        scratch_shapes=[pltpu.VMEM((N_EXPERTS, 128), F32)],
        compiler_params=pltpu.CompilerParams(
            dimension_semantics=("arbitrary", "arbitrary"), vmem_limit_bytes=VMEM_LIMIT),
        name="post",
    )(o_att, mix, x, p["n1"], p["w_gate"], p["b_gate"], p["w_co"], p["w_ao"], p["w_o"], p["n2"],
      p["wr"], p["br"], p["tri"])


def _moe_kernel(be_ref, nu_ref, nv_ref, nx_ref, x_ref, wg_hbm, wu_hbm, wd_hbm, out_ref,
                wgf, wuf, wdf, wgb, wub, wdb, sem, nchg):
    b = pl.program_id(0)

    def weight_copies(e, slot):
        return (pltpu.make_async_copy(wg_hbm.at[e], wgf.at[slot], sem.at[slot]),
                pltpu.make_async_copy(wu_hbm.at[e], wuf.at[slot], sem.at[slot]),
                pltpu.make_async_copy(wd_hbm.at[e], wdf.at[slot], sem.at[slot]))

    @pl.when(b == 0)
    def _():
        nchg[0] = 0
        for cp in weight_copies(be_ref[0], 0):
            cp.start()

    @pl.when(b < nu_ref[0])
    def _():
        @pl.when((b == 0) | (be_ref[b] != be_ref[jnp.maximum(b - 1, 0)]))
        def _():
            slot = nchg[0] % 2
            for cp in weight_copies(be_ref[b], slot):
                cp.wait()
            wgb[...] = wgf[slot].astype(BF16)
            wub[...] = wuf[slot].astype(BF16)
            wdb[...] = wdf[slot].astype(BF16)

            @pl.when(nx_ref[b] != be_ref[b])
            def _():
                for cp in weight_copies(nx_ref[b], 1 - slot):
                    cp.start()

            nchg[0] = nchg[0] + 1

        live = lax.broadcasted_iota(jnp.int32, (MOE_BLK, 1), 0) < nv_ref[b]
        x = jnp.where(live, x_ref[...], 0.0).astype(BF16)
        a = jnp.dot(x, wgb[...], preferred_element_type=F32)
        u = jnp.dot(x, wub[...], preferred_element_type=F32)
        hm = (a * _sigmoid(a) * u).astype(BF16)
        out_ref[...] = jnp.dot(hm, wdb[...], preferred_element_type=F32)

    @pl.when(b >= nu_ref[0])
    def _():
        out_ref[...] = jnp.zeros_like(out_ref)


def _moe(block_e, nused, nvalid, next_e, xs, w_g, w_u, w_d):
    nb = block_e.shape[0]
    D = xs.shape[1]
    grid_spec = pltpu.PrefetchScalarGridSpec(
        num_scalar_prefetch=4,
        grid=(nb,),
        in_specs=[
            pl.BlockSpec((MOE_BLK, D), lambda b, *_: (b, 0)),
            pl.BlockSpec(memory_space=pl.ANY),
            pl.BlockSpec(memory_space=pl.ANY),
            pl.BlockSpec(memory_space=pl.ANY),
        ],
        out_specs=pl.BlockSpec((MOE_BLK, D), lambda b, *_: (b, 0)),
        scratch_shapes=[pltpu.VMEM((2, D, D_EXPERT), F32), pltpu.VMEM((2, D, D_EXPERT), F32),
                        pltpu.VMEM((2, D_EXPERT, D), F32),
                        pltpu.VMEM((D, D_EXPERT), BF16), pltpu.VMEM((D, D_EXPERT), BF16),
                        pltpu.VMEM((D_EXPERT, D), BF16),
                        pltpu.SemaphoreType.DMA((2,)), pltpu.SMEM((1,), jnp.int32)],
    )
    return pl.pallas_call(
        _moe_kernel,
        grid_spec=grid_spec,
        out_shape=jax.ShapeDtypeStruct((nb * MOE_BLK, D), F32),
        compiler_params=pltpu.CompilerParams(
            dimension_semantics=("arbitrary",), vmem_limit_bytes=VMEM_LIMIT),
        name="moe",
    )(block_e, nused, nvalid, next_e, xs, w_g, w_u, w_d)


def _sc_mesh():
    return plsc.VectorSubcoreMesh(core_axis_name="c", subcore_axis_name="s")


def _sc_worker_range(total):
    info = plsc.get_sparse_core_info()
    nw = info.num_cores * info.num_subcores
    wid = lax.axis_index("s") * info.num_cores + lax.axis_index("c")
    per_w = total // nw
    assert per_w * nw == total and per_w % SC_ROWS == 0
    return wid * per_w, per_w // SC_ROWS


def _sc_scatter_rows(src, idx_a, idx_b, nslots):
    T, D = src.shape

    @functools.partial(
        pl.kernel, out_type=jax.ShapeDtypeStruct((nslots, D), src.dtype), mesh=_sc_mesh(),
        scratch_types=[pltpu.VMEM((SC_ROWS,), jnp.int32), pltpu.VMEM((SC_ROWS,), jnp.int32),
                       pltpu.VMEM((SC_ROWS, D), src.dtype), pltpu.SemaphoreType.DMA],
        name="sc_dispatch")
    def k(x_hbm, ia_hbm, ib_hbm, o_hbm, ia_v, ib_v, rows_v, sem):
        start, nchunks = _sc_worker_range(T)

        @pl.loop(0, nchunks)
        def _(c):
            rows = pl.ds(start + c * SC_ROWS, SC_ROWS)
            pltpu.sync_copy(ia_hbm.at[rows], ia_v)
            pltpu.sync_copy(ib_hbm.at[rows], ib_v)
            pltpu.sync_copy(x_hbm.at[rows], rows_v)
            ca = pltpu.async_copy(rows_v, o_hbm.at[ia_v], sem)
            cb = pltpu.async_copy(rows_v, o_hbm.at[ib_v], sem)
            ca.wait()
            cb.wait()

    return k(src, idx_a, idx_b)


def _sc_gather_rows(src, idx_a, idx_b):
    D = src.shape[1]
    T = idx_a.shape[0]
    out = jax.ShapeDtypeStruct((T, D), src.dtype)

    @functools.partial(
        pl.kernel, out_type=(out, out), mesh=_sc_mesh(),
        scratch_types=[pltpu.VMEM((SC_ROWS,), jnp.int32), pltpu.VMEM((SC_ROWS,), jnp.int32),
                       pltpu.VMEM((SC_ROWS, D), src.dtype), pltpu.VMEM((SC_ROWS, D), src.dtype),
                       pltpu.SemaphoreType.DMA],
        name="sc_combine")
    def k(x_hbm, ia_hbm, ib_hbm, oa_hbm, ob_hbm, ia_v, ib_v, ra_v, rb_v, sem):
        start, nchunks = _sc_worker_range(T)

        @pl.loop(0, nchunks)
        def _(c):
            rows = pl.ds(start + c * SC_ROWS, SC_ROWS)
            pltpu.sync_copy(ia_hbm.at[rows], ia_v)
            pltpu.sync_copy(ib_hbm.at[rows], ib_v)
            ca = pltpu.async_copy(x_hbm.at[ia_v], ra_v, sem)
            cb = pltpu.async_copy(x_hbm.at[ib_v], rb_v, sem)
            ca.wait()
            cb.wait()
            pltpu.sync_copy(ra_v, oa_hbm.at[rows])
            pltpu.sync_copy(rb_v, ob_hbm.at[rows])

    return k(src, idx_a, idx_b)


def _combine_kernel(x1_ref, ya_ref, yb_ref, rt_ref, gf_ref, out_ref):
    gt = jnp.transpose(rt_ref[0])
    y = gt[:, 2:3] * ya_ref[...] + gt[:, 3:4] * yb_ref[...]
    out_ref[...] = _rms(x1_ref[...] + y, gf_ref[...])


def _combine(x1, ya, yb, rt, gf, out, row0, total_rows):
    T, D = x1.shape
    per_seq = rt.shape[2] // TS
    tile = pl.BlockSpec((TS, D), lambda i: (i, 0))

    def body(x1_ref, ya_ref, yb_ref, rt_ref, gf_ref, *rest):
        _combine_kernel(x1_ref, ya_ref, yb_ref, rt_ref, gf_ref, rest[-1])

    in_specs = [tile, tile, tile,
                pl.BlockSpec((1, 8, TS), lambda i: (i // per_seq, 0, i % per_seq)),
                _const_spec((1, D))]
    args = [x1, ya, yb, rt, gf]
    aliases = {}
    if out is not None:
        in_specs.append(pl.BlockSpec(memory_space=pl.ANY))
        args.append(out)
        aliases = {5: 0}
    return pl.pallas_call(
        body,
        grid=(T // TS,),
        in_specs=in_specs,
        out_specs=pl.BlockSpec((TS, D), lambda i: (i + row0 // TS, 0)),
        out_shape=jax.ShapeDtypeStruct((total_rows, D), F32),
        input_output_aliases=aliases,
        compiler_params=pltpu.CompilerParams(
            dimension_semantics=("parallel",), vmem_limit_bytes=VMEM_LIMIT),
        name="combine",
    )(*args)


def _slots_kernel(ps_ref, rt_ref, d_ref):
    v = rt_ref[0]
    start = jnp.zeros(v.shape, F32)
    for e in range(N_EXPERTS):
        start = jnp.where(v == float(e), ps_ref[e].astype(F32), start)
    d_ref[0] = (start[0:2] + v[4:6]).astype(jnp.int32)


def _slots(pstarts, rt):
    B, _, S = rt.shape
    grid_spec = pltpu.PrefetchScalarGridSpec(
        num_scalar_prefetch=1,
        grid=(B,),
        in_specs=[pl.BlockSpec((1, 8, S), lambda b, ps: (b, 0, 0))],
        out_specs=pl.BlockSpec((1, 2, S), lambda b, ps: (b, 0, 0)),
    )
    return pl.pallas_call(
        _slots_kernel,
        grid_spec=grid_spec,
        out_shape=jax.ShapeDtypeStruct((B, 2, S), jnp.int32),
        compiler_params=pltpu.CompilerParams(dimension_semantics=("parallel",)),
        name="slots",
    )(pstarts, rt)


def _rotary_tables(S):
    inv_freq = 1.0 / (ROPE_THETA ** (jnp.arange(0, HEAD_DIM, 2, dtype=F32) / HEAD_DIM))
    ang = jnp.arange(S, dtype=F32)[:, None] * inv_freq[None, :]
    cos, sin = jnp.cos(ang), jnp.sin(ang)
    z = jnp.zeros_like(sin)
    cos_t = jnp.concatenate([cos, cos, cos, cos], axis=1)
    sa_t = jnp.concatenate([-sin, z, -sin, z], axis=1)
    sb_t = jnp.concatenate([z, sin, z, sin], axis=1)
    return cos_t, sa_t, sb_t


def _moe_segment(o_att, mix, x, p, b0, nseq, out):
    _, S, D = x.shape
    T = nseq * S
    x1, h2, rt, cnt = _post(o_att, mix, x, p, b0, nseq)

    counts = cnt[:, 0].astype(jnp.int32)
    pcounts = (counts + MOE_BLK - 1) // MOE_BLK * MOE_BLK
    pends = jnp.cumsum(pcounts)
    pstarts = pends - pcounts
    dest = _slots(pstarts, rt)
    d1 = dest[:, 0, :].reshape(T)
    d2 = dest[:, 1, :].reshape(T)
    nb = (2 * T) // MOE_BLK + N_EXPERTS
    block_start = jnp.arange(nb, dtype=jnp.int32) * MOE_BLK
    block_e = jnp.minimum(jnp.sum((pends[None, :] <= block_start[:, None]).astype(jnp.int32), axis=1),
                          N_EXPERTS - 1)
    nused = (pends[-1:] // MOE_BLK).astype(jnp.int32)
    mine = block_e[:, None] == jnp.arange(N_EXPERTS, dtype=jnp.int32)[None, :]
    valid_end = jnp.sum(jnp.where(mine, (pstarts + counts)[None, :], 0), axis=1)
    nvalid = jnp.clip(valid_end - block_start, 0, MOE_BLK)

    xs = _sc_scatter_rows(h2.reshape(T, D), d1, d2, nb * MOE_BLK)
    experts = jnp.arange(N_EXPERTS, dtype=jnp.int32)
    later = (experts[None, :] > block_e[:, None]) & (counts[None, :] > 0)
    next_e = jnp.min(jnp.where(later, experts[None, :], N_EXPERTS), axis=1)
    next_e = jnp.where(next_e == N_EXPERTS, block_e, next_e)
    yb = _moe(block_e, nused, nvalid, next_e, xs, p["w_g"], p["w_u"], p["w_d"])
    ya, yc = _sc_gather_rows(yb, d1, d2)
    return _combine(x1.reshape(T, D), ya, yc, rt, p["gf"], out, b0 * S, x.shape[0] * S)


def _trunk(x, p):
    B, S, D = x.shape
    outs = _in_proj(x, p["n1"], p["w_qkvc"], p["conv_w"], *p["rot"])
    o_att = _attn(outs[:9])
    nseg = max(1, B // SEG_SEQS)
    out = None
    for seg in range(nseg):
        out = _moe_segment(o_att, outs[9], x, p, seg * (B // nseg), B // nseg, out)
    return out.reshape(B, S, D)


def kernel(x_prompt, x_sample, norm1_g, w_in, b_gate, conv_w, w_attn_out, w_conv_out, w_out, norm2_g,
           w_router_group, b_router_group, w_router_expert, b_router_expert, w_exp_gate, w_exp_up,
           w_exp_down, norm_f_g):
    assert norm1_g.shape[0] == 1, "single-layer trunk"
    S = x_prompt.shape[1]
    wr = jnp.zeros((ROUTER_ROWS, D_MODEL), F32)
    wr = wr.at[0:N_EXPERT_GROUPS].set(w_router_group[0].T).at[8:8 + N_EXPERTS].set(w_router_expert[0].T)
    br = jnp.zeros((ROUTER_ROWS,), F32)
    br = br.at[0:N_EXPERT_GROUPS].set(b_router_group[0]).at[8:8 + N_EXPERTS].set(b_router_expert[0])
    ti = jnp.arange(CUM_CHUNK)
    p = dict(
        n1=norm1_g, w_qkvc=w_in[0, :, :C_ZA].astype(BF16), w_gate=w_in[0, :, C_ZA:].astype(BF16),
        b_gate=b_gate, conv_w=conv_w[0], w_co=w_conv_out[0].astype(BF16), rot=_rotary_tables(S),
        w_ao=w_attn_out[0].astype(BF16), w_o=w_out[0].astype(BF16), n2=norm2_g,
        wr=wr.astype(BF16), br=jnp.broadcast_to(br[:, None], (ROUTER_ROWS, 128)),
        tri=(ti[:, None] < ti[None, :]).astype(BF16),
        w_g=w_exp_gate[0], w_u=w_exp_up[0], w_d=w_exp_down[0],
        gf=norm_f_g.reshape(1, D_MODEL),
    )
    return _trunk(x_prompt, p), _trunk(x_sample, p)
```

```python
import functools

import jax
import jax.numpy as jnp
from jax import lax
from jax.experimental import pallas as pl
from jax.experimental.pallas import tpu as pltpu
from jax.experimental.pallas import tpu_sc as plsc

D_MODEL = 1024
HEAD_DIM = 64
HEADS_PER_GROUP = 4
DILATIONS = (1, 4, 16)
HALF = 64
D_GRP = HEADS_PER_GROUP * HEAD_DIM
D_ATT = 3 * D_GRP
D_CONV = 768
N_EXPERT_GROUPS = 4
EXPERTS_PER_GROUP = 8
N_EXPERTS = 32
D_EXPERT = 512
RMS_EPS = 1e-6
NEG = -1e30
ROPE_THETA = 10000.0

C_Q, C_K, C_V = 0, D_ATT, 2 * D_ATT
C_CU, C_CB, C_CC = 3 * D_ATT, 3 * D_ATT + D_CONV, 3 * D_ATT + 2 * D_CONV
C_ZA = 3 * D_ATT + 3 * D_CONV

TSI = 1024
TS = 1024
TQ = 2048
QB = 128
KW = QB + 2 * HALF
MOE_BLK = 512
SEG_SEQS = 4
SC_ROWS = 32
ROUTER_ROWS = 48
CUM_CHUNK = 256
VMEM_LIMIT = 56 * 1024 * 1024

F32 = jnp.float32
BF16 = jnp.bfloat16


def _rms(xf, g):
    return xf * lax.rsqrt(jnp.mean(xf * xf, axis=-1, keepdims=True) + RMS_EPS) * g


def _sigmoid(x):
    return 1.0 / (1.0 + jnp.exp(-x))


def _pack_bf16_pairs(xb):
    half = xb.shape[1] // 2
    bits = pltpu.bitcast(xb.astype(F32), jnp.uint32)
    return (bits[:, :half] & jnp.uint32(0xFFFF0000)) | (bits[:, half:] >> 16)


def _unpack_bf16_pairs(packed):
    hi = pltpu.bitcast(packed & jnp.uint32(0xFFFF0000), F32)
    lo = pltpu.bitcast(packed << 16, F32)
    return jnp.concatenate([hi, lo], axis=1).astype(BF16)


def _const_spec(shape):
    return pl.BlockSpec(shape, lambda *_: (0,) * len(shape), pipeline_mode=pl.Buffered(1))


def _in_proj_kernel(x_ref, xp_ref, xn_ref, n1_ref, w_ref, cw_ref, cos_ref, sa_ref, sb_ref,
                    q1, k1, v1, q2, k2, v2, q3, k3, v3, mix_ref, pbuf, dbuf):
    j = pl.program_id(1)
    nj = pl.num_programs(1)
    g1 = n1_ref[...]
    hf = _rms(x_ref[0], g1)
    h = hf.astype(BF16)

    def proj(hh, c0, width):
        return jnp.dot(hh, w_ref[:, c0:c0 + width], preferred_element_type=F32)

    cosv = cos_ref[...]
    sav = sa_ref[...]
    sbv = sb_ref[...]

    def rotary(z):
        return z * cosv + pltpu.roll(z, 96, 1) * sav + pltpu.roll(z, 32, 1) * sbv

    def emit(outs, c0, rot, scale):
        for g, d in enumerate(DILATIONS):
            z = proj(h, c0 + g * D_GRP, D_GRP)
            if rot:
                z = jnp.concatenate([rotary(z[:, :128]), rotary(z[:, 128:])], axis=1)
            if scale != 1.0:
                z = z * scale
            if d == 1:
                outs[g][0, 0] = z.astype(BF16)
            else:
                dbuf[0] = z[:, :128]
                dbuf[1] = z[:, 128:]
                for r in range(d):
                    for c in range(2):
                        outs[g][0, r, :, c * 128:(c + 1) * 128] = (
                            dbuf[c, pl.ds(r, TSI // d, stride=d), :].astype(BF16))

    emit((q1, q2, q3), C_Q, True, HEAD_DIM ** -0.5)
    emit((k1, k2, k3), C_K, True, 1.0)
    emit((v1, v2, v3), C_V, False, 1.0)

    hp = _rms(xp_ref[0], g1)
    hn = _rms(xn_ref[0], g1)
    he = jnp.concatenate([hp, hf, hn], axis=0).astype(BF16)
    pe = proj(he, C_CC, D_CONV) * proj(he, C_CU, D_CONV)
    erow = lax.broadcasted_iota(jnp.int32, (TSI + 16, 1), 0)
    outside = ((erow < 8) & (j == 0)) | ((erow >= 8 + TSI) & (j == nj - 1))
    pbuf[...] = jnp.where(outside, 0.0, pe)
    conv = (cw_ref[0:1, :] * pbuf[7:7 + TSI, :] + cw_ref[1:2, :] * pbuf[8:8 + TSI, :]
            + cw_ref[2:3, :] * pbuf[9:9 + TSI, :])
    mix_ref[0] = (proj(h, C_CB, D_CONV) * conv).astype(BF16)


def _in_proj(x, n1, w_qkvc, conv_w, cos_t, sa_t, sb_t):
    B, S, D = x.shape
    nj = S // TSI
    rot_spec = pl.BlockSpec((TSI, 128), lambda b, j: (j, 0))
    in_specs = [
        pl.BlockSpec((1, TSI, D), lambda b, j: (b, j, 0)),
        pl.BlockSpec((1, 8, D), lambda b, j: (b, jnp.maximum(j * (TSI // 8) - 1, 0), 0)),
        pl.BlockSpec((1, 8, D), lambda b, j: (b, jnp.minimum((j + 1) * (TSI // 8), S // 8 - 1), 0)),
        _const_spec((1, D)),
        _const_spec(w_qkvc.shape),
        _const_spec((3, D_CONV)),
        rot_spec, rot_spec, rot_spec,
    ]
    out_shape, out_specs = [], []
    for d in DILATIONS:
        for _ in range(3):
            out_shape.append(jax.ShapeDtypeStruct((B, d, S // d, D_GRP), BF16))
            out_specs.append(pl.BlockSpec((1, d, TSI // d, D_GRP), lambda b, j: (b, 0, j, 0)))
    out_shape.append(jax.ShapeDtypeStruct((B, S, D_CONV), BF16))
    out_specs.append(pl.BlockSpec((1, TSI, D_CONV), lambda b, j: (b, j, 0)))
    return pl.pallas_call(
        _in_proj_kernel,
        grid=(B, nj),
        in_specs=in_specs,
        out_specs=out_specs,
        out_shape=out_shape,
        scratch_shapes=[pltpu.VMEM((TSI + 16, D_CONV), F32), pltpu.VMEM((2, TSI, 128), F32)],
        compiler_params=pltpu.CompilerParams(
            dimension_semantics=("parallel", "parallel"), vmem_limit_bytes=VMEM_LIMIT),
        name="in_proj",
    )(x, x, x, n1, w_qkvc, conv_w, cos_t, sa_t, sb_t)


def _attn_kernel(*refs):
    ins = refs[:21]
    bias_ref = refs[21]
    o_ref = refs[22]
    kbufs = refs[23:29]
    os_ref, ls_ref = refs[29], refs[30]
    j = pl.program_id(1)
    nj = pl.num_programs(1)

    lane = lax.broadcasted_iota(jnp.int32, (QB, D_GRP), 1)
    head_of_lane = lane // HEAD_DIM

    for g, d in enumerate(DILATIONS):
        q_ref, kc, kp, kn, vc, vp, vn = ins[7 * g:7 * g + 7]
        kb, vb = kbufs[2 * g], kbufs[2 * g + 1]
        n = TQ // d
        nblk = n // QB
        for buf, prev, cur, nxt in ((kb, kp, kc, kn), (vb, vp, vc, vn)):
            buf[:, 0:HALF, :] = prev[0]
            buf[:, HALF:HALF + n, :] = cur[0]
            buf[:, HALF + n:, :] = nxt[0]

        def scores(idx, q_ref=q_ref, kb=kb, nblk=nblk):
            r, jb = divmod(idx, nblk)
            qb = q_ref[0, r, jb * QB:(jb + 1) * QB, :]
            kw = kb[r, jb * QB:jb * QB + KW, :]
            qs = jnp.concatenate(
                [jnp.where(head_of_lane == hh, qb, jnp.zeros_like(qb)) for hh in range(HEADS_PER_GROUP)],
                axis=0)
            s = lax.dot_general(qs, kw, (((1,), (1,)), ((), ())), preferred_element_type=F32)
            variant = 0
            if jb == 0:
                variant = jnp.where(j == 0, 1, variant)
            if jb == nblk - 1:
                variant = jnp.where(j == nj - 1, 2, variant)
            return s + bias_ref[variant]

        def softmax(s):
            m = jnp.max(s, axis=1, keepdims=True)
            p = jnp.exp(s - m)
            l = jnp.sum(p, axis=1, keepdims=True)
            return p.astype(BF16), 1.0 / l, m + jnp.log(l)

        def finish(idx, pb, inv_l, lse, vb=vb, d=d, nblk=nblk, g=g):
            r, jb = divmod(idx, nblk)
            pv = jnp.dot(pb, vb[r, jb * QB:jb * QB + KW, :], preferred_element_type=F32)
            o = jnp.zeros((QB, D_GRP), F32)
            ls = jnp.zeros((QB, D_GRP), F32)
            for hh in range(HEADS_PER_GROUP):
                sl = slice(hh * QB, (hh + 1) * QB)
                sel = head_of_lane == hh
                o = jnp.where(sel, pv[sl] * inv_l[sl], o)
                ls = jnp.where(sel, lse[sl], ls)
            for c in range(2):
                cs = slice(c * 128, (c + 1) * 128)
                rows = pl.ds(jb * QB, QB) if d == 1 else pl.ds(r + jb * QB * d, QB, stride=d)
                os_ref[2 * g + c, rows, :] = o[:, cs]
                ls_ref[2 * g + c, rows, :] = ls[:, cs]

        nb = d * nblk
        s_of, sm_of = {}, {}
        for i in range(nb + 2):
            if i >= 2:
                finish(i - 2, *sm_of.pop(i - 2))
            if i < nb:
                s_of[i] = scores(i)
            if 1 <= i <= nb:
                sm_of[i - 1] = softmax(s_of.pop(i - 1))

    def merge(c, carry):
        rows = pl.ds(pl.multiple_of(c * QB, QB), QB)
        for c in range(2):
            l0, l1, l2 = ls_ref[c, rows, :], ls_ref[2 + c, rows, :], ls_ref[4 + c, rows, :]
            mm = jnp.maximum(jnp.maximum(l0, l1), l2)
            w0, w1, w2 = jnp.exp(l0 - mm), jnp.exp(l1 - mm), jnp.exp(l2 - mm)
            o = (w0 * os_ref[c, rows, :] + w1 * os_ref[2 + c, rows, :]
                 + w2 * os_ref[4 + c, rows, :]) / (w0 + w1 + w2)
            o_ref[0, rows, c * 128:(c + 1) * 128] = o.astype(BF16)
        return carry

    lax.fori_loop(0, TQ // QB, merge, 0)


def _band_bias():
    row = jnp.arange(HEADS_PER_GROUP * QB)[:, None] % QB
    col = jnp.arange(KW)[None, :]
    band = (col >= row) & (col <= row + 2 * HALF)
    variants = (band, band & (col >= HALF), band & (col < KW - HALF))
    return jnp.stack([jnp.where(v, 0.0, NEG).astype(F32) for v in variants])


def _attn(qkv):
    B = qkv[0].shape[0]
    S = qkv[0].shape[2]
    nj = S // TQ
    assert nj > 1
    ins, in_specs, scratch = [], [], []
    for g, d in enumerate(DILATIONS):
        q, k, v = qkv[3 * g:3 * g + 3]
        n = TQ // d
        L = S // d
        nh = n // HALF
        cur = pl.BlockSpec((1, d, n, D_GRP), lambda b, j: (b, 0, j, 0))
        prev = pl.BlockSpec((1, d, HALF, D_GRP),
                            lambda b, j, nh=nh: (b, 0, jnp.maximum(j * nh - 1, 0), 0))
        nxt = pl.BlockSpec((1, d, HALF, D_GRP),
                           lambda b, j, nh=nh, L=L: (b, 0, jnp.minimum((j + 1) * nh, L // HALF - 1), 0))
        ins += [q, k, k, k, v, v, v]
        in_specs += [cur, cur, prev, nxt, cur, prev, nxt]
        scratch += [pltpu.VMEM((d, n + 2 * HALF, D_GRP), BF16)] * 2
    scratch += [pltpu.VMEM((6, TQ, 128), F32), pltpu.VMEM((6, TQ, 128), F32)]
    ins.append(_band_bias())
    in_specs.append(_const_spec((3, HEADS_PER_GROUP * QB, KW)))
    return pl.pallas_call(
        _attn_kernel,
        grid=(B, nj),
        in_specs=in_specs,
        out_specs=pl.BlockSpec((1, TQ, D_GRP), lambda b, j: (b, j, 0)),
        out_shape=jax.ShapeDtypeStruct((B, S, D_GRP), BF16),
        scratch_shapes=scratch,
        compiler_params=pltpu.CompilerParams(
            dimension_semantics=("parallel", "parallel"), vmem_limit_bytes=VMEM_LIMIT),
        name="attn",
    )(*ins)


def _post_kernel(o_ref, mix_ref, x_ref, n1_ref, wz_ref, bg_ref, wco_ref, wao_ref, wo_ref, n2_ref,
                 wr_ref, br_ref, tri_ref, x1_ref, h2_ref, rt_ref, cnt_ref, carry):
    first = (pl.program_id(0) == 0) & (pl.program_id(1) == 0)

    @pl.when(first)
    def _():
        carry[...] = jnp.zeros_like(carry)

    x = x_ref[0]
    h = _rms(x, n1_ref[...]).astype(BF16)
    g_a = _sigmoid(jnp.dot(h, wz_ref[:, :D_MODEL], preferred_element_type=F32) + bg_ref[:, :D_MODEL])
    g_b = _sigmoid(jnp.dot(h, wz_ref[:, D_MODEL:], preferred_element_type=F32) + bg_ref[:, D_MODEL:])
    att = jnp.dot(o_ref[0], wao_ref[...], preferred_element_type=F32)
    cvb = jnp.dot(mix_ref[0], wco_ref[...], preferred_element_type=F32)
    merged = (g_a * att + g_b * cvb).astype(BF16)
    x1 = x + jnp.dot(merged, wo_ref[...], preferred_element_type=F32)
    x1_ref[0] = x1
    h2 = _rms(x1, n2_ref[...]).astype(BF16)
    h2_ref[0] = _pack_bf16_pairs(h2)

    lt = lax.dot_general(wr_ref[...], h2, (((1,), (1,)), ((), ())),
                         preferred_element_type=F32) + br_ref[:, 0:1]
    grow = lax.broadcasted_iota(jnp.int32, (8, TS), 0)
    gl = jnp.where(grow < N_EXPERT_GROUPS, lt[0:8], NEG)
    gmax = jnp.max(gl, axis=0, keepdims=True)
    grp = jnp.min(jnp.where(gl == gmax, grow, 8), axis=0, keepdims=True)
    grp_w = 1.0 / jnp.sum(jnp.exp(gl - gmax), axis=0, keepdims=True)
    erow = lax.broadcasted_iota(jnp.int32, (N_EXPERTS, TS), 0)
    el = jnp.where(erow // EXPERTS_PER_GROUP == grp, lt[8:8 + N_EXPERTS], NEG)
    v1 = jnp.max(el, axis=0, keepdims=True)
    i1 = jnp.min(jnp.where(el == v1, erow, N_EXPERTS), axis=0, keepdims=True)
    el2 = jnp.where(erow == i1, NEG, el)
    v2 = jnp.max(el2, axis=0, keepdims=True)
    i2 = jnp.min(jnp.where(el2 == v2, erow, N_EXPERTS), axis=0, keepdims=True)
    t = jnp.exp(v2 - v1)
    den = 1.0 + t
    gate1 = grp_w * (1.0 / den)
    gate2 = grp_w * (t / den)

    oh1 = (erow == i1).astype(F32)
    oh2 = (erow == i2).astype(F32)
    both = oh1 + oh2
    bothb = both.astype(BF16)
    offset = carry[:, 0:1]
    chunks = []
    for c in range(TS // CUM_CHUNK):
        bc = bothb[:, c * CUM_CHUNK:(c + 1) * CUM_CHUNK]
        chunks.append(offset + jnp.dot(bc, tri_ref[...], preferred_element_type=F32))
        offset = offset + jnp.sum(both[:, c * CUM_CHUNK:(c + 1) * CUM_CHUNK], axis=1, keepdims=True)
    basec = jnp.concatenate(chunks, axis=1)
    rank1 = jnp.sum(oh1 * basec, axis=0, keepdims=True)
    rank2 = jnp.sum(oh2 * basec, axis=0, keepdims=True)
    newc = carry[...] + (offset - carry[:, 0:1])
    carry[...] = newc
    cnt_ref[...] = newc
    zero = jnp.zeros((1, TS), F32)
    rt_ref[0] = jnp.concatenate(
        [i1.astype(F32), i2.astype(F32), gate1, gate2, rank1, rank2, zero, zero], axis=0)


def _post(o_att, mix, x, p, b0, nseq):
    _, S, D = x.shape
    nj = S // TS
    src = lambda w: pl.BlockSpec((1, TS, w), lambda b, j: (b + b0, j, 0))
    tile = lambda w: pl.BlockSpec((1, TS, w), lambda b, j: (b, j, 0))
    return pl.pallas_call(
        _post_kernel,
        grid=(nseq, nj),
        in_specs=[src(D_GRP), src(D_CONV), src(D),
                  _const_spec((1, D)), _const_spec((D, 2 * D)), _const_spec((1, 2 * D)),
                  _const_spec((D_CONV, D)), _const_spec((D_GRP, D)), _const_spec((D, D)),
                  _const_spec((1, D)), _const_spec((ROUTER_ROWS, D)), _const_spec((ROUTER_ROWS, 128)),
                  _const_spec((CUM_CHUNK, CUM_CHUNK))],
        out_specs=[tile(D), tile(D // 2),
                   pl.BlockSpec((1, 8, TS), lambda b, j: (b, 0, j)),
                   pl.BlockSpec((N_EXPERTS, 128), lambda b, j: (0, 0))],
        out_shape=[jax.ShapeDtypeStruct((nseq, S, D), F32), jax.ShapeDtypeStruct((nseq, S, D // 2), jnp.uint32),
                   jax.ShapeDtypeStruct((nseq, 8, S), F32),
                   jax.ShapeDtypeStruct((N_EXPERTS, 128), F32)],
        scratch_shapes=[pltpu.VMEM((N_EXPERTS, 128), F32)],
        compiler_params=pltpu.CompilerParams(
            dimension_semantics=("arbitrary", "arbitrary"), vmem_limit_bytes=VMEM_LIMIT),
        name="post",
    )(o_att, mix, x, p["n1"], p["w_gate"], p["b_gate"], p["w_co"], p["w_ao"], p["w_o"], p["n2"],
      p["wr"], p["br"], p["tri"])


def _moe_kernel(be_ref, nu_ref, nv_ref, nx_ref, x_ref, wg_hbm, wu_hbm, wd_hbm, out_ref,
                wgf, wuf, wdf, wgb, wub, wdb, sem, nchg):
    b = pl.program_id(0)

    def weight_copies(e, slot):
        return (pltpu.make_async_copy(wg_hbm.at[e], wgf.at[slot], sem.at[slot]),
                pltpu.make_async_copy(wu_hbm.at[e], wuf.at[slot], sem.at[slot]),
                pltpu.make_async_copy(wd_hbm.at[e], wdf.at[slot], sem.at[slot]))

    @pl.when(b == 0)
    def _():
        nchg[0] = 0
        for cp in weight_copies(be_ref[0], 0):
            cp.start()

    @pl.when(b < nu_ref[0])
    def _():
        @pl.when((b == 0) | (be_ref[b] != be_ref[jnp.maximum(b - 1, 0)]))
        def _():
            slot = nchg[0] % 2
            for cp in weight_copies(be_ref[b], slot):
                cp.wait()
            wgb[...] = wgf[slot].astype(BF16)
            wub[...] = wuf[slot].astype(BF16)
            wdb[...] = wdf[slot].astype(BF16)

            @pl.when(nx_ref[b] != be_ref[b])
            def _():
                for cp in weight_copies(nx_ref[b], 1 - slot):
                    cp.start()

            nchg[0] = nchg[0] + 1

        live = lax.broadcasted_iota(jnp.int32, (MOE_BLK, 1), 0) < nv_ref[b]
        x = _unpack_bf16_pairs(jnp.where(live, x_ref[...], jnp.uint32(0)))
        a = jnp.dot(x, wgb[...], preferred_element_type=F32)
        u = jnp.dot(x, wub[...], preferred_element_type=F32)
        hm = (a * _sigmoid(a) * u).astype(BF16)
        out_ref[...] = jnp.dot(hm, wdb[...], preferred_element_type=F32)

    @pl.when(b >= nu_ref[0])
    def _():
        out_ref[...] = jnp.zeros_like(out_ref)


def _moe(block_e, nused, nvalid, next_e, xs, w_g, w_u, w_d):
    nb = block_e.shape[0]
    D = 2 * xs.shape[1]
    grid_spec = pltpu.PrefetchScalarGridSpec(
        num_scalar_prefetch=4,
        grid=(nb,),
        in_specs=[
            pl.BlockSpec((MOE_BLK, D // 2), lambda b, *_: (b, 0)),
            pl.BlockSpec(memory_space=pl.ANY),
            pl.BlockSpec(memory_space=pl.ANY),
            pl.BlockSpec(memory_space=pl.ANY),
        ],
        out_specs=pl.BlockSpec((MOE_BLK, D), lambda b, *_: (b, 0)),
        scratch_shapes=[pltpu.VMEM((2, D, D_EXPERT), F32), pltpu.VMEM((2, D, D_EXPERT), F32),
                        pltpu.VMEM((2, D_EXPERT, D), F32),
                        pltpu.VMEM((D, D_EXPERT), BF16), pltpu.VMEM((D, D_EXPERT), BF16),
                        pltpu.VMEM((D_EXPERT, D), BF16),
                        pltpu.SemaphoreType.DMA((2,)), pltpu.SMEM((1,), jnp.int32)],
    )
    return pl.pallas_call(
        _moe_kernel,
        grid_spec=grid_spec,
        out_shape=jax.ShapeDtypeStruct((nb * MOE_BLK, D), F32),
        compiler_params=pltpu.CompilerParams(
            dimension_semantics=("arbitrary",), vmem_limit_bytes=VMEM_LIMIT),
        name="moe",
    )(block_e, nused, nvalid, next_e, xs, w_g, w_u, w_d)


def _sc_mesh():
    return plsc.VectorSubcoreMesh(core_axis_name="c", subcore_axis_name="s")


def _sc_worker_range(total):
    info = plsc.get_sparse_core_info()
    nw = info.num_cores * info.num_subcores
    wid = lax.axis_index("s") * info.num_cores + lax.axis_index("c")
    per_w = total // nw
    assert per_w * nw == total and per_w % SC_ROWS == 0
    return wid * per_w, per_w // SC_ROWS


def _sc_scatter_rows(src, idx_a, idx_b, nslots):
    T, D = src.shape

    @functools.partial(
        pl.kernel, out_type=jax.ShapeDtypeStruct((nslots, D), src.dtype), mesh=_sc_mesh(),
        scratch_types=[pltpu.VMEM((SC_ROWS,), jnp.int32), pltpu.VMEM((SC_ROWS,), jnp.int32),
                       pltpu.VMEM((SC_ROWS, D), src.dtype), pltpu.SemaphoreType.DMA],
        name="sc_dispatch")
    def k(x_hbm, ia_hbm, ib_hbm, o_hbm, ia_v, ib_v, rows_v, sem):
        start, nchunks = _sc_worker_range(T)

        @pl.loop(0, nchunks)
        def _(c):
            rows = pl.ds(start + c * SC_ROWS, SC_ROWS)
            pltpu.sync_copy(ia_hbm.at[rows], ia_v)
            pltpu.sync_copy(ib_hbm.at[rows], ib_v)
            pltpu.sync_copy(x_hbm.at[rows], rows_v)
            ca = pltpu.async_copy(rows_v, o_hbm.at[ia_v], sem)
            cb = pltpu.async_copy(rows_v, o_hbm.at[ib_v], sem)
            ca.wait()
            cb.wait()

    return k(src, idx_a, idx_b)


def _sc_gather_rows(src, idx_a, idx_b):
    D = src.shape[1]
    T = idx_a.shape[0]
    out = jax.ShapeDtypeStruct((T, D), src.dtype)

    @functools.partial(
        pl.kernel, out_type=(out, out), mesh=_sc_mesh(),
        scratch_types=[pltpu.VMEM((SC_ROWS,), jnp.int32), pltpu.VMEM((SC_ROWS,), jnp.int32),
                       pltpu.VMEM((SC_ROWS, D), src.dtype), pltpu.VMEM((SC_ROWS, D), src.dtype),
                       pltpu.SemaphoreType.DMA],
        name="sc_combine")
    def k(x_hbm, ia_hbm, ib_hbm, oa_hbm, ob_hbm, ia_v, ib_v, ra_v, rb_v, sem):
        start, nchunks = _sc_worker_range(T)

        @pl.loop(0, nchunks)
        def _(c):
            rows = pl.ds(start + c * SC_ROWS, SC_ROWS)
            pltpu.sync_copy(ia_hbm.at[rows], ia_v)
            pltpu.sync_copy(ib_hbm.at[rows], ib_v)
            ca = pltpu.async_copy(x_hbm.at[ia_v], ra_v, sem)
            cb = pltpu.async_copy(x_hbm.at[ib_v], rb_v, sem)
            ca.wait()
            cb.wait()
            pltpu.sync_copy(ra_v, oa_hbm.at[rows])
            pltpu.sync_copy(rb_v, ob_hbm.at[rows])

    return k(src, idx_a, idx_b)


def _combine_kernel(x1_ref, ya_ref, yb_ref, rt_ref, gf_ref, out_ref):
    gt = jnp.transpose(rt_ref[0])
    y = gt[:, 2:3] * ya_ref[...] + gt[:, 3:4] * yb_ref[...]
    out_ref[...] = _rms(x1_ref[...] + y, gf_ref[...])


def _combine(x1, ya, yb, rt, gf, out, row0, total_rows):
    T, D = x1.shape
    per_seq = rt.shape[2] // TS
    tile = pl.BlockSpec((TS, D), lambda i: (i, 0))

    def body(x1_ref, ya_ref, yb_ref, rt_ref, gf_ref, *rest):
        _combine_kernel(x1_ref, ya_ref, yb_ref, rt_ref, gf_ref, rest[-1])

    in_specs = [tile, tile, tile,
                pl.BlockSpec((1, 8, TS), lambda i: (i // per_seq, 0, i % per_seq)),
                _const_spec((1, D))]
    args = [x1, ya, yb, rt, gf]
    aliases = {}
    if out is not None:
        in_specs.append(pl.BlockSpec(memory_space=pl.ANY))
        args.append(out)
        aliases = {5: 0}
    return pl.pallas_call(
        body,
        grid=(T // TS,),
        in_specs=in_specs,
        out_specs=pl.BlockSpec((TS, D), lambda i: (i + row0 // TS, 0)),
        out_shape=jax.ShapeDtypeStruct((total_rows, D), F32),
        input_output_aliases=aliases,
        compiler_params=pltpu.CompilerParams(
            dimension_semantics=("parallel",), vmem_limit_bytes=VMEM_LIMIT),
        name="combine",
    )(*args)


def _slots_kernel(ps_ref, rt_ref, d_ref):
    v = rt_ref[0]
    start = jnp.zeros(v.shape, F32)
    for e in range(N_EXPERTS):
        start = jnp.where(v == float(e), ps_ref[e].astype(F32), start)
    d_ref[0] = (start[0:2] + v[4:6]).astype(jnp.int32)


def _slots(pstarts, rt):
    B, _, S = rt.shape
    grid_spec = pltpu.PrefetchScalarGridSpec(
        num_scalar_prefetch=1,
        grid=(B,),
        in_specs=[pl.BlockSpec((1, 8, S), lambda b, ps: (b, 0, 0))],
        out_specs=pl.BlockSpec((1, 2, S), lambda b, ps: (b, 0, 0)),
    )
    return pl.pallas_call(
        _slots_kernel,
        grid_spec=grid_spec,
        out_shape=jax.ShapeDtypeStruct((B, 2, S), jnp.int32),
        compiler_params=pltpu.CompilerParams(dimension_semantics=("parallel",)),
        name="slots",
    )(pstarts, rt)


def _rotary_tables(S):
    inv_freq = 1.0 / (ROPE_THETA ** (jnp.arange(0, HEAD_DIM, 2, dtype=F32) / HEAD_DIM))
    ang = jnp.arange(S, dtype=F32)[:, None] * inv_freq[None, :]
    cos, sin = jnp.cos(ang), jnp.sin(ang)
    z = jnp.zeros_like(sin)
    cos_t = jnp.concatenate([cos, cos, cos, cos], axis=1)
    sa_t = jnp.concatenate([-sin, z, -sin, z], axis=1)
    sb_t = jnp.concatenate([z, sin, z, sin], axis=1)
    return cos_t, sa_t, sb_t


def _moe_segment(o_att, mix, x, p, b0, nseq, out):
    _, S, D = x.shape
    T = nseq * S
    x1, h2, rt, cnt = _post(o_att, mix, x, p, b0, nseq)

    counts = cnt[:, 0].astype(jnp.int32)
    pcounts = (counts + MOE_BLK - 1) // MOE_BLK * MOE_BLK
    pends = jnp.cumsum(pcounts)
    pstarts = pends - pcounts
    dest = _slots(pstarts, rt)
    d1 = dest[:, 0, :].reshape(T)
    d2 = dest[:, 1, :].reshape(T)
    nb = (2 * T) // MOE_BLK + N_EXPERTS
    block_start = jnp.arange(nb, dtype=jnp.int32) * MOE_BLK
    block_e = jnp.minimum(jnp.sum((pends[None, :] <= block_start[:, None]).astype(jnp.int32), axis=1),
                          N_EXPERTS - 1)
    nused = (pends[-1:] // MOE_BLK).astype(jnp.int32)
    mine = block_e[:, None] == jnp.arange(N_EXPERTS, dtype=jnp.int32)[None, :]
    valid_end = jnp.sum(jnp.where(mine, (pstarts + counts)[None, :], 0), axis=1)
    nvalid = jnp.clip(valid_end - block_start, 0, MOE_BLK)

    xs = _sc_scatter_rows(h2.reshape(T, D // 2), d1, d2, nb * MOE_BLK)
    experts = jnp.arange(N_EXPERTS, dtype=jnp.int32)
    later = (experts[None, :] > block_e[:, None]) & (counts[None, :] > 0)
    next_e = jnp.min(jnp.where(later, experts[None, :], N_EXPERTS), axis=1)
    next_e = jnp.where(next_e == N_EXPERTS, block_e, next_e)
    yb = _moe(block_e, nused, nvalid, next_e, xs, p["w_g"], p["w_u"], p["w_d"])
    ya, yc = _sc_gather_rows(yb, d1, d2)
    return _combine(x1.reshape(T, D), ya, yc, rt, p["gf"], out, b0 * S, x.shape[0] * S)


def _trunk(x, p):
    B, S, D = x.shape
    outs = _in_proj(x, p["n1"], p["w_qkvc"], p["conv_w"], *p["rot"])
    o_att = _attn(outs[:9])
    nseg = max(1, B // SEG_SEQS)
    out = None
    for seg in range(nseg):
        out = _moe_segment(o_att, outs[9], x, p, seg * (B // nseg), B // nseg, out)
    return out.reshape(B, S, D)


def kernel(x_prompt, x_sample, norm1_g, w_in, b_gate, conv_w, w_attn_out, w_conv_out, w_out, norm2_g,
           w_router_group, b_router_group, w_router_expert, b_router_expert, w_exp_gate, w_exp_up,
           w_exp_down, norm_f_g):
    assert norm1_g.shape[0] == 1, "single-layer trunk"
    S = x_prompt.shape[1]
    wr = jnp.zeros((ROUTER_ROWS, D_MODEL), F32)
    wr = wr.at[0:N_EXPERT_GROUPS].set(w_router_group[0].T).at[8:8 + N_EXPERTS].set(w_router_expert[0].T)
    br = jnp.zeros((ROUTER_ROWS,), F32)
    br = br.at[0:N_EXPERT_GROUPS].set(b_router_group[0]).at[8:8 + N_EXPERTS].set(b_router_expert[0])
    ti = jnp.arange(CUM_CHUNK)
    p = dict(
        n1=norm1_g, w_qkvc=w_in[0, :, :C_ZA].astype(BF16), w_gate=w_in[0, :, C_ZA:].astype(BF16),
        b_gate=b_gate, conv_w=conv_w[0], w_co=w_conv_out[0].astype(BF16), rot=_rotary_tables(S),
        w_ao=w_attn_out[0].astype(BF16), w_o=w_out[0].astype(BF16), n2=norm2_g,
        wr=wr.astype(BF16), br=jnp.broadcast_to(br[:, None], (ROUTER_ROWS, 128)),
        tri=(ti[:, None] < ti[None, :]).astype(BF16),
        w_g=w_exp_gate[0], w_u=w_exp_up[0], w_d=w_exp_down[0],
        gf=norm_f_g.reshape(1, D_MODEL),
    )
    return _trunk(x_prompt, p), _trunk(x_sample, p)
```

```python
import functools

import jax
import jax.numpy as jnp
from jax import lax
from jax.experimental import pallas as pl
from jax.experimental.pallas import tpu as pltpu
from jax.experimental.pallas import tpu_sc as plsc

D_MODEL = 1024
HEAD_DIM = 64
HEADS_PER_GROUP = 4
DILATIONS = (1, 4, 16)
HALF = 64
D_GRP = HEADS_PER_GROUP * HEAD_DIM
D_ATT = 3 * D_GRP
D_CONV = 768
N_EXPERT_GROUPS = 4
EXPERTS_PER_GROUP = 8
N_EXPERTS = 32
D_EXPERT = 512
RMS_EPS = 1e-6
NEG = -1e30
ROPE_THETA = 10000.0

C_Q, C_K, C_V = 0, D_ATT, 2 * D_ATT
C_CU, C_CB, C_CC = 3 * D_ATT, 3 * D_ATT + D_CONV, 3 * D_ATT + 2 * D_CONV
C_ZA = 3 * D_ATT + 3 * D_CONV

TSI = 1024
TS = 1024
TQ = 2048
QB = 128
KW = QB + 2 * HALF
MOE_BLK = 512
SEG_SEQS = 4
SC_ROWS = 32
ROUTER_ROWS = 48
CUM_CHUNK = 256
VMEM_LIMIT = 56 * 1024 * 1024

F32 = jnp.float32
BF16 = jnp.bfloat16


def _rms(xf, g):
    return xf * lax.rsqrt(jnp.mean(xf * xf, axis=-1, keepdims=True) + RMS_EPS) * g


def _sigmoid(x):
    return 1.0 / (1.0 + jnp.exp(-x))


def _pack_bf16_pairs(xb):
    half = xb.shape[1] // 2
    bits = pltpu.bitcast(xb.astype(F32), jnp.uint32)
    return (bits[:, :half] & jnp.uint32(0xFFFF0000)) | (bits[:, half:] >> 16)


def _unpack_bf16_pairs(packed, dtype):
    hi = pltpu.bitcast(packed & jnp.uint32(0xFFFF0000), F32)
    lo = pltpu.bitcast(packed << 16, F32)
    return jnp.concatenate([hi, lo], axis=1).astype(dtype)


def _const_spec(shape):
    return pl.BlockSpec(shape, lambda *_: (0,) * len(shape), pipeline_mode=pl.Buffered(1))


def _in_proj_kernel(x_ref, xp_ref, xn_ref, n1_ref, w_ref, cw_ref, cos_ref, sa_ref, sb_ref,
                    q1, k1, v1, q2, k2, v2, q3, k3, v3, mix_ref, pbuf, dbuf):
    j = pl.program_id(1)
    nj = pl.num_programs(1)
    g1 = n1_ref[...]
    hf = _rms(x_ref[0], g1)
    h = hf.astype(BF16)

    def proj(hh, c0, width):
        return jnp.dot(hh, w_ref[:, c0:c0 + width], preferred_element_type=F32)

    cosv = cos_ref[...]
    sav = sa_ref[...]
    sbv = sb_ref[...]

    def rotary(z):
        return z * cosv + pltpu.roll(z, 96, 1) * sav + pltpu.roll(z, 32, 1) * sbv

    def emit(outs, c0, rot, scale):
        for g, d in enumerate(DILATIONS):
            z = proj(h, c0 + g * D_GRP, D_GRP)
            if rot:
                z = jnp.concatenate([rotary(z[:, :128]), rotary(z[:, 128:])], axis=1)
            if scale != 1.0:
                z = z * scale
            if d == 1:
                outs[g][0, 0] = z.astype(BF16)
            else:
                dbuf[0] = z[:, :128]
                dbuf[1] = z[:, 128:]
                for r in range(d):
                    for c in range(2):
                        outs[g][0, r, :, c * 128:(c + 1) * 128] = (
                            dbuf[c, pl.ds(r, TSI // d, stride=d), :].astype(BF16))

    emit((q1, q2, q3), C_Q, True, HEAD_DIM ** -0.5)
    emit((k1, k2, k3), C_K, True, 1.0)
    emit((v1, v2, v3), C_V, False, 1.0)

    hp = _rms(xp_ref[0], g1)
    hn = _rms(xn_ref[0], g1)
    he = jnp.concatenate([hp, hf, hn], axis=0).astype(BF16)
    pe = proj(he, C_CC, D_CONV) * proj(he, C_CU, D_CONV)
    erow = lax.broadcasted_iota(jnp.int32, (TSI + 16, 1), 0)
    outside = ((erow < 8) & (j == 0)) | ((erow >= 8 + TSI) & (j == nj - 1))
    pbuf[...] = jnp.where(outside, 0.0, pe)
    conv = (cw_ref[0:1, :] * pbuf[7:7 + TSI, :] + cw_ref[1:2, :] * pbuf[8:8 + TSI, :]
            + cw_ref[2:3, :] * pbuf[9:9 + TSI, :])
    mix_ref[0] = (proj(h, C_CB, D_CONV) * conv).astype(BF16)


def _in_proj(x, n1, w_qkvc, conv_w, cos_t, sa_t, sb_t):
    B, S, D = x.shape
    nj = S // TSI
    rot_spec = pl.BlockSpec((TSI, 128), lambda b, j: (j, 0))
    in_specs = [
        pl.BlockSpec((1, TSI, D), lambda b, j: (b, j, 0)),
        pl.BlockSpec((1, 8, D), lambda b, j: (b, jnp.maximum(j * (TSI // 8) - 1, 0), 0)),
        pl.BlockSpec((1, 8, D), lambda b, j: (b, jnp.minimum((j + 1) * (TSI // 8), S // 8 - 1), 0)),
        _const_spec((1, D)),
        _const_spec(w_qkvc.shape),
        _const_spec((3, D_CONV)),
        rot_spec, rot_spec, rot_spec,
    ]
    out_shape, out_specs = [], []
    for d in DILATIONS:
        for _ in range(3):
            out_shape.append(jax.ShapeDtypeStruct((B, d, S // d, D_GRP), BF16))
            out_specs.append(pl.BlockSpec((1, d, TSI // d, D_GRP), lambda b, j: (b, 0, j, 0)))
    out_shape.append(jax.ShapeDtypeStruct((B, S, D_CONV), BF16))
    out_specs.append(pl.BlockSpec((1, TSI, D_CONV), lambda b, j: (b, j, 0)))
    return pl.pallas_call(
        _in_proj_kernel,
        grid=(B, nj),
        in_specs=in_specs,
        out_specs=out_specs,
        out_shape=out_shape,
        scratch_shapes=[pltpu.VMEM((TSI + 16, D_CONV), F32), pltpu.VMEM((2, TSI, 128), F32)],
        compiler_params=pltpu.CompilerParams(
            dimension_semantics=("parallel", "parallel"), vmem_limit_bytes=VMEM_LIMIT),
        name="in_proj",
    )(x, x, x, n1, w_qkvc, conv_w, cos_t, sa_t, sb_t)


def _attn_kernel(*refs):
    ins = refs[:21]
    bias_ref = refs[21]
    o_ref = refs[22]
    kbufs = refs[23:29]
    os_ref, ls_ref = refs[29], refs[30]
    j = pl.program_id(1)
    nj = pl.num_programs(1)

    lane = lax.broadcasted_iota(jnp.int32, (QB, D_GRP), 1)
    head_of_lane = lane // HEAD_DIM

    for g, d in enumerate(DILATIONS):
        q_ref, kc, kp, kn, vc, vp, vn = ins[7 * g:7 * g + 7]
        kb, vb = kbufs[2 * g], kbufs[2 * g + 1]
        n = TQ // d
        nblk = n // QB
        for buf, prev, cur, nxt in ((kb, kp, kc, kn), (vb, vp, vc, vn)):
            buf[:, 0:HALF, :] = prev[0]
            buf[:, HALF:HALF + n, :] = cur[0]
            buf[:, HALF + n:, :] = nxt[0]

        def scores(idx, q_ref=q_ref, kb=kb, nblk=nblk):
            r, jb = divmod(idx, nblk)
            qb = q_ref[0, r, jb * QB:(jb + 1) * QB, :]
            kw = kb[r, jb * QB:jb * QB + KW, :]
            qs = jnp.concatenate(
                [jnp.where(head_of_lane == hh, qb, jnp.zeros_like(qb)) for hh in range(HEADS_PER_GROUP)],
                axis=0)
            s = lax.dot_general(qs, kw, (((1,), (1,)), ((), ())), preferred_element_type=F32)
            variant = 0
            if jb == 0:
                variant = jnp.where(j == 0, 1, variant)
            if jb == nblk - 1:
                variant = jnp.where(j == nj - 1, 2, variant)
            return s + bias_ref[variant]

        def softmax(s):
            m = jnp.max(s, axis=1, keepdims=True)
            p = jnp.exp(s - m)
            l = jnp.sum(p, axis=1, keepdims=True)
            return p.astype(BF16), 1.0 / l, m + jnp.log(l)

        def finish(idx, pb, inv_l, lse, vb=vb, d=d, nblk=nblk, g=g):
            r, jb = divmod(idx, nblk)
            pv = jnp.dot(pb, vb[r, jb * QB:jb * QB + KW, :], preferred_element_type=F32)
            o = jnp.zeros((QB, D_GRP), F32)
            ls = jnp.zeros((QB, D_GRP), F32)
            for hh in range(HEADS_PER_GROUP):
                sl = slice(hh * QB, (hh + 1) * QB)
                sel = head_of_lane == hh
                o = jnp.where(sel, pv[sl] * inv_l[sl], o)
                ls = jnp.where(sel, lse[sl], ls)
            for c in range(2):
                cs = slice(c * 128, (c + 1) * 128)
                rows = pl.ds(jb * QB, QB) if d == 1 else pl.ds(r + jb * QB * d, QB, stride=d)
                os_ref[2 * g + c, rows, :] = o[:, cs]
                ls_ref[2 * g + c, rows, :] = ls[:, cs]

        nb = d * nblk
        s_of, sm_of = {}, {}
        for i in range(nb + 2):
            if i >= 2:
                finish(i - 2, *sm_of.pop(i - 2))
            if i < nb:
                s_of[i] = scores(i)
            if 1 <= i <= nb:
                sm_of[i - 1] = softmax(s_of.pop(i - 1))

    def merge(c, carry):
        rows = pl.ds(pl.multiple_of(c * QB, QB), QB)
        for c in range(2):
            l0, l1, l2 = ls_ref[c, rows, :], ls_ref[2 + c, rows, :], ls_ref[4 + c, rows, :]
            mm = jnp.maximum(jnp.maximum(l0, l1), l2)
            w0, w1, w2 = jnp.exp(l0 - mm), jnp.exp(l1 - mm), jnp.exp(l2 - mm)
            o = (w0 * os_ref[c, rows, :] + w1 * os_ref[2 + c, rows, :]
                 + w2 * os_ref[4 + c, rows, :]) / (w0 + w1 + w2)
            o_ref[0, rows, c * 128:(c + 1) * 128] = o.astype(BF16)
        return carry

    lax.fori_loop(0, TQ // QB, merge, 0)


def _band_bias():
    row = jnp.arange(HEADS_PER_GROUP * QB)[:, None] % QB
    col = jnp.arange(KW)[None, :]
    band = (col >= row) & (col <= row + 2 * HALF)
    variants = (band, band & (col >= HALF), band & (col < KW - HALF))
    return jnp.stack([jnp.where(v, 0.0, NEG).astype(F32) for v in variants])


def _attn(qkv):
    B = qkv[0].shape[0]
    S = qkv[0].shape[2]
    nj = S // TQ
    assert nj > 1
    ins, in_specs, scratch = [], [], []
    for g, d in enumerate(DILATIONS):
        q, k, v = qkv[3 * g:3 * g + 3]
        n = TQ // d
        L = S // d
        nh = n // HALF
        cur = pl.BlockSpec((1, d, n, D_GRP), lambda b, j: (b, 0, j, 0))
        prev = pl.BlockSpec((1, d, HALF, D_GRP),
                            lambda b, j, nh=nh: (b, 0, jnp.maximum(j * nh - 1, 0), 0))
        nxt = pl.BlockSpec((1, d, HALF, D_GRP),
                           lambda b, j, nh=nh, L=L: (b, 0, jnp.minimum((j + 1) * nh, L // HALF - 1), 0))
        ins += [q, k, k, k, v, v, v]
        in_specs += [cur, cur, prev, nxt, cur, prev, nxt]
        scratch += [pltpu.VMEM((d, n + 2 * HALF, D_GRP), BF16)] * 2
    scratch += [pltpu.VMEM((6, TQ, 128), F32), pltpu.VMEM((6, TQ, 128), F32)]
    ins.append(_band_bias())
    in_specs.append(_const_spec((3, HEADS_PER_GROUP * QB, KW)))
    return pl.pallas_call(
        _attn_kernel,
        grid=(B, nj),
        in_specs=in_specs,
        out_specs=pl.BlockSpec((1, TQ, D_GRP), lambda b, j: (b, j, 0)),
        out_shape=jax.ShapeDtypeStruct((B, S, D_GRP), BF16),
        scratch_shapes=scratch,
        compiler_params=pltpu.CompilerParams(
            dimension_semantics=("parallel", "parallel"), vmem_limit_bytes=VMEM_LIMIT),
        name="attn",
    )(*ins)


def _post_kernel(o_ref, mix_ref, x_ref, n1_ref, wz_ref, bg_ref, wco_ref, wao_ref, wo_ref, n2_ref,
                 wr_ref, br_ref, tri_ref, x1_ref, h2_ref, rt_ref, cnt_ref, carry):
    first = (pl.program_id(0) == 0) & (pl.program_id(1) == 0)

    @pl.when(first)
    def _():
        carry[...] = jnp.zeros_like(carry)

    x = x_ref[0]
    h = _rms(x, n1_ref[...]).astype(BF16)
    g_a = _sigmoid(jnp.dot(h, wz_ref[:, :D_MODEL], preferred_element_type=F32) + bg_ref[:, :D_MODEL])
    g_b = _sigmoid(jnp.dot(h, wz_ref[:, D_MODEL:], preferred_element_type=F32) + bg_ref[:, D_MODEL:])
    att = jnp.dot(o_ref[0], wao_ref[...], preferred_element_type=F32)
    cvb = jnp.dot(mix_ref[0], wco_ref[...], preferred_element_type=F32)
    merged = (g_a * att + g_b * cvb).astype(BF16)
    x1 = x + jnp.dot(merged, wo_ref[...], preferred_element_type=F32)
    x1_ref[0] = x1
    h2 = _rms(x1, n2_ref[...]).astype(BF16)
    h2_ref[0] = _pack_bf16_pairs(h2)

    lt = lax.dot_general(wr_ref[...], h2, (((1,), (1,)), ((), ())),
                         preferred_element_type=F32) + br_ref[:, 0:1]
    grow = lax.broadcasted_iota(jnp.int32, (8, TS), 0)
    gl = jnp.where(grow < N_EXPERT_GROUPS, lt[0:8], NEG)
    gmax = jnp.max(gl, axis=0, keepdims=True)
    grp = jnp.min(jnp.where(gl == gmax, grow, 8), axis=0, keepdims=True)
    grp_w = 1.0 / jnp.sum(jnp.exp(gl - gmax), axis=0, keepdims=True)
    erow = lax.broadcasted_iota(jnp.int32, (N_EXPERTS, TS), 0)
    el = jnp.where(erow // EXPERTS_PER_GROUP == grp, lt[8:8 + N_EXPERTS], NEG)
    v1 = jnp.max(el, axis=0, keepdims=True)
    i1 = jnp.min(jnp.where(el == v1, erow, N_EXPERTS), axis=0, keepdims=True)
    el2 = jnp.where(erow == i1, NEG, el)
    v2 = jnp.max(el2, axis=0, keepdims=True)
    i2 = jnp.min(jnp.where(el2 == v2, erow, N_EXPERTS), axis=0, keepdims=True)
    t = jnp.exp(v2 - v1)
    den = 1.0 + t
    gate1 = grp_w * (1.0 / den)
    gate2 = grp_w * (t / den)

    oh1 = (erow == i1).astype(F32)
    oh2 = (erow == i2).astype(F32)
    both = oh1 + oh2
    bothb = both.astype(BF16)
    offset = carry[:, 0:1]
    chunks = []
    for c in range(TS // CUM_CHUNK):
        bc = bothb[:, c * CUM_CHUNK:(c + 1) * CUM_CHUNK]
        chunks.append(offset + jnp.dot(bc, tri_ref[...], preferred_element_type=F32))
        offset = offset + jnp.sum(both[:, c * CUM_CHUNK:(c + 1) * CUM_CHUNK], axis=1, keepdims=True)
    basec = jnp.concatenate(chunks, axis=1)
    rank1 = jnp.sum(oh1 * basec, axis=0, keepdims=True)
    rank2 = jnp.sum(oh2 * basec, axis=0, keepdims=True)
    newc = carry[...] + (offset - carry[:, 0:1])
    carry[...] = newc
    cnt_ref[...] = newc
    zero = jnp.zeros((1, TS), F32)
    rt_ref[0] = jnp.concatenate(
        [i1.astype(F32), i2.astype(F32), gate1, gate2, rank1, rank2, zero, zero], axis=0)


def _post(o_att, mix, x, p, b0, nseq):
    _, S, D = x.shape
    nj = S // TS
    src = lambda w: pl.BlockSpec((1, TS, w), lambda b, j: (b + b0, j, 0))
    tile = lambda w: pl.BlockSpec((1, TS, w), lambda b, j: (b, j, 0))
    return pl.pallas_call(
        _post_kernel,
        grid=(nseq, nj),
        in_specs=[src(D_GRP), src(D_CONV), src(D),
                  _const_spec((1, D)), _const_spec((D, 2 * D)), _const_spec((1, 2 * D)),
                  _const_spec((D_CONV, D)), _const_spec((D_GRP, D)), _const_spec((D, D)),
                  _const_spec((1, D)), _const_spec((ROUTER_ROWS, D)), _const_spec((ROUTER_ROWS, 128)),
                  _const_spec((CUM_CHUNK, CUM_CHUNK))],
        out_specs=[tile(D), tile(D // 2),
                   pl.BlockSpec((1, 8, TS), lambda b, j: (b, 0, j)),
                   pl.BlockSpec((N_EXPERTS, 128), lambda b, j: (0, 0))],
        out_shape=[jax.ShapeDtypeStruct((nseq, S, D), F32), jax.ShapeDtypeStruct((nseq, S, D // 2), jnp.uint32),
                   jax.ShapeDtypeStruct((nseq, 8, S), F32),
                   jax.ShapeDtypeStruct((N_EXPERTS, 128), F32)],
        scratch_shapes=[pltpu.VMEM((N_EXPERTS, 128), F32)],
        compiler_params=pltpu.CompilerParams(
            dimension_semantics=("arbitrary", "arbitrary"), vmem_limit_bytes=VMEM_LIMIT),
        name="post",
    )(o_att, mix, x, p["n1"], p["w_gate"], p["b_gate"], p["w_co"], p["w_ao"], p["w_o"], p["n2"],
      p["wr"], p["br"], p["tri"])


def _moe_kernel(be_ref, nu_ref, nv_ref, nx_ref, x_ref, wg_hbm, wu_hbm, wd_hbm, out_ref,
                wgf, wuf, wdf, wgb, wub, wdb, sem, nchg):
    b = pl.program_id(0)

    def weight_copies(e, slot):
        return (pltpu.make_async_copy(wg_hbm.at[e], wgf.at[slot], sem.at[slot]),
                pltpu.make_async_copy(wu_hbm.at[e], wuf.at[slot], sem.at[slot]),
                pltpu.make_async_copy(wd_hbm.at[e], wdf.at[slot], sem.at[slot]))

    @pl.when(b == 0)
    def _():
        nchg[0] = 0
        for cp in weight_copies(be_ref[0], 0):
            cp.start()

    @pl.when(b < nu_ref[0])
    def _():
        @pl.when((b == 0) | (be_ref[b] != be_ref[jnp.maximum(b - 1, 0)]))
        def _():
            slot = nchg[0] % 2
            for cp in weight_copies(be_ref[b], slot):
                cp.wait()
            wgb[...] = wgf[slot].astype(BF16)
            wub[...] = wuf[slot].astype(BF16)
            wdb[...] = wdf[slot].astype(BF16)

            @pl.when(nx_ref[b] != be_ref[b])
            def _():
                for cp in weight_copies(nx_ref[b], 1 - slot):
                    cp.start()

            nchg[0] = nchg[0] + 1

        live = lax.broadcasted_iota(jnp.int32, (MOE_BLK, 1), 0) < nv_ref[b]
        x = _unpack_bf16_pairs(jnp.where(live, x_ref[...], jnp.uint32(0)), BF16)
        a = jnp.dot(x, wgb[...], preferred_element_type=F32)
        u = jnp.dot(x, wub[...], preferred_element_type=F32)
        hm = (a * _sigmoid(a) * u).astype(BF16)
        out_ref[...] = _pack_bf16_pairs(jnp.dot(hm, wdb[...], preferred_element_type=F32).astype(BF16))

    @pl.when(b >= nu_ref[0])
    def _():
        out_ref[...] = jnp.zeros_like(out_ref)


def _moe(block_e, nused, nvalid, next_e, xs, w_g, w_u, w_d):
    nb = block_e.shape[0]
    D = 2 * xs.shape[1]
    grid_spec = pltpu.PrefetchScalarGridSpec(
        num_scalar_prefetch=4,
        grid=(nb,),
        in_specs=[
            pl.BlockSpec((MOE_BLK, D // 2), lambda b, *_: (b, 0)),
            pl.BlockSpec(memory_space=pl.ANY),
            pl.BlockSpec(memory_space=pl.ANY),
            pl.BlockSpec(memory_space=pl.ANY),
        ],
        out_specs=pl.BlockSpec((MOE_BLK, D // 2), lambda b, *_: (b, 0)),
        scratch_shapes=[pltpu.VMEM((2, D, D_EXPERT), F32), pltpu.VMEM((2, D, D_EXPERT), F32),
                        pltpu.VMEM((2, D_EXPERT, D), F32),
                        pltpu.VMEM((D, D_EXPERT), BF16), pltpu.VMEM((D, D_EXPERT), BF16),
                        pltpu.VMEM((D_EXPERT, D), BF16),
                        pltpu.SemaphoreType.DMA((2,)), pltpu.SMEM((1,), jnp.int32)],
    )
    return pl.pallas_call(
        _moe_kernel,
        grid_spec=grid_spec,
        out_shape=jax.ShapeDtypeStruct((nb * MOE_BLK, D // 2), jnp.uint32),
        compiler_params=pltpu.CompilerParams(
            dimension_semantics=("arbitrary",), vmem_limit_bytes=VMEM_LIMIT),
        name="moe",
    )(block_e, nused, nvalid, next_e, xs, w_g, w_u, w_d)


def _sc_mesh():
    return plsc.VectorSubcoreMesh(core_axis_name="c", subcore_axis_name="s")


def _sc_worker_range(total):
    info = plsc.get_sparse_core_info()
    nw = info.num_cores * info.num_subcores
    wid = lax.axis_index("s") * info.num_cores + lax.axis_index("c")
    per_w = total // nw
    assert per_w * nw == total and per_w % SC_ROWS == 0
    return wid * per_w, per_w // SC_ROWS


def _sc_scatter_rows(src, idx_a, idx_b, nslots):
    T, D = src.shape

    @functools.partial(
        pl.kernel, out_type=jax.ShapeDtypeStruct((nslots, D), src.dtype), mesh=_sc_mesh(),
        scratch_types=[pltpu.VMEM((SC_ROWS,), jnp.int32), pltpu.VMEM((SC_ROWS,), jnp.int32),
                       pltpu.VMEM((SC_ROWS, D), src.dtype), pltpu.SemaphoreType.DMA],
        name="sc_dispatch")
    def k(x_hbm, ia_hbm, ib_hbm, o_hbm, ia_v, ib_v, rows_v, sem):
        start, nchunks = _sc_worker_range(T)

        @pl.loop(0, nchunks)
        def _(c):
            rows = pl.ds(start + c * SC_ROWS, SC_ROWS)
            pltpu.sync_copy(ia_hbm.at[rows], ia_v)
            pltpu.sync_copy(ib_hbm.at[rows], ib_v)
            pltpu.sync_copy(x_hbm.at[rows], rows_v)
            ca = pltpu.async_copy(rows_v, o_hbm.at[ia_v], sem)
            cb = pltpu.async_copy(rows_v, o_hbm.at[ib_v], sem)
            ca.wait()
            cb.wait()

    return k(src, idx_a, idx_b)


def _sc_gather_rows(src, idx_a, idx_b):
    D = src.shape[1]
    T = idx_a.shape[0]
    out = jax.ShapeDtypeStruct((T, D), src.dtype)

    @functools.partial(
        pl.kernel, out_type=(out, out), mesh=_sc_mesh(),
        scratch_types=[pltpu.VMEM((SC_ROWS,), jnp.int32), pltpu.VMEM((SC_ROWS,), jnp.int32),
                       pltpu.VMEM((SC_ROWS, D), src.dtype), pltpu.VMEM((SC_ROWS, D), src.dtype),
                       pltpu.SemaphoreType.DMA],
        name="sc_combine")
    def k(x_hbm, ia_hbm, ib_hbm, oa_hbm, ob_hbm, ia_v, ib_v, ra_v, rb_v, sem):
        start, nchunks = _sc_worker_range(T)

        @pl.loop(0, nchunks)
        def _(c):
            rows = pl.ds(start + c * SC_ROWS, SC_ROWS)
            pltpu.sync_copy(ia_hbm.at[rows], ia_v)
            pltpu.sync_copy(ib_hbm.at[rows], ib_v)
            ca = pltpu.async_copy(x_hbm.at[ia_v], ra_v, sem)
            cb = pltpu.async_copy(x_hbm.at[ib_v], rb_v, sem)
            ca.wait()
            cb.wait()
            pltpu.sync_copy(ra_v, oa_hbm.at[rows])
            pltpu.sync_copy(rb_v, ob_hbm.at[rows])

    return k(src, idx_a, idx_b)


def _combine_kernel(x1_ref, ya_ref, yb_ref, rt_ref, gf_ref, out_ref):
    gt = jnp.transpose(rt_ref[0])
    y = gt[:, 2:3] * _unpack_bf16_pairs(ya_ref[...], F32) + gt[:, 3:4] * _unpack_bf16_pairs(yb_ref[...], F32)
    out_ref[...] = _rms(x1_ref[...] + y, gf_ref[...])


def _combine(x1, ya, yb, rt, gf, out, row0, total_rows):
    T, D = x1.shape
    per_seq = rt.shape[2] // TS
    tile = pl.BlockSpec((TS, D), lambda i: (i, 0))

    def body(x1_ref, ya_ref, yb_ref, rt_ref, gf_ref, *rest):
        _combine_kernel(x1_ref, ya_ref, yb_ref, rt_ref, gf_ref, rest[-1])

    half = pl.BlockSpec((TS, D // 2), lambda i: (i, 0))
    in_specs = [tile, half, half,
                pl.BlockSpec((1, 8, TS), lambda i: (i // per_seq, 0, i % per_seq)),
                _const_spec((1, D))]
    args = [x1, ya, yb, rt, gf]
    aliases = {}
    if out is not None:
        in_specs.append(pl.BlockSpec(memory_space=pl.ANY))
        args.append(out)
        aliases = {5: 0}
    return pl.pallas_call(
        body,
        grid=(T // TS,),
        in_specs=in_specs,
        out_specs=pl.BlockSpec((TS, D), lambda i: (i + row0 // TS, 0)),
        out_shape=jax.ShapeDtypeStruct((total_rows, D), F32),
        input_output_aliases=aliases,
        compiler_params=pltpu.CompilerParams(
            dimension_semantics=("parallel",), vmem_limit_bytes=VMEM_LIMIT),
        name="combine",
    )(*args)


def _slots_kernel(ps_ref, rt_ref, d_ref):
    v = rt_ref[0]
    start = jnp.zeros(v.shape, F32)
    for e in range(N_EXPERTS):
        start = jnp.where(v == float(e), ps_ref[e].astype(F32), start)
    d_ref[0] = (start[0:2] + v[4:6]).astype(jnp.int32)


def _slots(pstarts, rt):
    B, _, S = rt.shape
    grid_spec = pltpu.PrefetchScalarGridSpec(
        num_scalar_prefetch=1,
        grid=(B,),
        in_specs=[pl.BlockSpec((1, 8, S), lambda b, ps: (b, 0, 0))],
        out_specs=pl.BlockSpec((1, 2, S), lambda b, ps: (b, 0, 0)),
    )
    return pl.pallas_call(
        _slots_kernel,
        grid_spec=grid_spec,
        out_shape=jax.ShapeDtypeStruct((B, 2, S), jnp.int32),
        compiler_params=pltpu.CompilerParams(dimension_semantics=("parallel",)),
        name="slots",
    )(pstarts, rt)


def _rotary_tables(S):
    inv_freq = 1.0 / (ROPE_THETA ** (jnp.arange(0, HEAD_DIM, 2, dtype=F32) / HEAD_DIM))
    ang = jnp.arange(S, dtype=F32)[:, None] * inv_freq[None, :]
    cos, sin = jnp.cos(ang), jnp.sin(ang)
    z = jnp.zeros_like(sin)
    cos_t = jnp.concatenate([cos, cos, cos, cos], axis=1)
    sa_t = jnp.concatenate([-sin, z, -sin, z], axis=1)
    sb_t = jnp.concatenate([z, sin, z, sin], axis=1)
    return cos_t, sa_t, sb_t


def _moe_segment(o_att, mix, x, p, b0, nseq, out):
    _, S, D = x.shape
    T = nseq * S
    x1, h2, rt, cnt = _post(o_att, mix, x, p, b0, nseq)

    counts = cnt[:, 0].astype(jnp.int32)
    pcounts = (counts + MOE_BLK - 1) // MOE_BLK * MOE_BLK
    pends = jnp.cumsum(pcounts)
    pstarts = pends - pcounts
    dest = _slots(pstarts, rt)
    d1 = dest[:, 0, :].reshape(T)
    d2 = dest[:, 1, :].reshape(T)
    nb = (2 * T) // MOE_BLK + N_EXPERTS
    block_start = jnp.arange(nb, dtype=jnp.int32) * MOE_BLK
    block_e = jnp.minimum(jnp.sum((pends[None, :] <= block_start[:, None]).astype(jnp.int32), axis=1),
                          N_EXPERTS - 1)
    nused = (pends[-1:] // MOE_BLK).astype(jnp.int32)
    mine = block_e[:, None] == jnp.arange(N_EXPERTS, dtype=jnp.int32)[None, :]
    valid_end = jnp.sum(jnp.where(mine, (pstarts + counts)[None, :], 0), axis=1)
    nvalid = jnp.clip(valid_end - block_start, 0, MOE_BLK)

    xs = _sc_scatter_rows(h2.reshape(T, D // 2), d1, d2, nb * MOE_BLK)
    experts = jnp.arange(N_EXPERTS, dtype=jnp.int32)
    later = (experts[None, :] > block_e[:, None]) & (counts[None, :] > 0)
    next_e = jnp.min(jnp.where(later, experts[None, :], N_EXPERTS), axis=1)
    next_e = jnp.where(next_e == N_EXPERTS, block_e, next_e)
    yb = _moe(block_e, nused, nvalid, next_e, xs, p["w_g"], p["w_u"], p["w_d"])
    ya, yc = _sc_gather_rows(yb, d1, d2)
    return _combine(x1.reshape(T, D), ya, yc, rt, p["gf"], out, b0 * S, x.shape[0] * S)


def _trunk(x, p):
    B, S, D = x.shape
    outs = _in_proj(x, p["n1"], p["w_qkvc"], p["conv_w"], *p["rot"])
    o_att = _attn(outs[:9])
    nseg = max(1, B // SEG_SEQS)
    out = None
    for seg in range(nseg):
        out = _moe_segment(o_att, outs[9], x, p, seg * (B // nseg), B // nseg, out)
    return out.reshape(B, S, D)


def kernel(x_prompt, x_sample, norm1_g, w_in, b_gate, conv_w, w_attn_out, w_conv_out, w_out, norm2_g,
           w_router_group, b_router_group, w_router_expert, b_router_expert, w_exp_gate, w_exp_up,
           w_exp_down, norm_f_g):
    assert norm1_g.shape[0] == 1, "single-layer trunk"
    S = x_prompt.shape[1]
    wr = jnp.zeros((ROUTER_ROWS, D_MODEL), F32)
    wr = wr.at[0:N_EXPERT_GROUPS].set(w_router_group[0].T).at[8:8 + N_EXPERTS].set(w_router_expert[0].T)
    br = jnp.zeros((ROUTER_ROWS,), F32)
    br = br.at[0:N_EXPERT_GROUPS].set(b_router_group[0]).at[8:8 + N_EXPERTS].set(b_router_expert[0])
    ti = jnp.arange(CUM_CHUNK)
    p = dict(
        n1=norm1_g, w_qkvc=w_in[0, :, :C_ZA].astype(BF16), w_gate=w_in[0, :, C_ZA:].astype(BF16),
        b_gate=b_gate, conv_w=conv_w[0], w_co=w_conv_out[0].astype(BF16), rot=_rotary_tables(S),
        w_ao=w_attn_out[0].astype(BF16), w_o=w_out[0].astype(BF16), n2=norm2_g,
        wr=wr.astype(BF16), br=jnp.broadcast_to(br[:, None], (ROUTER_ROWS, 128)),
        tri=(ti[:, None] < ti[None, :]).astype(BF16),
        w_g=w_exp_gate[0], w_u=w_exp_up[0], w_d=w_exp_down[0],
        gf=norm_f_g.reshape(1, D_MODEL),
    )
    return _trunk(x_prompt, p), _trunk(x_sample, p)
```

```python
import functools

import jax
import jax.numpy as jnp
from jax import lax
from jax.experimental import pallas as pl
from jax.experimental.pallas import tpu as pltpu
from jax.experimental.pallas import tpu_sc as plsc

D_MODEL = 1024
HEAD_DIM = 64
HEADS_PER_GROUP = 4
DILATIONS = (1, 4, 16)
HALF = 64
D_GRP = HEADS_PER_GROUP * HEAD_DIM
D_ATT = 3 * D_GRP
D_CONV = 768
N_EXPERT_GROUPS = 4
EXPERTS_PER_GROUP = 8
N_EXPERTS = 32
D_EXPERT = 512
RMS_EPS = 1e-6
NEG = -1e30
ROPE_THETA = 10000.0

C_Q, C_K, C_V = 0, D_ATT, 2 * D_ATT
C_CU, C_CB, C_CC = 3 * D_ATT, 3 * D_ATT + D_CONV, 3 * D_ATT + 2 * D_CONV
C_ZA = 3 * D_ATT + 3 * D_CONV

TSI = 1024
TS = 1024
TQ = 2048
QB = 128
KW = QB + 2 * HALF
MOE_BLK = 512
SEG_SEQS = 4
SC_ROWS = 64
ROUTER_ROWS = 48
CUM_CHUNK = 256
VMEM_LIMIT = 56 * 1024 * 1024

F32 = jnp.float32
BF16 = jnp.bfloat16


def _rms(xf, g):
    return xf * lax.rsqrt(jnp.mean(xf * xf, axis=-1, keepdims=True) + RMS_EPS) * g


def _sigmoid(x):
    return 1.0 / (1.0 + jnp.exp(-x))


def _pack_bf16_pairs(xb):
    half = xb.shape[1] // 2
    bits = pltpu.bitcast(xb.astype(F32), jnp.uint32)
    return (bits[:, :half] & jnp.uint32(0xFFFF0000)) | (bits[:, half:] >> 16)


def _unpack_bf16_pairs(packed, dtype):
    hi = pltpu.bitcast(packed & jnp.uint32(0xFFFF0000), F32)
    lo = pltpu.bitcast(packed << 16, F32)
    return jnp.concatenate([hi, lo], axis=1).astype(dtype)


def _const_spec(shape):
    return pl.BlockSpec(shape, lambda *_: (0,) * len(shape), pipeline_mode=pl.Buffered(1))


def _in_proj_kernel(x_ref, xp_ref, xn_ref, n1_ref, w_ref, cw_ref, cos_ref, sa_ref, sb_ref,
                    q1, k1, v1, q2, k2, v2, q3, k3, v3, mix_ref, pbuf, dbuf):
    j = pl.program_id(1)
    nj = pl.num_programs(1)
    g1 = n1_ref[...]
    hf = _rms(x_ref[0], g1)
    h = hf.astype(BF16)

    def proj(hh, c0, width):
        return jnp.dot(hh, w_ref[:, c0:c0 + width], preferred_element_type=F32)

    cosv = cos_ref[...]
    sav = sa_ref[...]
    sbv = sb_ref[...]

    def rotary(z):
        return z * cosv + pltpu.roll(z, 96, 1) * sav + pltpu.roll(z, 32, 1) * sbv

    def emit(outs, c0, rot, scale):
        for g, d in enumerate(DILATIONS):
            z = proj(h, c0 + g * D_GRP, D_GRP)
            if rot:
                z = jnp.concatenate([rotary(z[:, :128]), rotary(z[:, 128:])], axis=1)
            if scale != 1.0:
                z = z * scale
            if d == 1:
                outs[g][0, 0] = z.astype(BF16)
            else:
                dbuf[0] = z[:, :128]
                dbuf[1] = z[:, 128:]
                for r in range(d):
                    for c in range(2):
                        outs[g][0, r, :, c * 128:(c + 1) * 128] = (
                            dbuf[c, pl.ds(r, TSI // d, stride=d), :].astype(BF16))

    emit((q1, q2, q3), C_Q, True, HEAD_DIM ** -0.5)
    emit((k1, k2, k3), C_K, True, 1.0)
    emit((v1, v2, v3), C_V, False, 1.0)

    hp = _rms(xp_ref[0], g1)
    hn = _rms(xn_ref[0], g1)
    he = jnp.concatenate([hp, hf, hn], axis=0).astype(BF16)
    pe = proj(he, C_CC, D_CONV) * proj(he, C_CU, D_CONV)
    erow = lax.broadcasted_iota(jnp.int32, (TSI + 16, 1), 0)
    outside = ((erow < 8) & (j == 0)) | ((erow >= 8 + TSI) & (j == nj - 1))
    pbuf[...] = jnp.where(outside, 0.0, pe)
    conv = (cw_ref[0:1, :] * pbuf[7:7 + TSI, :] + cw_ref[1:2, :] * pbuf[8:8 + TSI, :]
            + cw_ref[2:3, :] * pbuf[9:9 + TSI, :])
    mix_ref[0] = (proj(h, C_CB, D_CONV) * conv).astype(BF16)


def _in_proj(x, n1, w_qkvc, conv_w, cos_t, sa_t, sb_t):
    B, S, D = x.shape
    nj = S // TSI
    rot_spec = pl.BlockSpec((TSI, 128), lambda b, j: (j, 0))
    in_specs = [
        pl.BlockSpec((1, TSI, D), lambda b, j: (b, j, 0)),
        pl.BlockSpec((1, 8, D), lambda b, j: (b, jnp.maximum(j * (TSI // 8) - 1, 0), 0)),
        pl.BlockSpec((1, 8, D), lambda b, j: (b, jnp.minimum((j + 1) * (TSI // 8), S // 8 - 1), 0)),
        _const_spec((1, D)),
        _const_spec(w_qkvc.shape),
        _const_spec((3, D_CONV)),
        rot_spec, rot_spec, rot_spec,
    ]
    out_shape, out_specs = [], []
    for d in DILATIONS:
        for _ in range(3):
            out_shape.append(jax.ShapeDtypeStruct((B, d, S // d, D_GRP), BF16))
            out_specs.append(pl.BlockSpec((1, d, TSI // d, D_GRP), lambda b, j: (b, 0, j, 0)))
    out_shape.append(jax.ShapeDtypeStruct((B, S, D_CONV), BF16))
    out_specs.append(pl.BlockSpec((1, TSI, D_CONV), lambda b, j: (b, j, 0)))
    return pl.pallas_call(
        _in_proj_kernel,
        grid=(B, nj),
        in_specs=in_specs,
        out_specs=out_specs,
        out_shape=out_shape,
        scratch_shapes=[pltpu.VMEM((TSI + 16, D_CONV), F32), pltpu.VMEM((2, TSI, 128), F32)],
        compiler_params=pltpu.CompilerParams(
            dimension_semantics=("parallel", "parallel"), vmem_limit_bytes=VMEM_LIMIT),
        name="in_proj",
    )(x, x, x, n1, w_qkvc, conv_w, cos_t, sa_t, sb_t)


def _attn_kernel(*refs):
    ins = refs[:21]
    bias_ref = refs[21]
    o_ref = refs[22]
    kbufs = refs[23:29]
    os_ref, ls_ref = refs[29], refs[30]
    j = pl.program_id(1)
    nj = pl.num_programs(1)

    lane = lax.broadcasted_iota(jnp.int32, (QB, D_GRP), 1)
    head_of_lane = lane // HEAD_DIM

    for g, d in enumerate(DILATIONS):
        q_ref, kc, kp, kn, vc, vp, vn = ins[7 * g:7 * g + 7]
        kb, vb = kbufs[2 * g], kbufs[2 * g + 1]
        n = TQ // d
        nblk = n // QB
        for buf, prev, cur, nxt in ((kb, kp, kc, kn), (vb, vp, vc, vn)):
            buf[:, 0:HALF, :] = prev[0]
            buf[:, HALF:HALF + n, :] = cur[0]
            buf[:, HALF + n:, :] = nxt[0]

        def scores(idx, q_ref=q_ref, kb=kb, nblk=nblk):
            r, jb = divmod(idx, nblk)
            qb = q_ref[0, r, jb * QB:(jb + 1) * QB, :]
            kw = kb[r, jb * QB:jb * QB + KW, :]
            qs = jnp.concatenate(
                [jnp.where(head_of_lane == hh, qb, jnp.zeros_like(qb)) for hh in range(HEADS_PER_GROUP)],
                axis=0)
            s = lax.dot_general(qs, kw, (((1,), (1,)), ((), ())), preferred_element_type=F32)
            variant = 0
            if jb == 0:
                variant = jnp.where(j == 0, 1, variant)
            if jb == nblk - 1:
                variant = jnp.where(j == nj - 1, 2, variant)
            return s + bias_ref[variant]

        def softmax(s):
            m = jnp.max(s, axis=1, keepdims=True)
            p = jnp.exp(s - m)
            l = jnp.sum(p, axis=1, keepdims=True)
            return p.astype(BF16), 1.0 / l, m + jnp.log(l)

        def finish(idx, pb, inv_l, lse, vb=vb, d=d, nblk=nblk, g=g):
            r, jb = divmod(idx, nblk)
            pv = jnp.dot(pb, vb[r, jb * QB:jb * QB + KW, :], preferred_element_type=F32)
            o = jnp.zeros((QB, D_GRP), F32)
            ls = jnp.zeros((QB, D_GRP), F32)
            for hh in range(HEADS_PER_GROUP):
                sl = slice(hh * QB, (hh + 1) * QB)
                sel = head_of_lane == hh
                o = jnp.where(sel, pv[sl] * inv_l[sl], o)
                ls = jnp.where(sel, lse[sl], ls)
            for c in range(2):
                cs = slice(c * 128, (c + 1) * 128)
                rows = pl.ds(jb * QB, QB) if d == 1 else pl.ds(r + jb * QB * d, QB, stride=d)
                os_ref[2 * g + c, rows, :] = o[:, cs]
                ls_ref[2 * g + c, rows, :] = ls[:, cs]

        nb = d * nblk
        s_of, sm_of = {}, {}
        for i in range(nb + 2):
            if i >= 2:
                finish(i - 2, *sm_of.pop(i - 2))
            if i < nb:
                s_of[i] = scores(i)
            if 1 <= i <= nb:
                sm_of[i - 1] = softmax(s_of.pop(i - 1))

    def merge(c, carry):
        rows = pl.ds(pl.multiple_of(c * QB, QB), QB)
        for c in range(2):
            l0, l1, l2 = ls_ref[c, rows, :], ls_ref[2 + c, rows, :], ls_ref[4 + c, rows, :]
            mm = jnp.maximum(jnp.maximum(l0, l1), l2)
            w0, w1, w2 = jnp.exp(l0 - mm), jnp.exp(l1 - mm), jnp.exp(l2 - mm)
            o = (w0 * os_ref[c, rows, :] + w1 * os_ref[2 + c, rows, :]
                 + w2 * os_ref[4 + c, rows, :]) / (w0 + w1 + w2)
            o_ref[0, rows, c * 128:(c + 1) * 128] = o.astype(BF16)
        return carry

    lax.fori_loop(0, TQ // QB, merge, 0)


def _band_bias():
    row = jnp.arange(HEADS_PER_GROUP * QB)[:, None] % QB
    col = jnp.arange(KW)[None, :]
    band = (col >= row) & (col <= row + 2 * HALF)
    variants = (band, band & (col >= HALF), band & (col < KW - HALF))
    return jnp.stack([jnp.where(v, 0.0, NEG).astype(F32) for v in variants])


def _attn(qkv):
    B = qkv[0].shape[0]
    S = qkv[0].shape[2]
    nj = S // TQ
    assert nj > 1
    ins, in_specs, scratch = [], [], []
    for g, d in enumerate(DILATIONS):
        q, k, v = qkv[3 * g:3 * g + 3]
        n = TQ // d
        L = S // d
        nh = n // HALF
        cur = pl.BlockSpec((1, d, n, D_GRP), lambda b, j: (b, 0, j, 0))
        prev = pl.BlockSpec((1, d, HALF, D_GRP),
                            lambda b, j, nh=nh: (b, 0, jnp.maximum(j * nh - 1, 0), 0))
        nxt = pl.BlockSpec((1, d, HALF, D_GRP),
                           lambda b, j, nh=nh, L=L: (b, 0, jnp.minimum((j + 1) * nh, L // HALF - 1), 0))
        ins += [q, k, k, k, v, v, v]
        in_specs += [cur, cur, prev, nxt, cur, prev, nxt]
        scratch += [pltpu.VMEM((d, n + 2 * HALF, D_GRP), BF16)] * 2
    scratch += [pltpu.VMEM((6, TQ, 128), F32), pltpu.VMEM((6, TQ, 128), F32)]
    ins.append(_band_bias())
    in_specs.append(_const_spec((3, HEADS_PER_GROUP * QB, KW)))
    return pl.pallas_call(
        _attn_kernel,
        grid=(B, nj),
        in_specs=in_specs,
        out_specs=pl.BlockSpec((1, TQ, D_GRP), lambda b, j: (b, j, 0)),
        out_shape=jax.ShapeDtypeStruct((B, S, D_GRP), BF16),
        scratch_shapes=scratch,
        compiler_params=pltpu.CompilerParams(
            dimension_semantics=("parallel", "parallel"), vmem_limit_bytes=VMEM_LIMIT),
        name="attn",
    )(*ins)


def _post_kernel(o_ref, mix_ref, x_ref, n1_ref, wz_ref, bg_ref, wco_ref, wao_ref, wo_ref, n2_ref,
                 wr_ref, br_ref, tri_ref, x1_ref, h2_ref, rt_ref, cnt_ref, carry):
    first = (pl.program_id(0) == 0) & (pl.program_id(1) == 0)

    @pl.when(first)
    def _():
        carry[...] = jnp.zeros_like(carry)

    x = x_ref[0]
    h = _rms(x, n1_ref[...]).astype(BF16)
    g_a = _sigmoid(jnp.dot(h, wz_ref[:, :D_MODEL], preferred_element_type=F32) + bg_ref[:, :D_MODEL])
    g_b = _sigmoid(jnp.dot(h, wz_ref[:, D_MODEL:], preferred_element_type=F32) + bg_ref[:, D_MODEL:])
    att = jnp.dot(o_ref[0], wao_ref[...], preferred_element_type=F32)
    cvb = jnp.dot(mix_ref[0], wco_ref[...], preferred_element_type=F32)
    merged = (g_a * att + g_b * cvb).astype(BF16)
    x1 = x + jnp.dot(merged, wo_ref[...], preferred_element_type=F32)
    x1_ref[0] = x1
    h2 = _rms(x1, n2_ref[...]).astype(BF16)
    h2_ref[0] = _pack_bf16_pairs(h2)

    lt = lax.dot_general(wr_ref[...], h2, (((1,), (1,)), ((), ())),
                         preferred_element_type=F32) + br_ref[:, 0:1]
    grow = lax.broadcasted_iota(jnp.int32, (8, TS), 0)
    gl = jnp.where(grow < N_EXPERT_GROUPS, lt[0:8], NEG)
    gmax = jnp.max(gl, axis=0, keepdims=True)
    grp = jnp.min(jnp.where(gl == gmax, grow, 8), axis=0, keepdims=True)
    grp_w = 1.0 / jnp.sum(jnp.exp(gl - gmax), axis=0, keepdims=True)
    erow = lax.broadcasted_iota(jnp.int32, (N_EXPERTS, TS), 0)
    el = jnp.where(erow // EXPERTS_PER_GROUP == grp, lt[8:8 + N_EXPERTS], NEG)
    v1 = jnp.max(el, axis=0, keepdims=True)
    i1 = jnp.min(jnp.where(el == v1, erow, N_EXPERTS), axis=0, keepdims=True)
    el2 = jnp.where(erow == i1, NEG, el)
    v2 = jnp.max(el2, axis=0, keepdims=True)
    i2 = jnp.min(jnp.where(el2 == v2, erow, N_EXPERTS), axis=0, keepdims=True)
    t = jnp.exp(v2 - v1)
    den = 1.0 + t
    gate1 = grp_w * (1.0 / den)
    gate2 = grp_w * (t / den)

    oh1 = (erow == i1).astype(F32)
    oh2 = (erow == i2).astype(F32)
    both = oh1 + oh2
    bothb = both.astype(BF16)
    offset = carry[:, 0:1]
    chunks = []
    for c in range(TS // CUM_CHUNK):
        bc = bothb[:, c * CUM_CHUNK:(c + 1) * CUM_CHUNK]
        chunks.append(offset + jnp.dot(bc, tri_ref[...], preferred_element_type=F32))
        offset = offset + jnp.sum(both[:, c * CUM_CHUNK:(c + 1) * CUM_CHUNK], axis=1, keepdims=True)
    basec = jnp.concatenate(chunks, axis=1)
    rank1 = jnp.sum(oh1 * basec, axis=0, keepdims=True)
    rank2 = jnp.sum(oh2 * basec, axis=0, keepdims=True)
    newc = carry[...] + (offset - carry[:, 0:1])
    carry[...] = newc
    cnt_ref[...] = newc
    zero = jnp.zeros((1, TS), F32)
    rt_ref[0] = jnp.concatenate(
        [i1.astype(F32), i2.astype(F32), gate1, gate2, rank1, rank2, zero, zero], axis=0)


def _post(o_att, mix, x, p, b0, nseq):
    _, S, D = x.shape
    nj = S // TS
    src = lambda w: pl.BlockSpec((1, TS, w), lambda b, j: (b + b0, j, 0))
    tile = lambda w: pl.BlockSpec((1, TS, w), lambda b, j: (b, j, 0))
    return pl.pallas_call(
        _post_kernel,
        grid=(nseq, nj),
        in_specs=[src(D_GRP), src(D_CONV), src(D),
                  _const_spec((1, D)), _const_spec((D, 2 * D)), _const_spec((1, 2 * D)),
                  _const_spec((D_CONV, D)), _const_spec((D_GRP, D)), _const_spec((D, D)),
                  _const_spec((1, D)), _const_spec((ROUTER_ROWS, D)), _const_spec((ROUTER_ROWS, 128)),
                  _const_spec((CUM_CHUNK, CUM_CHUNK))],
        out_specs=[tile(D), tile(D // 2),
                   pl.BlockSpec((1, 8, TS), lambda b, j: (b, 0, j)),
                   pl.BlockSpec((N_EXPERTS, 128), lambda b, j: (0, 0))],
        out_shape=[jax.ShapeDtypeStruct((nseq, S, D), F32), jax.ShapeDtypeStruct((nseq, S, D // 2), jnp.uint32),
                   jax.ShapeDtypeStruct((nseq, 8, S), F32),
                   jax.ShapeDtypeStruct((N_EXPERTS, 128), F32)],
        scratch_shapes=[pltpu.VMEM((N_EXPERTS, 128), F32)],
        compiler_params=pltpu.CompilerParams(
            dimension_semantics=("arbitrary", "arbitrary"), vmem_limit_bytes=VMEM_LIMIT),
        name="post",
    )(o_att, mix, x, p["n1"], p["w_gate"], p["b_gate"], p["w_co"], p["w_ao"], p["w_o"], p["n2"],
      p["wr"], p["br"], p["tri"])


def _moe_kernel(be_ref, nu_ref, nv_ref, nx_ref, x_ref, wg_hbm, wu_hbm, wd_hbm, out_ref,
                wgf, wuf, wdf, wgb, wub, wdb, sem, nchg):
    b = pl.program_id(0)

    def weight_copies(e, slot):
        return (pltpu.make_async_copy(wg_hbm.at[e], wgf.at[slot], sem.at[slot]),
                pltpu.make_async_copy(wu_hbm.at[e], wuf.at[slot], sem.at[slot]),
                pltpu.make_async_copy(wd_hbm.at[e], wdf.at[slot], sem.at[slot]))

    @pl.when(b == 0)
    def _():
        nchg[0] = 0
        for cp in weight_copies(be_ref[0], 0):
            cp.start()

    @pl.when(b < nu_ref[0])
    def _():
        @pl.when((b == 0) | (be_ref[b] != be_ref[jnp.maximum(b - 1, 0)]))
        def _():
            slot = nchg[0] % 2
            for cp in weight_copies(be_ref[b], slot):
                cp.wait()
            wgb[...] = wgf[slot].astype(BF16)
            wub[...] = wuf[slot].astype(BF16)
            wdb[...] = wdf[slot].astype(BF16)

            @pl.when(nx_ref[b] != be_ref[b])
            def _():
                for cp in weight_copies(nx_ref[b], 1 - slot):
                    cp.start()

            nchg[0] = nchg[0] + 1

        def experts_rows(nrows):
            live = lax.broadcasted_iota(jnp.int32, (nrows, 1), 0) < nv_ref[b]
            x = _unpack_bf16_pairs(jnp.where(live, x_ref[0:nrows, :], jnp.uint32(0)), BF16)
            a = jnp.dot(x, wgb[...], preferred_element_type=F32)
            u = jnp.dot(x, wub[...], preferred_element_type=F32)
            hm = (a * _sigmoid(a) * u).astype(BF16)
            y = jnp.dot(hm, wdb[...], preferred_element_type=F32)
            out_ref[0:nrows, :] = _pack_bf16_pairs(y.astype(BF16))

        @pl.when(nv_ref[b] > MOE_BLK // 2)
        def _():
            experts_rows(MOE_BLK)

        @pl.when(nv_ref[b] <= MOE_BLK // 2)
        def _():
            experts_rows(MOE_BLK // 2)
            out_ref[MOE_BLK // 2:, :] = jnp.zeros((MOE_BLK // 2, out_ref.shape[1]), out_ref.dtype)

    @pl.when(b >= nu_ref[0])
    def _():
        out_ref[...] = jnp.zeros_like(out_ref)


def _moe(block_e, nused, nvalid, next_e, xs, w_g, w_u, w_d):
    nb = block_e.shape[0]
    D = 2 * xs.shape[1]
    grid_spec = pltpu.PrefetchScalarGridSpec(
        num_scalar_prefetch=4,
        grid=(nb,),
        in_specs=[
            pl.BlockSpec((MOE_BLK, D // 2), lambda b, *_: (b, 0)),
            pl.BlockSpec(memory_space=pl.ANY),
            pl.BlockSpec(memory_space=pl.ANY),
            pl.BlockSpec(memory_space=pl.ANY),
        ],
        out_specs=pl.BlockSpec((MOE_BLK, D // 2), lambda b, *_: (b, 0)),
        scratch_shapes=[pltpu.VMEM((2, D, D_EXPERT), F32), pltpu.VMEM((2, D, D_EXPERT), F32),
                        pltpu.VMEM((2, D_EXPERT, D), F32),
                        pltpu.VMEM((D, D_EXPERT), BF16), pltpu.VMEM((D, D_EXPERT), BF16),
                        pltpu.VMEM((D_EXPERT, D), BF16),
                        pltpu.SemaphoreType.DMA((2,)), pltpu.SMEM((1,), jnp.int32)],
    )
    return pl.pallas_call(
        _moe_kernel,
        grid_spec=grid_spec,
        out_shape=jax.ShapeDtypeStruct((nb * MOE_BLK, D // 2), jnp.uint32),
        compiler_params=pltpu.CompilerParams(
            dimension_semantics=("arbitrary",), vmem_limit_bytes=VMEM_LIMIT),
        name="moe",
    )(block_e, nused, nvalid, next_e, xs, w_g, w_u, w_d)


def _sc_mesh():
    return plsc.VectorSubcoreMesh(core_axis_name="c", subcore_axis_name="s")


def _sc_worker_range(total):
    info = plsc.get_sparse_core_info()
    nw = info.num_cores * info.num_subcores
    wid = lax.axis_index("s") * info.num_cores + lax.axis_index("c")
    per_w = total // nw
    assert per_w * nw == total and per_w % SC_ROWS == 0
    return wid * per_w, per_w // SC_ROWS


def _sc_scatter_rows(src, idx_a, idx_b, nslots):
    T, D = src.shape

    @functools.partial(
        pl.kernel, out_type=jax.ShapeDtypeStruct((nslots, D), src.dtype), mesh=_sc_mesh(),
        scratch_types=[pltpu.VMEM((SC_ROWS,), jnp.int32), pltpu.VMEM((SC_ROWS,), jnp.int32),
                       pltpu.VMEM((SC_ROWS, D), src.dtype), pltpu.SemaphoreType.DMA],
        name="sc_dispatch")
    def k(x_hbm, ia_hbm, ib_hbm, o_hbm, ia_v, ib_v, rows_v, sem):
        start, nchunks = _sc_worker_range(T)

        @pl.loop(0, nchunks)
        def _(c):
            rows = pl.ds(start + c * SC_ROWS, SC_ROWS)
            pltpu.sync_copy(ia_hbm.at[rows], ia_v)
            pltpu.sync_copy(ib_hbm.at[rows], ib_v)
            pltpu.sync_copy(x_hbm.at[rows], rows_v)
            ca = pltpu.async_copy(rows_v, o_hbm.at[ia_v], sem)
            cb = pltpu.async_copy(rows_v, o_hbm.at[ib_v], sem)
            ca.wait()
            cb.wait()

    return k(src, idx_a, idx_b)


def _sc_gather_rows(src, idx_a, idx_b):
    D = src.shape[1]
    T = idx_a.shape[0]
    out = jax.ShapeDtypeStruct((T, D), src.dtype)

    @functools.partial(
        pl.kernel, out_type=(out, out), mesh=_sc_mesh(),
        scratch_types=[pltpu.VMEM((SC_ROWS,), jnp.int32), pltpu.VMEM((SC_ROWS,), jnp.int32),
                       pltpu.VMEM((SC_ROWS, D), src.dtype), pltpu.VMEM((SC_ROWS, D), src.dtype),
                       pltpu.SemaphoreType.DMA],
        name="sc_combine")
    def k(x_hbm, ia_hbm, ib_hbm, oa_hbm, ob_hbm, ia_v, ib_v, ra_v, rb_v, sem):
        start, nchunks = _sc_worker_range(T)

        @pl.loop(0, nchunks)
        def _(c):
            rows = pl.ds(start + c * SC_ROWS, SC_ROWS)
            pltpu.sync_copy(ia_hbm.at[rows], ia_v)
            pltpu.sync_copy(ib_hbm.at[rows], ib_v)
            ca = pltpu.async_copy(x_hbm.at[ia_v], ra_v, sem)
            cb = pltpu.async_copy(x_hbm.at[ib_v], rb_v, sem)
            ca.wait()
            cb.wait()
            pltpu.sync_copy(ra_v, oa_hbm.at[rows])
            pltpu.sync_copy(rb_v, ob_hbm.at[rows])

    return k(src, idx_a, idx_b)


def _combine_kernel(x1_ref, ya_ref, yb_ref, rt_ref, gf_ref, out_ref):
    gt = jnp.transpose(rt_ref[0])
    y = gt[:, 2:3] * _unpack_bf16_pairs(ya_ref[...], F32) + gt[:, 3:4] * _unpack_bf16_pairs(yb_ref[...], F32)
    out_ref[...] = _rms(x1_ref[...] + y, gf_ref[...])


def _combine(x1, ya, yb, rt, gf, out, row0, total_rows):
    T, D = x1.shape
    per_seq = rt.shape[2] // TS
    tile = pl.BlockSpec((TS, D), lambda i: (i, 0))

    def body(x1_ref, ya_ref, yb_ref, rt_ref, gf_ref, *rest):
        _combine_kernel(x1_ref, ya_ref, yb_ref, rt_ref, gf_ref, rest[-1])

    half = pl.BlockSpec((TS, D // 2), lambda i: (i, 0))
    in_specs = [tile, half, half,
                pl.BlockSpec((1, 8, TS), lambda i: (i // per_seq, 0, i % per_seq)),
                _const_spec((1, D))]
    args = [x1, ya, yb, rt, gf]
    aliases = {}
    if out is not None:
        in_specs.append(pl.BlockSpec(memory_space=pl.ANY))
        args.append(out)
        aliases = {5: 0}
    return pl.pallas_call(
        body,
        grid=(T // TS,),
        in_specs=in_specs,
        out_specs=pl.BlockSpec((TS, D), lambda i: (i + row0 // TS, 0)),
        out_shape=jax.ShapeDtypeStruct((total_rows, D), F32),
        input_output_aliases=aliases,
        compiler_params=pltpu.CompilerParams(
            dimension_semantics=("parallel",), vmem_limit_bytes=VMEM_LIMIT),
        name="combine",
    )(*args)


def _slots_kernel(ps_ref, rt_ref, d_ref):
    v = rt_ref[0]
    start = jnp.zeros(v.shape, F32)
    for e in range(N_EXPERTS):
        start = jnp.where(v == float(e), ps_ref[e].astype(F32), start)
    d_ref[0] = (start[0:2] + v[4:6]).astype(jnp.int32)


def _slots(pstarts, rt):
    B, _, S = rt.shape
    grid_spec = pltpu.PrefetchScalarGridSpec(
        num_scalar_prefetch=1,
        grid=(B,),
        in_specs=[pl.BlockSpec((1, 8, S), lambda b, ps: (b, 0, 0))],
        out_specs=pl.BlockSpec((1, 2, S), lambda b, ps: (b, 0, 0)),
    )
    return pl.pallas_call(
        _slots_kernel,
        grid_spec=grid_spec,
        out_shape=jax.ShapeDtypeStruct((B, 2, S), jnp.int32),
        compiler_params=pltpu.CompilerParams(dimension_semantics=("parallel",)),
        name="slots",
    )(pstarts, rt)


def _rotary_tables(S):
    inv_freq = 1.0 / (ROPE_THETA ** (jnp.arange(0, HEAD_DIM, 2, dtype=F32) / HEAD_DIM))
    ang = jnp.arange(S, dtype=F32)[:, None] * inv_freq[None, :]
    cos, sin = jnp.cos(ang), jnp.sin(ang)
    z = jnp.zeros_like(sin)
    cos_t = jnp.concatenate([cos, cos, cos, cos], axis=1)
    sa_t = jnp.concatenate([-sin, z, -sin, z], axis=1)
    sb_t = jnp.concatenate([z, sin, z, sin], axis=1)
    return cos_t, sa_t, sb_t


def _moe_segment(o_att, mix, x, p, b0, nseq, out):
    _, S, D = x.shape
    T = nseq * S
    x1, h2, rt, cnt = _post(o_att, mix, x, p, b0, nseq)

    counts = cnt[:, 0].astype(jnp.int32)
    pcounts = (counts + MOE_BLK - 1) // MOE_BLK * MOE_BLK
    pends = jnp.cumsum(pcounts)
    pstarts = pends - pcounts
    dest = _slots(pstarts, rt)
    d1 = dest[:, 0, :].reshape(T)
    d2 = dest[:, 1, :].reshape(T)
    nb = (2 * T) // MOE_BLK + N_EXPERTS
    block_start = jnp.arange(nb, dtype=jnp.int32) * MOE_BLK
    block_e = jnp.minimum(jnp.sum((pends[None, :] <= block_start[:, None]).astype(jnp.int32), axis=1),
                          N_EXPERTS - 1)
    nused = (pends[-1:] // MOE_BLK).astype(jnp.int32)
    mine = block_e[:, None] == jnp.arange(N_EXPERTS, dtype=jnp.int32)[None, :]
    valid_end = jnp.sum(jnp.where(mine, (pstarts + counts)[None, :], 0), axis=1)
    nvalid = jnp.clip(valid_end - block_start, 0, MOE_BLK)

    xs = _sc_scatter_rows(h2.reshape(T, D // 2), d1, d2, nb * MOE_BLK)
    experts = jnp.arange(N_EXPERTS, dtype=jnp.int32)
    later = (experts[None, :] > block_e[:, None]) & (counts[None, :] > 0)
    next_e = jnp.min(jnp.where(later, experts[None, :], N_EXPERTS), axis=1)
    next_e = jnp.where(next_e == N_EXPERTS, block_e, next_e)
    yb = _moe(block_e, nused, nvalid, next_e, xs, p["w_g"], p["w_u"], p["w_d"])
    ya, yc = _sc_gather_rows(yb, d1, d2)
    return _combine(x1.reshape(T, D), ya, yc, rt, p["gf"], out, b0 * S, x.shape[0] * S)


def _trunk(x, p):
    B, S, D = x.shape
    outs = _in_proj(x, p["n1"], p["w_qkvc"], p["conv_w"], *p["rot"])
    o_att = _attn(outs[:9])
    nseg = max(1, B // SEG_SEQS)
    out = None
    for seg in range(nseg):
        out = _moe_segment(o_att, outs[9], x, p, seg * (B // nseg), B // nseg, out)
    return out.reshape(B, S, D)


def kernel(x_prompt, x_sample, norm1_g, w_in, b_gate, conv_w, w_attn_out, w_conv_out, w_out, norm2_g,
           w_router_group, b_router_group, w_router_expert, b_router_expert, w_exp_gate, w_exp_up,
           w_exp_down, norm_f_g):
    assert norm1_g.shape[0] == 1, "single-layer trunk"
    S = x_prompt.shape[1]
    wr = jnp.zeros((ROUTER_ROWS, D_MODEL), F32)
    wr = wr.at[0:N_EXPERT_GROUPS].set(w_router_group[0].T).at[8:8 + N_EXPERTS].set(w_router_expert[0].T)
    br = jnp.zeros((ROUTER_ROWS,), F32)
    br = br.at[0:N_EXPERT_GROUPS].set(b_router_group[0]).at[8:8 + N_EXPERTS].set(b_router_expert[0])
    ti = jnp.arange(CUM_CHUNK)
    p = dict(
        n1=norm1_g, w_qkvc=w_in[0, :, :C_ZA].astype(BF16), w_gate=w_in[0, :, C_ZA:].astype(BF16),
        b_gate=b_gate, conv_w=conv_w[0], w_co=w_conv_out[0].astype(BF16), rot=_rotary_tables(S),
        w_ao=w_attn_out[0].astype(BF16), w_o=w_out[0].astype(BF16), n2=norm2_g,
        wr=wr.astype(BF16), br=jnp.broadcast_to(br[:, None], (ROUTER_ROWS, 128)),
        tri=(ti[:, None] < ti[None, :]).astype(BF16),
        w_g=w_exp_gate[0], w_u=w_exp_up[0], w_d=w_exp_down[0],
        gf=norm_f_g.reshape(1, D_MODEL),
    )
    return _trunk(x_prompt, p), _trunk(x_sample, p)
```

```python
import functools

import jax
import jax.numpy as jnp
from jax import lax
from jax.experimental import pallas as pl
from jax.experimental.pallas import tpu as pltpu
from jax.experimental.pallas import tpu_sc as plsc

D_MODEL = 1024
HEAD_DIM = 64
HEADS_PER_GROUP = 4
DILATIONS = (1, 4, 16)
HALF = 64
D_GRP = HEADS_PER_GROUP * HEAD_DIM
D_ATT = 3 * D_GRP
D_CONV = 768
N_EXPERT_GROUPS = 4
EXPERTS_PER_GROUP = 8
N_EXPERTS = 32
D_EXPERT = 512
RMS_EPS = 1e-6
NEG = -1e30
ROPE_THETA = 10000.0

C_Q, C_K, C_V = 0, D_ATT, 2 * D_ATT
C_CU, C_CB, C_CC = 3 * D_ATT, 3 * D_ATT + D_CONV, 3 * D_ATT + 2 * D_CONV
C_ZA = 3 * D_ATT + 3 * D_CONV

TSI = 1024
TS = 1024
TQ = 2048
QB = 128
KW = QB + 2 * HALF
MOE_BLK = 512
SEG_SEQS = 4
SC_ROWS = 64
ROUTER_ROWS = 48
CUM_CHUNK = 256
VMEM_LIMIT = 56 * 1024 * 1024

F32 = jnp.float32
BF16 = jnp.bfloat16


def _rms(xf, g):
    return xf * lax.rsqrt(jnp.mean(xf * xf, axis=-1, keepdims=True) + RMS_EPS) * g


def _sigmoid(x):
    return 1.0 / (1.0 + jnp.exp(-x))


def _pack_bf16_pairs(xb):
    half = xb.shape[1] // 2
    bits = pltpu.bitcast(xb.astype(F32), jnp.uint32)
    return (bits[:, :half] & jnp.uint32(0xFFFF0000)) | (bits[:, half:] >> 16)


def _unpack_bf16_pairs(packed, dtype):
    hi = pltpu.bitcast(packed & jnp.uint32(0xFFFF0000), F32)
    lo = pltpu.bitcast(packed << 16, F32)
    return jnp.concatenate([hi, lo], axis=1).astype(dtype)


def _const_spec(shape):
    return pl.BlockSpec(shape, lambda *_: (0,) * len(shape), pipeline_mode=pl.Buffered(1))


def _in_proj_kernel(x_ref, xp_ref, xn_ref, n1_ref, w_ref, cw_ref, cos_ref, sa_ref, sb_ref,
                    q1, k1, v1, q2, k2, v2, q3, k3, v3, mix_ref, pbuf, dbuf):
    j = pl.program_id(1)
    nj = pl.num_programs(1)
    g1 = n1_ref[...]
    hf = _rms(x_ref[0], g1)
    h = hf.astype(BF16)

    def proj(hh, c0, width):
        return jnp.dot(hh, w_ref[:, c0:c0 + width], preferred_element_type=F32)

    cosv = cos_ref[...]
    sav = sa_ref[...]
    sbv = sb_ref[...]

    def rotary(z):
        return z * cosv + pltpu.roll(z, 96, 1) * sav + pltpu.roll(z, 32, 1) * sbv

    def emit(outs, c0, rot, scale):
        for g, d in enumerate(DILATIONS):
            z = proj(h, c0 + g * D_GRP, D_GRP)
            if rot:
                z = jnp.concatenate([rotary(z[:, :128]), rotary(z[:, 128:])], axis=1)
            if scale != 1.0:
                z = z * scale
            if d == 1:
                outs[g][0, 0] = z.astype(BF16)
            else:
                dbuf[0] = z[:, :128]
                dbuf[1] = z[:, 128:]
                for r in range(d):
                    for c in range(2):
                        outs[g][0, r, :, c * 128:(c + 1) * 128] = (
                            dbuf[c, pl.ds(r, TSI // d, stride=d), :].astype(BF16))

    emit((q1, q2, q3), C_Q, True, HEAD_DIM ** -0.5)
    emit((k1, k2, k3), C_K, True, 1.0)
    emit((v1, v2, v3), C_V, False, 1.0)

    hp = _rms(xp_ref[0], g1)
    hn = _rms(xn_ref[0], g1)
    he = jnp.concatenate([hp, hf, hn], axis=0).astype(BF16)
    pe = proj(he, C_CC, D_CONV) * proj(he, C_CU, D_CONV)
    erow = lax.broadcasted_iota(jnp.int32, (TSI + 16, 1), 0)
    outside = ((erow < 8) & (j == 0)) | ((erow >= 8 + TSI) & (j == nj - 1))
    pbuf[...] = jnp.where(outside, 0.0, pe)
    conv = (cw_ref[0:1, :] * pbuf[7:7 + TSI, :] + cw_ref[1:2, :] * pbuf[8:8 + TSI, :]
            + cw_ref[2:3, :] * pbuf[9:9 + TSI, :])
    mix_ref[0] = (proj(h, C_CB, D_CONV) * conv).astype(BF16)


def _in_proj(x, n1, w_qkvc, conv_w, cos_t, sa_t, sb_t):
    B, S, D = x.shape
    nj = S // TSI
    rot_spec = pl.BlockSpec((TSI, 128), lambda b, j: (j, 0))
    in_specs = [
        pl.BlockSpec((1, TSI, D), lambda b, j: (b, j, 0)),
        pl.BlockSpec((1, 8, D), lambda b, j: (b, jnp.maximum(j * (TSI // 8) - 1, 0), 0)),
        pl.BlockSpec((1, 8, D), lambda b, j: (b, jnp.minimum((j + 1) * (TSI // 8), S // 8 - 1), 0)),
        _const_spec((1, D)),
        _const_spec(w_qkvc.shape),
        _const_spec((3, D_CONV)),
        rot_spec, rot_spec, rot_spec,
    ]
    out_shape, out_specs = [], []
    for d in DILATIONS:
        for _ in range(3):
            out_shape.append(jax.ShapeDtypeStruct((B, d, S // d, D_GRP), BF16))
            out_specs.append(pl.BlockSpec((1, d, TSI // d, D_GRP), lambda b, j: (b, 0, j, 0)))
    out_shape.append(jax.ShapeDtypeStruct((B, S, D_CONV), BF16))
    out_specs.append(pl.BlockSpec((1, TSI, D_CONV), lambda b, j: (b, j, 0)))
    return pl.pallas_call(
        _in_proj_kernel,
        grid=(B, nj),
        in_specs=in_specs,
        out_specs=out_specs,
        out_shape=out_shape,
        scratch_shapes=[pltpu.VMEM((TSI + 16, D_CONV), F32), pltpu.VMEM((2, TSI, 128), F32)],
        compiler_params=pltpu.CompilerParams(
            dimension_semantics=("parallel", "parallel"), vmem_limit_bytes=VMEM_LIMIT),
        name="in_proj",
    )(x, x, x, n1, w_qkvc, conv_w, cos_t, sa_t, sb_t)


def _attn_kernel(*refs):
    ins = refs[:21]
    bias_ref = refs[21]
    o_ref = refs[22]
    kbufs = refs[23:29]
    os_ref, ls_ref = refs[29], refs[30]
    j = pl.program_id(1)
    nj = pl.num_programs(1)

    lane = lax.broadcasted_iota(jnp.int32, (QB, D_GRP), 1)
    head_of_lane = lane // HEAD_DIM

    for g, d in enumerate(DILATIONS):
        q_ref, kc, kp, kn, vc, vp, vn = ins[7 * g:7 * g + 7]
        kb, vb = kbufs[2 * g], kbufs[2 * g + 1]
        n = TQ // d
        nblk = n // QB
        for buf, prev, cur, nxt in ((kb, kp, kc, kn), (vb, vp, vc, vn)):
            buf[:, 0:HALF, :] = prev[0]
            buf[:, HALF:HALF + n, :] = cur[0]
            buf[:, HALF + n:, :] = nxt[0]

        def scores(idx, q_ref=q_ref, kb=kb, nblk=nblk):
            r, jb = divmod(idx, nblk)
            qb = q_ref[0, r, jb * QB:(jb + 1) * QB, :]
            kw = kb[r, jb * QB:jb * QB + KW, :]
            qs = jnp.concatenate(
                [jnp.where(head_of_lane == hh, qb, jnp.zeros_like(qb)) for hh in range(HEADS_PER_GROUP)],
                axis=0)
            s = lax.dot_general(qs, kw, (((1,), (1,)), ((), ())), preferred_element_type=F32)
            variant = 0
            if jb == 0:
                variant = jnp.where(j == 0, 1, variant)
            if jb == nblk - 1:
                variant = jnp.where(j == nj - 1, 2, variant)
            return s + bias_ref[variant]

        def softmax(s):
            m = jnp.max(s, axis=1, keepdims=True)
            p = jnp.exp(s - m)
            l = jnp.sum(p, axis=1, keepdims=True)
            return p.astype(BF16), 1.0 / l, m + jnp.log(l)

        def finish(idx, pb, inv_l, lse, vb=vb, d=d, nblk=nblk, g=g):
            r, jb = divmod(idx, nblk)
            pv = jnp.dot(pb, vb[r, jb * QB:jb * QB + KW, :], preferred_element_type=F32)
            o = jnp.zeros((QB, D_GRP), F32)
            ls = jnp.zeros((QB, D_GRP), F32)
            for hh in range(HEADS_PER_GROUP):
                sl = slice(hh * QB, (hh + 1) * QB)
                sel = head_of_lane == hh
                o = jnp.where(sel, pv[sl] * inv_l[sl], o)
                ls = jnp.where(sel, lse[sl], ls)
            for c in range(2):
                cs = slice(c * 128, (c + 1) * 128)
                rows = pl.ds(jb * QB, QB) if d == 1 else pl.ds(r + jb * QB * d, QB, stride=d)
                os_ref[2 * g + c, rows, :] = o[:, cs]
                ls_ref[2 * g + c, rows, :] = ls[:, cs]

        nb = d * nblk
        s_of, sm_of = {}, {}
        for i in range(nb + 2):
            if i >= 2:
                finish(i - 2, *sm_of.pop(i - 2))
            if i < nb:
                s_of[i] = scores(i)
            if 1 <= i <= nb:
                sm_of[i - 1] = softmax(s_of.pop(i - 1))

    def merge(c, carry):
        rows = pl.ds(pl.multiple_of(c * QB, QB), QB)
        for c in range(2):
            l0, l1, l2 = ls_ref[c, rows, :], ls_ref[2 + c, rows, :], ls_ref[4 + c, rows, :]
            mm = jnp.maximum(jnp.maximum(l0, l1), l2)
            w0, w1, w2 = jnp.exp(l0 - mm), jnp.exp(l1 - mm), jnp.exp(l2 - mm)
            o = (w0 * os_ref[c, rows, :] + w1 * os_ref[2 + c, rows, :]
                 + w2 * os_ref[4 + c, rows, :]) / (w0 + w1 + w2)
            o_ref[0, rows, c * 128:(c + 1) * 128] = o.astype(BF16)
        return carry

    lax.fori_loop(0, TQ // QB, merge, 0)


def _band_bias():
    row = jnp.arange(HEADS_PER_GROUP * QB)[:, None] % QB
    col = jnp.arange(KW)[None, :]
    band = (col >= row) & (col <= row + 2 * HALF)
    variants = (band, band & (col >= HALF), band & (col < KW - HALF))
    return jnp.stack([jnp.where(v, 0.0, NEG).astype(F32) for v in variants])


def _attn(qkv):
    B = qkv[0].shape[0]
    S = qkv[0].shape[2]
    nj = S // TQ
    assert nj > 1
    ins, in_specs, scratch = [], [], []
    for g, d in enumerate(DILATIONS):
        q, k, v = qkv[3 * g:3 * g + 3]
        n = TQ // d
        L = S // d
        nh = n // HALF
        cur = pl.BlockSpec((1, d, n, D_GRP), lambda b, j: (b, 0, j, 0))
        prev = pl.BlockSpec((1, d, HALF, D_GRP),
                            lambda b, j, nh=nh: (b, 0, jnp.maximum(j * nh - 1, 0), 0))
        nxt = pl.BlockSpec((1, d, HALF, D_GRP),
                           lambda b, j, nh=nh, L=L: (b, 0, jnp.minimum((j + 1) * nh, L // HALF - 1), 0))
        ins += [q, k, k, k, v, v, v]
        in_specs += [cur, cur, prev, nxt, cur, prev, nxt]
        scratch += [pltpu.VMEM((d, n + 2 * HALF, D_GRP), BF16)] * 2
    scratch += [pltpu.VMEM((6, TQ, 128), F32), pltpu.VMEM((6, TQ, 128), F32)]
    ins.append(_band_bias())
    in_specs.append(_const_spec((3, HEADS_PER_GROUP * QB, KW)))
    return pl.pallas_call(
        _attn_kernel,
        grid=(B, nj),
        in_specs=in_specs,
        out_specs=pl.BlockSpec((1, TQ, D_GRP), lambda b, j: (b, j, 0)),
        out_shape=jax.ShapeDtypeStruct((B, S, D_GRP), BF16),
        scratch_shapes=scratch,
        compiler_params=pltpu.CompilerParams(
            dimension_semantics=("parallel", "parallel"), vmem_limit_bytes=VMEM_LIMIT),
        name="attn",
    )(*ins)


def _post_kernel(o_ref, mix_ref, x_ref, n1_ref, wz_ref, bg_ref, wco_ref, wao_ref, wo_ref, n2_ref,
                 wr_ref, br_ref, tri_ref, x1_ref, h2_ref, rt_ref, cnt_ref, carry):
    first = (pl.program_id(0) == 0) & (pl.program_id(1) == 0)

    @pl.when(first)
    def _():
        carry[...] = jnp.zeros_like(carry)

    x = x_ref[0]
    h = _rms(x, n1_ref[...]).astype(BF16)
    g_a = _sigmoid(jnp.dot(h, wz_ref[:, :D_MODEL], preferred_element_type=F32) + bg_ref[:, :D_MODEL])
    g_b = _sigmoid(jnp.dot(h, wz_ref[:, D_MODEL:], preferred_element_type=F32) + bg_ref[:, D_MODEL:])
    att = jnp.dot(o_ref[0], wao_ref[...], preferred_element_type=F32)
    cvb = jnp.dot(mix_ref[0], wco_ref[...], preferred_element_type=F32)
    merged = (g_a * att + g_b * cvb).astype(BF16)
    x1 = x + jnp.dot(merged, wo_ref[...], preferred_element_type=F32)
    x1_ref[0] = x1
    h2 = _rms(x1, n2_ref[...]).astype(BF16)
    h2_ref[0] = _pack_bf16_pairs(h2)

    lt = lax.dot_general(wr_ref[...], h2, (((1,), (1,)), ((), ())),
                         preferred_element_type=F32) + br_ref[:, 0:1]
    grow = lax.broadcasted_iota(jnp.int32, (8, TS), 0)
    gl = jnp.where(grow < N_EXPERT_GROUPS, lt[0:8], NEG)
    gmax = jnp.max(gl, axis=0, keepdims=True)
    grp = jnp.min(jnp.where(gl == gmax, grow, 8), axis=0, keepdims=True)
    grp_w = 1.0 / jnp.sum(jnp.exp(gl - gmax), axis=0, keepdims=True)
    erow = lax.broadcasted_iota(jnp.int32, (N_EXPERTS, TS), 0)
    el = jnp.where(erow // EXPERTS_PER_GROUP == grp, lt[8:8 + N_EXPERTS], NEG)
    v1 = jnp.max(el, axis=0, keepdims=True)
    i1 = jnp.min(jnp.where(el == v1, erow, N_EXPERTS), axis=0, keepdims=True)
    el2 = jnp.where(erow == i1, NEG, el)
    v2 = jnp.max(el2, axis=0, keepdims=True)
    i2 = jnp.min(jnp.where(el2 == v2, erow, N_EXPERTS), axis=0, keepdims=True)
    t = jnp.exp(v2 - v1)
    den = 1.0 + t
    gate1 = grp_w * (1.0 / den)
    gate2 = grp_w * (t / den)

    oh1 = (erow == i1).astype(F32)
    oh2 = (erow == i2).astype(F32)
    both = oh1 + oh2
    bothb = both.astype(BF16)
    offset = carry[:, 0:1]
    chunks = []
    for c in range(TS // CUM_CHUNK):
        bc = bothb[:, c * CUM_CHUNK:(c + 1) * CUM_CHUNK]
        chunks.append(offset + jnp.dot(bc, tri_ref[...], preferred_element_type=F32))
        offset = offset + jnp.sum(both[:, c * CUM_CHUNK:(c + 1) * CUM_CHUNK], axis=1, keepdims=True)
    basec = jnp.concatenate(chunks, axis=1)
    rank1 = jnp.sum(oh1 * basec, axis=0, keepdims=True)
    rank2 = jnp.sum(oh2 * basec, axis=0, keepdims=True)
    newc = carry[...] + (offset - carry[:, 0:1])
    carry[...] = newc
    cnt_ref[...] = newc
    zero = jnp.zeros((1, TS), F32)
    rt_ref[0] = jnp.concatenate(
        [i1.astype(F32), i2.astype(F32), gate1, gate2, rank1, rank2, zero, zero], axis=0)


def _post(o_att, mix, x, p, b0, nseq):
    _, S, D = x.shape
    nj = S // TS
    src = lambda w: pl.BlockSpec((1, TS, w), lambda b, j: (b + b0, j, 0))
    tile = lambda w: pl.BlockSpec((1, TS, w), lambda b, j: (b, j, 0))
    return pl.pallas_call(
        _post_kernel,
        grid=(nseq, nj),
        in_specs=[src(D_GRP), src(D_CONV), src(D),
                  _const_spec((1, D)), _const_spec((D, 2 * D)), _const_spec((1, 2 * D)),
                  _const_spec((D_CONV, D)), _const_spec((D_GRP, D)), _const_spec((D, D)),
                  _const_spec((1, D)), _const_spec((ROUTER_ROWS, D)), _const_spec((ROUTER_ROWS, 128)),
                  _const_spec((CUM_CHUNK, CUM_CHUNK))],
        out_specs=[tile(D), tile(D // 2),
                   pl.BlockSpec((1, 8, TS), lambda b, j: (b, 0, j)),
                   pl.BlockSpec((N_EXPERTS, 128), lambda b, j: (0, 0))],
        out_shape=[jax.ShapeDtypeStruct((nseq, S, D), F32), jax.ShapeDtypeStruct((nseq, S, D // 2), jnp.uint32),
                   jax.ShapeDtypeStruct((nseq, 8, S), F32),
                   jax.ShapeDtypeStruct((N_EXPERTS, 128), F32)],
        scratch_shapes=[pltpu.VMEM((N_EXPERTS, 128), F32)],
        compiler_params=pltpu.CompilerParams(
            dimension_semantics=("arbitrary", "arbitrary"), vmem_limit_bytes=VMEM_LIMIT),
        name="post",
    )(o_att, mix, x, p["n1"], p["w_gate"], p["b_gate"], p["w_co"], p["w_ao"], p["w_o"], p["n2"],
      p["wr"], p["br"], p["tri"])


def _moe_kernel(be_ref, nu_ref, nv_ref, nx_ref, x_ref, wg_hbm, wu_hbm, wd_hbm, out_ref,
                wgf, wuf, wdf, wgb, wub, wdb, sem, nchg):
    b = pl.program_id(0)

    def weight_copies(e, slot):
        return (pltpu.make_async_copy(wg_hbm.at[e], wgf.at[slot], sem.at[slot]),
                pltpu.make_async_copy(wu_hbm.at[e], wuf.at[slot], sem.at[slot]),
                pltpu.make_async_copy(wd_hbm.at[e], wdf.at[slot], sem.at[slot]))

    @pl.when(b == 0)
    def _():
        nchg[0] = 0
        for cp in weight_copies(be_ref[0], 0):
            cp.start()

    @pl.when(b < nu_ref[0])
    def _():
        @pl.when((b == 0) | (be_ref[b] != be_ref[jnp.maximum(b - 1, 0)]))
        def _():
            slot = nchg[0] % 2
            for cp in weight_copies(be_ref[b], slot):
                cp.wait()
            wgb[...] = wgf[slot].astype(BF16)
            wub[...] = wuf[slot].astype(BF16)
            wdb[...] = wdf[slot].astype(BF16)

            @pl.when(nx_ref[b] != be_ref[b])
            def _():
                for cp in weight_copies(nx_ref[b], 1 - slot):
                    cp.start()

            nchg[0] = nchg[0] + 1

        live = lax.broadcasted_iota(jnp.int32, (MOE_BLK, 1), 0) < nv_ref[b]
        x = _unpack_bf16_pairs(jnp.where(live, x_ref[...], jnp.uint32(0)), BF16)
        a = jnp.dot(x, wgb[...], preferred_element_type=F32)
        u = jnp.dot(x, wub[...], preferred_element_type=F32)
        hm = (a * _sigmoid(a) * u).astype(BF16)
        out_ref[...] = _pack_bf16_pairs(jnp.dot(hm, wdb[...], preferred_element_type=F32).astype(BF16))

    @pl.when(b >= nu_ref[0])
    def _():
        out_ref[...] = jnp.zeros_like(out_ref)


def _moe(block_e, nused, nvalid, next_e, xs, w_g, w_u, w_d):
    nb = block_e.shape[0]
    D = 2 * xs.shape[1]
    grid_spec = pltpu.PrefetchScalarGridSpec(
        num_scalar_prefetch=4,
        grid=(nb,),
        in_specs=[
            pl.BlockSpec((MOE_BLK, D // 2), lambda b, *_: (b, 0)),
            pl.BlockSpec(memory_space=pl.ANY),
            pl.BlockSpec(memory_space=pl.ANY),
            pl.BlockSpec(memory_space=pl.ANY),
        ],
        out_specs=pl.BlockSpec((MOE_BLK, D // 2), lambda b, *_: (b, 0)),
        scratch_shapes=[pltpu.VMEM((2, D, D_EXPERT), F32), pltpu.VMEM((2, D, D_EXPERT), F32),
                        pltpu.VMEM((2, D_EXPERT, D), F32),
                        pltpu.VMEM((D, D_EXPERT), BF16), pltpu.VMEM((D, D_EXPERT), BF16),
                        pltpu.VMEM((D_EXPERT, D), BF16),
                        pltpu.SemaphoreType.DMA((2,)), pltpu.SMEM((1,), jnp.int32)],
    )
    return pl.pallas_call(
        _moe_kernel,
        grid_spec=grid_spec,
        out_shape=jax.ShapeDtypeStruct((nb * MOE_BLK, D // 2), jnp.uint32),
        compiler_params=pltpu.CompilerParams(
            dimension_semantics=("arbitrary",), vmem_limit_bytes=VMEM_LIMIT),
        name="moe",
    )(block_e, nused, nvalid, next_e, xs, w_g, w_u, w_d)


def _sc_mesh():
    return plsc.VectorSubcoreMesh(core_axis_name="c", subcore_axis_name="s")


def _sc_worker_range(total):
    info = plsc.get_sparse_core_info()
    nw = info.num_cores * info.num_subcores
    wid = lax.axis_index("s") * info.num_cores + lax.axis_index("c")
    per_w = total // nw
    assert per_w * nw == total and per_w % SC_ROWS == 0
    return wid * per_w, per_w // SC_ROWS


def _sc_scatter_rows(src, idx_a, idx_b, nslots):
    T, D = src.shape

    @functools.partial(
        pl.kernel, out_type=jax.ShapeDtypeStruct((nslots, D), src.dtype), mesh=_sc_mesh(),
        scratch_types=[pltpu.VMEM((SC_ROWS,), jnp.int32), pltpu.VMEM((SC_ROWS,), jnp.int32),
                       pltpu.VMEM((SC_ROWS, D), src.dtype), pltpu.SemaphoreType.DMA],
        name="sc_dispatch")
    def k(x_hbm, ia_hbm, ib_hbm, o_hbm, ia_v, ib_v, rows_v, sem):
        start, nchunks = _sc_worker_range(T)

        @pl.loop(0, nchunks)
        def _(c):
            rows = pl.ds(start + c * SC_ROWS, SC_ROWS)
            pltpu.sync_copy(ia_hbm.at[rows], ia_v)
            pltpu.sync_copy(ib_hbm.at[rows], ib_v)
            pltpu.sync_copy(x_hbm.at[rows], rows_v)
            ca = pltpu.async_copy(rows_v, o_hbm.at[ia_v], sem)
            cb = pltpu.async_copy(rows_v, o_hbm.at[ib_v], sem)
            ca.wait()
            cb.wait()

    return k(src, idx_a, idx_b)


def _sc_gather_rows(src, idx_a, idx_b):
    D = src.shape[1]
    T = idx_a.shape[0]
    out = jax.ShapeDtypeStruct((T, D), src.dtype)

    @functools.partial(
        pl.kernel, out_type=(out, out), mesh=_sc_mesh(),
        scratch_types=[pltpu.VMEM((SC_ROWS,), jnp.int32), pltpu.VMEM((SC_ROWS,), jnp.int32),
                       pltpu.VMEM((SC_ROWS, D), src.dtype), pltpu.VMEM((SC_ROWS, D), src.dtype),
                       pltpu.SemaphoreType.DMA],
        name="sc_combine")
    def k(x_hbm, ia_hbm, ib_hbm, oa_hbm, ob_hbm, ia_v, ib_v, ra_v, rb_v, sem):
        start, nchunks = _sc_worker_range(T)

        @pl.loop(0, nchunks)
        def _(c):
            rows = pl.ds(start + c * SC_ROWS, SC_ROWS)
            pltpu.sync_copy(ia_hbm.at[rows], ia_v)
            pltpu.sync_copy(ib_hbm.at[rows], ib_v)
            ca = pltpu.async_copy(x_hbm.at[ia_v], ra_v, sem)
            cb = pltpu.async_copy(x_hbm.at[ib_v], rb_v, sem)
            ca.wait()
            cb.wait()
            pltpu.sync_copy(ra_v, oa_hbm.at[rows])
            pltpu.sync_copy(rb_v, ob_hbm.at[rows])

    return k(src, idx_a, idx_b)


def _combine_kernel(x1_ref, ya_ref, yb_ref, rt_ref, gf_ref, out_ref):
    gt = jnp.transpose(rt_ref[0])
    y = gt[:, 2:3] * _unpack_bf16_pairs(ya_ref[...], F32) + gt[:, 3:4] * _unpack_bf16_pairs(yb_ref[...], F32)
    out_ref[...] = _rms(x1_ref[...] + y, gf_ref[...])


def _combine(x1, ya, yb, rt, gf, out, row0, total_rows):
    T, D = x1.shape
    per_seq = rt.shape[2] // TS
    tile = pl.BlockSpec((TS, D), lambda i: (i, 0))

    def body(x1_ref, ya_ref, yb_ref, rt_ref, gf_ref, *rest):
        _combine_kernel(x1_ref, ya_ref, yb_ref, rt_ref, gf_ref, rest[-1])

    half = pl.BlockSpec((TS, D // 2), lambda i: (i, 0))
    in_specs = [tile, half, half,
                pl.BlockSpec((1, 8, TS), lambda i: (i // per_seq, 0, i % per_seq)),
                _const_spec((1, D))]
    args = [x1, ya, yb, rt, gf]
    aliases = {}
    if out is not None:
        in_specs.append(pl.BlockSpec(memory_space=pl.ANY))
        args.append(out)
        aliases = {5: 0}
    return pl.pallas_call(
        body,
        grid=(T // TS,),
        in_specs=in_specs,
        out_specs=pl.BlockSpec((TS, D), lambda i: (i + row0 // TS, 0)),
        out_shape=jax.ShapeDtypeStruct((total_rows, D), F32),
        input_output_aliases=aliases,
        compiler_params=pltpu.CompilerParams(
            dimension_semantics=("parallel",), vmem_limit_bytes=VMEM_LIMIT),
        name="combine",
    )(*args)


def _slots_kernel(ps_ref, rt_ref, d_ref):
    v = rt_ref[0]
    start = jnp.zeros(v.shape, F32)
    for e in range(N_EXPERTS):
        start = jnp.where(v == float(e), ps_ref[e].astype(F32), start)
    d_ref[0] = (start[0:2] + v[4:6]).astype(jnp.int32)


def _slots(pstarts, rt):
    B, _, S = rt.shape
    grid_spec = pltpu.PrefetchScalarGridSpec(
        num_scalar_prefetch=1,
        grid=(B,),
        in_specs=[pl.BlockSpec((1, 8, S), lambda b, ps: (b, 0, 0))],
        out_specs=pl.BlockSpec((1, 2, S), lambda b, ps: (b, 0, 0)),
    )
    return pl.pallas_call(
        _slots_kernel,
        grid_spec=grid_spec,
        out_shape=jax.ShapeDtypeStruct((B, 2, S), jnp.int32),
        compiler_params=pltpu.CompilerParams(dimension_semantics=("parallel",)),
        name="slots",
    )(pstarts, rt)


def _rotary_tables(S):
    inv_freq = 1.0 / (ROPE_THETA ** (jnp.arange(0, HEAD_DIM, 2, dtype=F32) / HEAD_DIM))
    ang = jnp.arange(S, dtype=F32)[:, None] * jnp.tile(inv_freq, 4)[None, :]
    cos_t, sin = jnp.cos(ang), jnp.sin(ang)
    first_half = (jnp.arange(128) % HEAD_DIM < HEAD_DIM // 2)[None, :]
    sa_t = jnp.where(first_half, -sin, 0.0)
    sb_t = jnp.where(first_half, 0.0, sin)
    return cos_t, sa_t, sb_t


def _moe_segment(o_att, mix, x, p, b0, nseq, out):
    _, S, D = x.shape
    T = nseq * S
    x1, h2, rt, cnt = _post(o_att, mix, x, p, b0, nseq)

    counts = cnt[:, 0].astype(jnp.int32)
    pcounts = (counts + MOE_BLK - 1) // MOE_BLK * MOE_BLK
    pends = jnp.cumsum(pcounts)
    pstarts = pends - pcounts
    dest = _slots(pstarts, rt)
    d1 = dest[:, 0, :].reshape(T)
    d2 = dest[:, 1, :].reshape(T)
    nb = (2 * T) // MOE_BLK + N_EXPERTS
    block_start = jnp.arange(nb, dtype=jnp.int32) * MOE_BLK
    block_e = jnp.minimum(jnp.sum((pends[None, :] <= block_start[:, None]).astype(jnp.int32), axis=1),
                          N_EXPERTS - 1)
    nused = (pends[-1:] // MOE_BLK).astype(jnp.int32)
    mine = block_e[:, None] == jnp.arange(N_EXPERTS, dtype=jnp.int32)[None, :]
    valid_end = jnp.sum(jnp.where(mine, (pstarts + counts)[None, :], 0), axis=1)
    nvalid = jnp.clip(valid_end - block_start, 0, MOE_BLK)

    xs = _sc_scatter_rows(h2.reshape(T, D // 2), d1, d2, nb * MOE_BLK)
    experts = jnp.arange(N_EXPERTS, dtype=jnp.int32)
    later = (experts[None, :] > block_e[:, None]) & (counts[None, :] > 0)
    next_e = jnp.min(jnp.where(later, experts[None, :], N_EXPERTS), axis=1)
    next_e = jnp.where(next_e == N_EXPERTS, block_e, next_e)
    yb = _moe(block_e, nused, nvalid, next_e, xs, p["w_g"], p["w_u"], p["w_d"])
    ya, yc = _sc_gather_rows(yb, d1, d2)
    return _combine(x1.reshape(T, D), ya, yc, rt, p["gf"], out, b0 * S, x.shape[0] * S)


def _trunk(x, p):
    B, S, D = x.shape
    outs = _in_proj(x, p["n1"], p["w_qkvc"], p["conv_w"], *p["rot"])
    o_att = _attn(outs[:9])
    nseg = max(1, B // SEG_SEQS)
    out = None
    for seg in range(nseg):
        out = _moe_segment(o_att, outs[9], x, p, seg * (B // nseg), B // nseg, out)
    return out.reshape(B, S, D)


def kernel(x_prompt, x_sample, norm1_g, w_in, b_gate, conv_w, w_attn_out, w_conv_out, w_out, norm2_g,
           w_router_group, b_router_group, w_router_expert, b_router_expert, w_exp_gate, w_exp_up,
           w_exp_down, norm_f_g):
    assert norm1_g.shape[0] == 1, "single-layer trunk"
    S = x_prompt.shape[1]
    wr = jnp.zeros((ROUTER_ROWS, D_MODEL), F32)
    wr = wr.at[0:N_EXPERT_GROUPS].set(w_router_group[0].T).at[8:8 + N_EXPERTS].set(w_router_expert[0].T)
    br = jnp.zeros((ROUTER_ROWS,), F32)
    br = br.at[0:N_EXPERT_GROUPS].set(b_router_group[0]).at[8:8 + N_EXPERTS].set(b_router_expert[0])
    ti = jnp.arange(CUM_CHUNK)
    p = dict(
        n1=norm1_g, w_qkvc=w_in[0, :, :C_ZA].astype(BF16), w_gate=w_in[0, :, C_ZA:].astype(BF16),
        b_gate=b_gate, conv_w=conv_w[0], w_co=w_conv_out[0].astype(BF16), rot=_rotary_tables(S),
        w_ao=w_attn_out[0].astype(BF16), w_o=w_out[0].astype(BF16), n2=norm2_g,
        wr=wr.astype(BF16), br=jnp.broadcast_to(br[:, None], (ROUTER_ROWS, 128)),
        tri=(ti[:, None] < ti[None, :]).astype(BF16),
        w_g=w_exp_gate[0], w_u=w_exp_up[0], w_d=w_exp_down[0],
        gf=norm_f_g.reshape(1, D_MODEL),
    )
    return _trunk(x_prompt, p), _trunk(x_sample, p)
```

```python
import functools

import jax
import jax.numpy as jnp
from jax import lax
from jax.experimental import pallas as pl
from jax.experimental.pallas import tpu as pltpu
from jax.experimental.pallas import tpu_sc as plsc

D_MODEL = 1024
HEAD_DIM = 64
HEADS_PER_GROUP = 4
DILATIONS = (1, 4, 16)
HALF = 64
D_GRP = HEADS_PER_GROUP * HEAD_DIM
D_ATT = 3 * D_GRP
D_CONV = 768
N_EXPERT_GROUPS = 4
EXPERTS_PER_GROUP = 8
N_EXPERTS = 32
D_EXPERT = 512
RMS_EPS = 1e-6
NEG = -1e30
ROPE_THETA = 10000.0
LOG2E = 1.4426950408889634

C_Q, C_K, C_V = 0, D_ATT, 2 * D_ATT
C_CU, C_CB, C_CC = 3 * D_ATT, 3 * D_ATT + D_CONV, 3 * D_ATT + 2 * D_CONV
C_ZA = 3 * D_ATT + 3 * D_CONV

TSI = 1024
TS = 1024
TQ = 2048
QB = 128
KW = QB + 2 * HALF
MOE_BLK = 512
SEG_SEQS = 4
SC_ROWS = 64
ROUTER_ROWS = 48
CUM_CHUNK = 256
VMEM_LIMIT = 56 * 1024 * 1024

F32 = jnp.float32
BF16 = jnp.bfloat16


def _rms(xf, g):
    return xf * lax.rsqrt(jnp.mean(xf * xf, axis=-1, keepdims=True) + RMS_EPS) * g


def _sigmoid(x):
    return 1.0 / (1.0 + jnp.exp(-x))


def _pack_bf16_pairs(xb):
    half = xb.shape[1] // 2
    bits = pltpu.bitcast(xb.astype(F32), jnp.uint32)
    return (bits[:, :half] & jnp.uint32(0xFFFF0000)) | (bits[:, half:] >> 16)


def _unpack_bf16_pairs(packed, dtype):
    hi = pltpu.bitcast(packed & jnp.uint32(0xFFFF0000), F32)
    lo = pltpu.bitcast(packed << 16, F32)
    return jnp.concatenate([hi, lo], axis=1).astype(dtype)


def _const_spec(shape):
    return pl.BlockSpec(shape, lambda *_: (0,) * len(shape), pipeline_mode=pl.Buffered(1))


def _in_proj_kernel(x_ref, xp_ref, xn_ref, n1_ref, w_ref, cw_ref, cos_ref, sa_ref, sb_ref,
                    q1, k1, v1, q2, k2, v2, q3, k3, v3, mix_ref, pbuf, dbuf):
    j = pl.program_id(1)
    nj = pl.num_programs(1)
    g1 = n1_ref[...]
    hf = _rms(x_ref[0], g1)
    h = hf.astype(BF16)

    def proj(hh, c0, width):
        return jnp.dot(hh, w_ref[:, c0:c0 + width], preferred_element_type=F32)

    cosv = cos_ref[...]
    sav = sa_ref[...]
    sbv = sb_ref[...]

    def rotary(z):
        return z * cosv + pltpu.roll(z, 96, 1) * sav + pltpu.roll(z, 32, 1) * sbv

    def emit(outs, c0, rot, scale):
        for g, d in enumerate(DILATIONS):
            z = proj(h, c0 + g * D_GRP, D_GRP)
            if rot:
                z = jnp.concatenate([rotary(z[:, :128]), rotary(z[:, 128:])], axis=1)
            if scale != 1.0:
                z = z * scale
            if d == 1:
                outs[g][0, 0] = z.astype(BF16)
            else:
                dbuf[0] = z[:, :128]
                dbuf[1] = z[:, 128:]
                for r in range(d):
                    for c in range(2):
                        outs[g][0, r, :, c * 128:(c + 1) * 128] = (
                            dbuf[c, pl.ds(r, TSI // d, stride=d), :].astype(BF16))

    emit((q1, q2, q3), C_Q, True, HEAD_DIM ** -0.5 * LOG2E)
    emit((k1, k2, k3), C_K, True, 1.0)
    emit((v1, v2, v3), C_V, False, 1.0)

    hp = _rms(xp_ref[0], g1)
    hn = _rms(xn_ref[0], g1)
    he = jnp.concatenate([hp, hf, hn], axis=0).astype(BF16)
    pe = proj(he, C_CC, D_CONV) * proj(he, C_CU, D_CONV)
    erow = lax.broadcasted_iota(jnp.int32, (TSI + 16, 1), 0)
    outside = ((erow < 8) & (j == 0)) | ((erow >= 8 + TSI) & (j == nj - 1))
    pbuf[...] = jnp.where(outside, 0.0, pe)
    conv = (cw_ref[0:1, :] * pbuf[7:7 + TSI, :] + cw_ref[1:2, :] * pbuf[8:8 + TSI, :]
            + cw_ref[2:3, :] * pbuf[9:9 + TSI, :])
    mix_ref[0] = (proj(h, C_CB, D_CONV) * conv).astype(BF16)


def _in_proj(x, n1, w_qkvc, conv_w, cos_t, sa_t, sb_t):
    B, S, D = x.shape
    nj = S // TSI
    rot_spec = pl.BlockSpec((TSI, 128), lambda b, j: (j, 0))
    in_specs = [
        pl.BlockSpec((1, TSI, D), lambda b, j: (b, j, 0)),
        pl.BlockSpec((1, 8, D), lambda b, j: (b, jnp.maximum(j * (TSI // 8) - 1, 0), 0)),
        pl.BlockSpec((1, 8, D), lambda b, j: (b, jnp.minimum((j + 1) * (TSI // 8), S // 8 - 1), 0)),
        _const_spec((1, D)),
        _const_spec(w_qkvc.shape),
        _const_spec((3, D_CONV)),
        rot_spec, rot_spec, rot_spec,
    ]
    out_shape, out_specs = [], []
    for d in DILATIONS:
        for _ in range(3):
            out_shape.append(jax.ShapeDtypeStruct((B, d, S // d, D_GRP), BF16))
            out_specs.append(pl.BlockSpec((1, d, TSI // d, D_GRP), lambda b, j: (b, 0, j, 0)))
    out_shape.append(jax.ShapeDtypeStruct((B, S, D_CONV), BF16))
    out_specs.append(pl.BlockSpec((1, TSI, D_CONV), lambda b, j: (b, j, 0)))
    return pl.pallas_call(
        _in_proj_kernel,
        grid=(B, nj),
        in_specs=in_specs,
        out_specs=out_specs,
        out_shape=out_shape,
        scratch_shapes=[pltpu.VMEM((TSI + 16, D_CONV), F32), pltpu.VMEM((2, TSI, 128), F32)],
        compiler_params=pltpu.CompilerParams(
            dimension_semantics=("parallel", "parallel"), vmem_limit_bytes=VMEM_LIMIT),
        name="in_proj",
    )(x, x, x, n1, w_qkvc, conv_w, cos_t, sa_t, sb_t)


def _attn_kernel(*refs):
    ins = refs[:21]
    bias_ref = refs[21]
    o_ref = refs[22]
    kbufs = refs[23:29]
    os_ref, ls_ref = refs[29], refs[30]
    j = pl.program_id(1)
    nj = pl.num_programs(1)

    lane = lax.broadcasted_iota(jnp.int32, (QB, D_GRP), 1)
    head_of_lane = lane // HEAD_DIM

    for g, d in enumerate(DILATIONS):
        q_ref, kc, kp, kn, vc, vp, vn = ins[7 * g:7 * g + 7]
        kb, vb = kbufs[2 * g], kbufs[2 * g + 1]
        n = TQ // d
        nblk = n // QB
        for buf, prev, cur, nxt in ((kb, kp, kc, kn), (vb, vp, vc, vn)):
            buf[:, 0:HALF, :] = prev[0]
            buf[:, HALF:HALF + n, :] = cur[0]
            buf[:, HALF + n:, :] = nxt[0]

        def scores(idx, q_ref=q_ref, kb=kb, nblk=nblk):
            r, jb = divmod(idx, nblk)
            qb = q_ref[0, r, jb * QB:(jb + 1) * QB, :]
            kw = kb[r, jb * QB:jb * QB + KW, :]
            qs = jnp.concatenate(
                [jnp.where(head_of_lane == hh, qb, jnp.zeros_like(qb)) for hh in range(HEADS_PER_GROUP)],
                axis=0)
            s = lax.dot_general(qs, kw, (((1,), (1,)), ((), ())), preferred_element_type=F32)
            variant = 0
            if jb == 0:
                variant = jnp.where(j == 0, 1, variant)
            if jb == nblk - 1:
                variant = jnp.where(j == nj - 1, 2, variant)
            return s + bias_ref[variant]

        def token_rows(idx, d=d, nblk=nblk):
            r, jb = divmod(idx, nblk)
            return pl.ds(jb * QB, QB) if d == 1 else pl.ds(r + jb * QB * d, QB, stride=d)

        def per_head_lanes(col):
            out = jnp.broadcast_to(col[0:QB], (QB, D_GRP))
            for hh in range(1, HEADS_PER_GROUP):
                out = jnp.where(head_of_lane == hh, col[hh * QB:(hh + 1) * QB], out)
            return out

        def softmax(idx, s, g=g):
            m = jnp.max(s, axis=1, keepdims=True)
            p = jnp.exp2(s - m)
            l = jnp.sum(p, axis=1, keepdims=True)
            ls = per_head_lanes(m + jnp.log2(l))
            for c in range(2):
                ls_ref[2 * g + c, token_rows(idx), :] = ls[:, c * 128:(c + 1) * 128]
            return p.astype(BF16), per_head_lanes(1.0 / l)

        def finish(idx, pb, inv, vb=vb, nblk=nblk, g=g):
            r, jb = divmod(idx, nblk)
            pv = jnp.dot(pb, vb[r, jb * QB:jb * QB + KW, :], preferred_element_type=F32)
            o = pv[0:QB]
            for hh in range(1, HEADS_PER_GROUP):
                o = jnp.where(head_of_lane == hh, pv[hh * QB:(hh + 1) * QB], o)
            o = o * inv
            for c in range(2):
                os_ref[2 * g + c, token_rows(idx), :] = o[:, c * 128:(c + 1) * 128]

        nb = d * nblk
        s_of, sm_of = {}, {}
        for i in range(nb + 2):
            if i >= 2:
                finish(i - 2, *sm_of.pop(i - 2))
            if i < nb:
                s_of[i] = scores(i)
            if 1 <= i <= nb:
                sm_of[i - 1] = softmax(i - 1, s_of.pop(i - 1))

    def merge(c, carry):
        rows = pl.ds(pl.multiple_of(c * QB, QB), QB)
        for c in range(2):
            l0, l1, l2 = ls_ref[c, rows, :], ls_ref[2 + c, rows, :], ls_ref[4 + c, rows, :]
            mm = jnp.maximum(jnp.maximum(l0, l1), l2)
            w0, w1, w2 = jnp.exp2(l0 - mm), jnp.exp2(l1 - mm), jnp.exp2(l2 - mm)
            o = (w0 * os_ref[c, rows, :] + w1 * os_ref[2 + c, rows, :]
                 + w2 * os_ref[4 + c, rows, :]) / (w0 + w1 + w2)
            o_ref[0, rows, c * 128:(c + 1) * 128] = o.astype(BF16)
        return carry

    lax.fori_loop(0, TQ // QB, merge, 0)


def _band_bias():
    row = jnp.arange(HEADS_PER_GROUP * QB)[:, None] % QB
    col = jnp.arange(KW)[None, :]
    band = (col >= row) & (col <= row + 2 * HALF)
    variants = (band, band & (col >= HALF), band & (col < KW - HALF))
    return jnp.stack([jnp.where(v, 0.0, NEG).astype(F32) for v in variants])


def _attn(qkv):
    B = qkv[0].shape[0]
    S = qkv[0].shape[2]
    nj = S // TQ
    assert nj > 1
    ins, in_specs, scratch = [], [], []
    for g, d in enumerate(DILATIONS):
        q, k, v = qkv[3 * g:3 * g + 3]
        n = TQ // d
        L = S // d
        nh = n // HALF
        cur = pl.BlockSpec((1, d, n, D_GRP), lambda b, j: (b, 0, j, 0))
        prev = pl.BlockSpec((1, d, HALF, D_GRP),
                            lambda b, j, nh=nh: (b, 0, jnp.maximum(j * nh - 1, 0), 0))
        nxt = pl.BlockSpec((1, d, HALF, D_GRP),
                           lambda b, j, nh=nh, L=L: (b, 0, jnp.minimum((j + 1) * nh, L // HALF - 1), 0))
        ins += [q, k, k, k, v, v, v]
        in_specs += [cur, cur, prev, nxt, cur, prev, nxt]
        scratch += [pltpu.VMEM((d, n + 2 * HALF, D_GRP), BF16)] * 2
    scratch += [pltpu.VMEM((6, TQ, 128), F32), pltpu.VMEM((6, TQ, 128), F32)]
    ins.append(_band_bias())
    in_specs.append(_const_spec((3, HEADS_PER_GROUP * QB, KW)))
    return pl.pallas_call(
        _attn_kernel,
        grid=(B, nj),
        in_specs=in_specs,
        out_specs=pl.BlockSpec((1, TQ, D_GRP), lambda b, j: (b, j, 0)),
        out_shape=jax.ShapeDtypeStruct((B, S, D_GRP), BF16),
        scratch_shapes=scratch,
        compiler_params=pltpu.CompilerParams(
            dimension_semantics=("parallel", "parallel"), vmem_limit_bytes=VMEM_LIMIT),
        name="attn",
    )(*ins)


def _post_kernel(o_ref, mix_ref, x_ref, n1_ref, wz_ref, bg_ref, wco_ref, wao_ref, wo_ref, n2_ref,
                 wr_ref, br_ref, tri_ref, x1_ref, h2_ref, rt_ref, cnt_ref, carry):
    first = (pl.program_id(0) == 0) & (pl.program_id(1) == 0)

    @pl.when(first)
    def _():
        carry[...] = jnp.zeros_like(carry)

    x = x_ref[0]
    h = _rms(x, n1_ref[...]).astype(BF16)
    g_a = _sigmoid(jnp.dot(h, wz_ref[:, :D_MODEL], preferred_element_type=F32) + bg_ref[:, :D_MODEL])
    g_b = _sigmoid(jnp.dot(h, wz_ref[:, D_MODEL:], preferred_element_type=F32) + bg_ref[:, D_MODEL:])
    att = jnp.dot(o_ref[0], wao_ref[...], preferred_element_type=F32)
    cvb = jnp.dot(mix_ref[0], wco_ref[...], preferred_element_type=F32)
    merged = (g_a * att + g_b * cvb).astype(BF16)
    x1 = x + jnp.dot(merged, wo_ref[...], preferred_element_type=F32)
    x1_ref[0] = x1
    h2 = _rms(x1, n2_ref[...]).astype(BF16)
    h2_ref[0] = _pack_bf16_pairs(h2)

    lt = lax.dot_general(wr_ref[...], h2, (((1,), (1,)), ((), ())),
                         preferred_element_type=F32) + br_ref[:, 0:1]
    grow = lax.broadcasted_iota(jnp.int32, (8, TS), 0)
    gl = jnp.where(grow < N_EXPERT_GROUPS, lt[0:8], NEG)
    gmax = jnp.max(gl, axis=0, keepdims=True)
    grp = jnp.min(jnp.where(gl == gmax, grow, 8), axis=0, keepdims=True)
    grp_w = 1.0 / jnp.sum(jnp.exp(gl - gmax), axis=0, keepdims=True)
    erow = lax.broadcasted_iota(jnp.int32, (N_EXPERTS, TS), 0)
    el = jnp.where(erow // EXPERTS_PER_GROUP == grp, lt[8:8 + N_EXPERTS], NEG)
    v1 = jnp.max(el, axis=0, keepdims=True)
    i1 = jnp.min(jnp.where(el == v1, erow, N_EXPERTS), axis=0, keepdims=True)
    el2 = jnp.where(erow == i1, NEG, el)
    v2 = jnp.max(el2, axis=0, keepdims=True)
    i2 = jnp.min(jnp.where(el2 == v2, erow, N_EXPERTS), axis=0, keepdims=True)
    t = jnp.exp(v2 - v1)
    den = 1.0 + t
    gate1 = grp_w * (1.0 / den)
    gate2 = grp_w * (t / den)

    oh1 = (erow == i1).astype(F32)
    oh2 = (erow == i2).astype(F32)
    both = oh1 + oh2
    bothb = both.astype(BF16)
    offset = carry[:, 0:1]
    chunks = []
    for c in range(TS // CUM_CHUNK):
        bc = bothb[:, c * CUM_CHUNK:(c + 1) * CUM_CHUNK]
        chunks.append(offset + jnp.dot(bc, tri_ref[...], preferred_element_type=F32))
        offset = offset + jnp.sum(both[:, c * CUM_CHUNK:(c + 1) * CUM_CHUNK], axis=1, keepdims=True)
    basec = jnp.concatenate(chunks, axis=1)
    rank1 = jnp.sum(oh1 * basec, axis=0, keepdims=True)
    rank2 = jnp.sum(oh2 * basec, axis=0, keepdims=True)
    newc = carry[...] + (offset - carry[:, 0:1])
    carry[...] = newc
    cnt_ref[...] = newc
    zero = jnp.zeros((1, TS), F32)
    rt_ref[0] = jnp.concatenate(
        [i1.astype(F32), i2.astype(F32), gate1, gate2, rank1, rank2, zero, zero], axis=0)


def _post(o_att, mix, x, p, b0, nseq):
    _, S, D = x.shape
    nj = S // TS
    src = lambda w: pl.BlockSpec((1, TS, w), lambda b, j: (b + b0, j, 0))
    tile = lambda w: pl.BlockSpec((1, TS, w), lambda b, j: (b, j, 0))
    return pl.pallas_call(
        _post_kernel,
        grid=(nseq, nj),
        in_specs=[src(D_GRP), src(D_CONV), src(D),
                  _const_spec((1, D)), _const_spec((D, 2 * D)), _const_spec((1, 2 * D)),
                  _const_spec((D_CONV, D)), _const_spec((D_GRP, D)), _const_spec((D, D)),
                  _const_spec((1, D)), _const_spec((ROUTER_ROWS, D)), _const_spec((ROUTER_ROWS, 128)),
                  _const_spec((CUM_CHUNK, CUM_CHUNK))],
        out_specs=[tile(D), tile(D // 2),
                   pl.BlockSpec((1, 8, TS), lambda b, j: (b, 0, j)),
                   pl.BlockSpec((N_EXPERTS, 128), lambda b, j: (0, 0))],
        out_shape=[jax.ShapeDtypeStruct((nseq, S, D), F32), jax.ShapeDtypeStruct((nseq, S, D // 2), jnp.uint32),
                   jax.ShapeDtypeStruct((nseq, 8, S), F32),
                   jax.ShapeDtypeStruct((N_EXPERTS, 128), F32)],
        scratch_shapes=[pltpu.VMEM((N_EXPERTS, 128), F32)],
        compiler_params=pltpu.CompilerParams(
            dimension_semantics=("arbitrary", "arbitrary"), vmem_limit_bytes=VMEM_LIMIT),
        name="post",
    )(o_att, mix, x, p["n1"], p["w_gate"], p["b_gate"], p["w_co"], p["w_ao"], p["w_o"], p["n2"],
      p["wr"], p["br"], p["tri"])


def _moe_kernel(be_ref, nu_ref, nv_ref, nx_ref, x_ref, wg_hbm, wu_hbm, wd_hbm, out_ref,
                wgf, wuf, wdf, wgb, wub, wdb, sem, nchg):
    b = pl.program_id(0)

    def weight_copies(e, slot):
        return (pltpu.make_async_copy(wg_hbm.at[e], wgf.at[slot], sem.at[slot]),
                pltpu.make_async_copy(wu_hbm.at[e], wuf.at[slot], sem.at[slot]),
                pltpu.make_async_copy(wd_hbm.at[e], wdf.at[slot], sem.at[slot]))

    @pl.when(b == 0)
    def _():
        nchg[0] = 0
        for cp in weight_copies(be_ref[0], 0):
            cp.start()

    @pl.when(b < nu_ref[0])
    def _():
        @pl.when((b == 0) | (be_ref[b] != be_ref[jnp.maximum(b - 1, 0)]))
        def _():
            slot = nchg[0] % 2
            for cp in weight_copies(be_ref[b], slot):
                cp.wait()
            wgb[...] = wgf[slot].astype(BF16)
            wub[...] = wuf[slot].astype(BF16)
            wdb[...] = wdf[slot].astype(BF16)

            @pl.when(nx_ref[b] != be_ref[b])
            def _():
                for cp in weight_copies(nx_ref[b], 1 - slot):
                    cp.start()

            nchg[0] = nchg[0] + 1

        live = lax.broadcasted_iota(jnp.int32, (MOE_BLK, 1), 0) < nv_ref[b]
        x = _unpack_bf16_pairs(jnp.where(live, x_ref[...], jnp.uint32(0)), BF16)
        a = jnp.dot(x, wgb[...], preferred_element_type=F32)
        u = jnp.dot(x, wub[...], preferred_element_type=F32)
        hm = (a * _sigmoid(a) * u).astype(BF16)
        out_ref[...] = _pack_bf16_pairs(jnp.dot(hm, wdb[...], preferred_element_type=F32).astype(BF16))

    @pl.when(b >= nu_ref[0])
    def _():
        out_ref[...] = jnp.zeros_like(out_ref)


def _moe(block_e, nused, nvalid, next_e, xs, w_g, w_u, w_d):
    nb = block_e.shape[0]
    D = 2 * xs.shape[1]
    grid_spec = pltpu.PrefetchScalarGridSpec(
        num_scalar_prefetch=4,
        grid=(nb,),
        in_specs=[
            pl.BlockSpec((MOE_BLK, D // 2), lambda b, *_: (b, 0)),
            pl.BlockSpec(memory_space=pl.ANY),
            pl.BlockSpec(memory_space=pl.ANY),
            pl.BlockSpec(memory_space=pl.ANY),
        ],
        out_specs=pl.BlockSpec((MOE_BLK, D // 2), lambda b, *_: (b, 0)),
        scratch_shapes=[pltpu.VMEM((2, D, D_EXPERT), F32), pltpu.VMEM((2, D, D_EXPERT), F32),
                        pltpu.VMEM((2, D_EXPERT, D), F32),
                        pltpu.VMEM((D, D_EXPERT), BF16), pltpu.VMEM((D, D_EXPERT), BF16),
                        pltpu.VMEM((D_EXPERT, D), BF16),
                        pltpu.SemaphoreType.DMA((2,)), pltpu.SMEM((1,), jnp.int32)],
    )
    return pl.pallas_call(
        _moe_kernel,
        grid_spec=grid_spec,
        out_shape=jax.ShapeDtypeStruct((nb * MOE_BLK, D // 2), jnp.uint32),
        compiler_params=pltpu.CompilerParams(
            dimension_semantics=("arbitrary",), vmem_limit_bytes=VMEM_LIMIT),
        name="moe",
    )(block_e, nused, nvalid, next_e, xs, w_g, w_u, w_d)


def _sc_mesh():
    return plsc.VectorSubcoreMesh(core_axis_name="c", subcore_axis_name="s")


def _sc_worker_range(total):
    info = plsc.get_sparse_core_info()
    nw = info.num_cores * info.num_subcores
    wid = lax.axis_index("s") * info.num_cores + lax.axis_index("c")
    per_w = total // nw
    assert per_w * nw == total and per_w % SC_ROWS == 0
    return wid * per_w, per_w // SC_ROWS


def _sc_scatter_rows(src, idx_a, idx_b, nslots):
    T, D = src.shape

    @functools.partial(
        pl.kernel, out_type=jax.ShapeDtypeStruct((nslots, D), src.dtype), mesh=_sc_mesh(),
        scratch_types=[pltpu.VMEM((SC_ROWS,), jnp.int32), pltpu.VMEM((SC_ROWS,), jnp.int32),
                       pltpu.VMEM((SC_ROWS, D), src.dtype), pltpu.SemaphoreType.DMA],
        name="sc_dispatch")
    def k(x_hbm, ia_hbm, ib_hbm, o_hbm, ia_v, ib_v, rows_v, sem):
        start, nchunks = _sc_worker_range(T)

        @pl.loop(0, nchunks)
        def _(c):
            rows = pl.ds(start + c * SC_ROWS, SC_ROWS)
            pltpu.sync_copy(ia_hbm.at[rows], ia_v)
            pltpu.sync_copy(ib_hbm.at[rows], ib_v)
            pltpu.sync_copy(x_hbm.at[rows], rows_v)
            ca = pltpu.async_copy(rows_v, o_hbm.at[ia_v], sem)
            cb = pltpu.async_copy(rows_v, o_hbm.at[ib_v], sem)
            ca.wait()
            cb.wait()

    return k(src, idx_a, idx_b)


def _sc_gather_rows(src, idx_a, idx_b):
    D = src.shape[1]
    T = idx_a.shape[0]
    out = jax.ShapeDtypeStruct((T, D), src.dtype)

    @functools.partial(
        pl.kernel, out_type=(out, out), mesh=_sc_mesh(),
        scratch_types=[pltpu.VMEM((SC_ROWS,), jnp.int32), pltpu.VMEM((SC_ROWS,), jnp.int32),
                       pltpu.VMEM((SC_ROWS, D), src.dtype), pltpu.VMEM((SC_ROWS, D), src.dtype),
                       pltpu.SemaphoreType.DMA],
        name="sc_combine")
    def k(x_hbm, ia_hbm, ib_hbm, oa_hbm, ob_hbm, ia_v, ib_v, ra_v, rb_v, sem):
        start, nchunks = _sc_worker_range(T)

        @pl.loop(0, nchunks)
        def _(c):
            rows = pl.ds(start + c * SC_ROWS, SC_ROWS)
            pltpu.sync_copy(ia_hbm.at[rows], ia_v)
            pltpu.sync_copy(ib_hbm.at[rows], ib_v)
            ca = pltpu.async_copy(x_hbm.at[ia_v], ra_v, sem)
            cb = pltpu.async_copy(x_hbm.at[ib_v], rb_v, sem)
            ca.wait()
            cb.wait()
            pltpu.sync_copy(ra_v, oa_hbm.at[rows])
            pltpu.sync_copy(rb_v, ob_hbm.at[rows])

    return k(src, idx_a, idx_b)


def _combine_kernel(x1_ref, ya_ref, yb_ref, rt_ref, gf_ref, out_ref):
    gt = jnp.transpose(rt_ref[0])
    y = gt[:, 2:3] * _unpack_bf16_pairs(ya_ref[...], F32) + gt[:, 3:4] * _unpack_bf16_pairs(yb_ref[...], F32)
    out_ref[...] = _rms(x1_ref[...] + y, gf_ref[...])


def _combine(x1, ya, yb, rt, gf, out, row0, total_rows):
    T, D = x1.shape
    per_seq = rt.shape[2] // TS
    tile = pl.BlockSpec((TS, D), lambda i: (i, 0))

    def body(x1_ref, ya_ref, yb_ref, rt_ref, gf_ref, *rest):
        _combine_kernel(x1_ref, ya_ref, yb_ref, rt_ref, gf_ref, rest[-1])

    half = pl.BlockSpec((TS, D // 2), lambda i: (i, 0))
    in_specs = [tile, half, half,
                pl.BlockSpec((1, 8, TS), lambda i: (i // per_seq, 0, i % per_seq)),
                _const_spec((1, D))]
    args = [x1, ya, yb, rt, gf]
    aliases = {}
    if out is not None:
        in_specs.append(pl.BlockSpec(memory_space=pl.ANY))
        args.append(out)
        aliases = {5: 0}
    return pl.pallas_call(
        body,
        grid=(T // TS,),
        in_specs=in_specs,
        out_specs=pl.BlockSpec((TS, D), lambda i: (i + row0 // TS, 0)),
        out_shape=jax.ShapeDtypeStruct((total_rows, D), F32),
        input_output_aliases=aliases,
        compiler_params=pltpu.CompilerParams(
            dimension_semantics=("parallel",), vmem_limit_bytes=VMEM_LIMIT),
        name="combine",
    )(*args)


def _slots_kernel(ps_ref, rt_ref, d_ref):
    v = rt_ref[0]
    start = jnp.zeros(v.shape, F32)
    for e in range(N_EXPERTS):
        start = jnp.where(v == float(e), ps_ref[e].astype(F32), start)
    d_ref[0] = (start[0:2] + v[4:6]).astype(jnp.int32)


def _slots(pstarts, rt):
    B, _, S = rt.shape
    grid_spec = pltpu.PrefetchScalarGridSpec(
        num_scalar_prefetch=1,
        grid=(B,),
        in_specs=[pl.BlockSpec((1, 8, S), lambda b, ps: (b, 0, 0))],
        out_specs=pl.BlockSpec((1, 2, S), lambda b, ps: (b, 0, 0)),
    )
    return pl.pallas_call(
        _slots_kernel,
        grid_spec=grid_spec,
        out_shape=jax.ShapeDtypeStruct((B, 2, S), jnp.int32),
        compiler_params=pltpu.CompilerParams(dimension_semantics=("parallel",)),
        name="slots",
    )(pstarts, rt)


def _rotary_tables(S):
    inv_freq = 1.0 / (ROPE_THETA ** (jnp.arange(0, HEAD_DIM, 2, dtype=F32) / HEAD_DIM))
    ang = jnp.arange(S, dtype=F32)[:, None] * jnp.tile(inv_freq, 4)[None, :]
    cos_t, sin = jnp.cos(ang), jnp.sin(ang)
    first_half = (jnp.arange(128) % HEAD_DIM < HEAD_DIM // 2)[None, :]
    sa_t = jnp.where(first_half, -sin, 0.0)
    sb_t = jnp.where(first_half, 0.0, sin)
    return cos_t, sa_t, sb_t


def _moe_segment(o_att, mix, x, p, b0, nseq, out):
    _, S, D = x.shape
    T = nseq * S
    x1, h2, rt, cnt = _post(o_att, mix, x, p, b0, nseq)

    counts = cnt[:, 0].astype(jnp.int32)
    pcounts = (counts + MOE_BLK - 1) // MOE_BLK * MOE_BLK
    pends = jnp.cumsum(pcounts)
    pstarts = pends - pcounts
    dest = _slots(pstarts, rt)
    d1 = dest[:, 0, :].reshape(T)
    d2 = dest[:, 1, :].reshape(T)
    nb = (2 * T) // MOE_BLK + N_EXPERTS
    block_start = jnp.arange(nb, dtype=jnp.int32) * MOE_BLK
    block_e = jnp.minimum(jnp.sum((pends[None, :] <= block_start[:, None]).astype(jnp.int32), axis=1),
                          N_EXPERTS - 1)
    nused = (pends[-1:] // MOE_BLK).astype(jnp.int32)
    mine = block_e[:, None] == jnp.arange(N_EXPERTS, dtype=jnp.int32)[None, :]
    valid_end = jnp.sum(jnp.where(mine, (pstarts + counts)[None, :], 0), axis=1)
    nvalid = jnp.clip(valid_end - block_start, 0, MOE_BLK)

    xs = _sc_scatter_rows(h2.reshape(T, D // 2), d1, d2, nb * MOE_BLK)
    experts = jnp.arange(N_EXPERTS, dtype=jnp.int32)
    later = (experts[None, :] > block_e[:, None]) & (counts[None, :] > 0)
    next_e = jnp.min(jnp.where(later, experts[None, :], N_EXPERTS), axis=1)
    next_e = jnp.where(next_e == N_EXPERTS, block_e, next_e)
    yb = _moe(block_e, nused, nvalid, next_e, xs, p["w_g"], p["w_u"], p["w_d"])
    ya, yc = _sc_gather_rows(yb, d1, d2)
    return _combine(x1.reshape(T, D), ya, yc, rt, p["gf"], out, b0 * S, x.shape[0] * S)


def _trunk(x, p):
    B, S, D = x.shape
    outs = _in_proj(x, p["n1"], p["w_qkvc"], p["conv_w"], *p["rot"])
    o_att = _attn(outs[:9])
    nseg = max(1, B // SEG_SEQS)
    out = None
    for seg in range(nseg):
        out = _moe_segment(o_att, outs[9], x, p, seg * (B // nseg), B // nseg, out)
    return out.reshape(B, S, D)


def kernel(x_prompt, x_sample, norm1_g, w_in, b_gate, conv_w, w_attn_out, w_conv_out, w_out, norm2_g,
           w_router_group, b_router_group, w_router_expert, b_router_expert, w_exp_gate, w_exp_up,
           w_exp_down, norm_f_g):
    assert norm1_g.shape[0] == 1, "single-layer trunk"
    S = x_prompt.shape[1]
    wr = jnp.zeros((ROUTER_ROWS, D_MODEL), F32)
    wr = wr.at[0:N_EXPERT_GROUPS].set(w_router_group[0].T).at[8:8 + N_EXPERTS].set(w_router_expert[0].T)
    br = jnp.zeros((ROUTER_ROWS,), F32)
    br = br.at[0:N_EXPERT_GROUPS].set(b_router_group[0]).at[8:8 + N_EXPERTS].set(b_router_expert[0])
    ti = jnp.arange(CUM_CHUNK)
    p = dict(
        n1=norm1_g, w_qkvc=w_in[0, :, :C_ZA].astype(BF16), w_gate=w_in[0, :, C_ZA:].astype(BF16),
        b_gate=b_gate, conv_w=conv_w[0], w_co=w_conv_out[0].astype(BF16), rot=_rotary_tables(S),
        w_ao=w_attn_out[0].astype(BF16), w_o=w_out[0].astype(BF16), n2=norm2_g,
        wr=wr.astype(BF16), br=jnp.broadcast_to(br[:, None], (ROUTER_ROWS, 128)),
        tri=(ti[:, None] < ti[None, :]).astype(BF16),
        w_g=w_exp_gate[0], w_u=w_exp_up[0], w_d=w_exp_down[0],
        gf=norm_f_g.reshape(1, D_MODEL),
    )
    return _trunk(x_prompt, p), _trunk(x_sample, p)
```

```python
import functools

import jax
import jax.numpy as jnp
from jax import lax
from jax.experimental import pallas as pl
from jax.experimental.pallas import tpu as pltpu
from jax.experimental.pallas import tpu_sc as plsc

D_MODEL = 1024
HEAD_DIM = 64
HEADS_PER_GROUP = 4
DILATIONS = (1, 4, 16)
HALF = 64
D_GRP = HEADS_PER_GROUP * HEAD_DIM
D_ATT = 3 * D_GRP
D_CONV = 768
N_EXPERT_GROUPS = 4
EXPERTS_PER_GROUP = 8
N_EXPERTS = 32
D_EXPERT = 512
RMS_EPS = 1e-6
NEG = -1e30
ROPE_THETA = 10000.0

C_Q, C_K, C_V = 0, D_ATT, 2 * D_ATT
C_CU, C_CB, C_CC = 3 * D_ATT, 3 * D_ATT + D_CONV, 3 * D_ATT + 2 * D_CONV
C_ZA = 3 * D_ATT + 3 * D_CONV

TSI = 1024
TS = 1024
TQ = 2048
QB = 128
KW = QB + 2 * HALF
MOE_BLK = 512
SEG_SEQS = 4
SC_ROWS = 32
ROUTER_ROWS = 48
CUM_CHUNK = 256
VMEM_LIMIT = 56 * 1024 * 1024

F32 = jnp.float32
BF16 = jnp.bfloat16


def _rms(xf, g):
    return xf * lax.rsqrt(jnp.mean(xf * xf, axis=-1, keepdims=True) + RMS_EPS) * g


def _sigmoid(x):
    return 1.0 / (1.0 + jnp.exp(-x))


def _pack_bf16_pairs(xb):
    half = xb.shape[1] // 2
    bits = pltpu.bitcast(xb.astype(F32), jnp.uint32)
    return (bits[:, :half] & jnp.uint32(0xFFFF0000)) | (bits[:, half:] >> 16)


def _unpack_bf16_pairs(packed, dtype):
    hi = pltpu.bitcast(packed & jnp.uint32(0xFFFF0000), F32)
    lo = pltpu.bitcast(packed << 16, F32)
    return jnp.concatenate([hi, lo], axis=1).astype(dtype)


def _const_spec(shape):
    return pl.BlockSpec(shape, lambda *_: (0,) * len(shape), pipeline_mode=pl.Buffered(1))


def _in_proj_kernel(x_ref, xp_ref, xn_ref, n1_ref, w_ref, cw_ref, cos_ref, sa_ref, sb_ref,
                    q1, k1, v1, q2, k2, v2, q3, k3, v3, mix_ref, pbuf, dbuf):
    j = pl.program_id(1)
    nj = pl.num_programs(1)
    g1 = n1_ref[...]
    hf = _rms(x_ref[0], g1)
    h = hf.astype(BF16)

    def proj(hh, c0, width):
        return jnp.dot(hh, w_ref[:, c0:c0 + width], preferred_element_type=F32)

    cosv = cos_ref[...]
    sav = sa_ref[...]
    sbv = sb_ref[...]

    def rotary(z):
        return z * cosv + pltpu.roll(z, 96, 1) * sav + pltpu.roll(z, 32, 1) * sbv

    def emit(outs, c0, rot, scale):
        for g, d in enumerate(DILATIONS):
            z = proj(h, c0 + g * D_GRP, D_GRP)
            if rot:
                z = jnp.concatenate([rotary(z[:, :128]), rotary(z[:, 128:])], axis=1)
            if scale != 1.0:
                z = z * scale
            if d == 1:
                outs[g][0, 0] = z.astype(BF16)
            else:
                dbuf[0] = z[:, :128]
                dbuf[1] = z[:, 128:]
                for r in range(d):
                    for c in range(2):
                        outs[g][0, r, :, c * 128:(c + 1) * 128] = (
                            dbuf[c, pl.ds(r, TSI // d, stride=d), :].astype(BF16))

    emit((q1, q2, q3), C_Q, True, HEAD_DIM ** -0.5)
    emit((k1, k2, k3), C_K, True, 1.0)
    emit((v1, v2, v3), C_V, False, 1.0)

    hp = _rms(xp_ref[0], g1)
    hn = _rms(xn_ref[0], g1)
    he = jnp.concatenate([hp, hf, hn], axis=0).astype(BF16)
    pe = proj(he, C_CC, D_CONV) * proj(he, C_CU, D_CONV)
    erow = lax.broadcasted_iota(jnp.int32, (TSI + 16, 1), 0)
    outside = ((erow < 8) & (j == 0)) | ((erow >= 8 + TSI) & (j == nj - 1))
    pbuf[...] = jnp.where(outside, 0.0, pe)
    conv = (cw_ref[0:1, :] * pbuf[7:7 + TSI, :] + cw_ref[1:2, :] * pbuf[8:8 + TSI, :]
            + cw_ref[2:3, :] * pbuf[9:9 + TSI, :])
    mix_ref[0] = (proj(h, C_CB, D_CONV) * conv).astype(BF16)


def _in_proj(x, n1, w_qkvc, conv_w, cos_t, sa_t, sb_t):
    B, S, D = x.shape
    nj = S // TSI
    rot_spec = pl.BlockSpec((TSI, 128), lambda b, j: (j, 0))
    in_specs = [
        pl.BlockSpec((1, TSI, D), lambda b, j: (b, j, 0)),
        pl.BlockSpec((1, 8, D), lambda b, j: (b, jnp.maximum(j * (TSI // 8) - 1, 0), 0)),
        pl.BlockSpec((1, 8, D), lambda b, j: (b, jnp.minimum((j + 1) * (TSI // 8), S // 8 - 1), 0)),
        _const_spec((1, D)),
        _const_spec(w_qkvc.shape),
        _const_spec((3, D_CONV)),
        rot_spec, rot_spec, rot_spec,
    ]
    out_shape, out_specs = [], []
    for d in DILATIONS:
        for _ in range(3):
            out_shape.append(jax.ShapeDtypeStruct((B, d, S // d, D_GRP), BF16))
            out_specs.append(pl.BlockSpec((1, d, TSI // d, D_GRP), lambda b, j: (b, 0, j, 0)))
    out_shape.append(jax.ShapeDtypeStruct((B, S, D_CONV), BF16))
    out_specs.append(pl.BlockSpec((1, TSI, D_CONV), lambda b, j: (b, j, 0)))
    return pl.pallas_call(
        _in_proj_kernel,
        grid=(B, nj),
        in_specs=in_specs,
        out_specs=out_specs,
        out_shape=out_shape,
        scratch_shapes=[pltpu.VMEM((TSI + 16, D_CONV), F32), pltpu.VMEM((2, TSI, 128), F32)],
        compiler_params=pltpu.CompilerParams(
            dimension_semantics=("parallel", "parallel"), vmem_limit_bytes=VMEM_LIMIT),
        name="in_proj",
    )(x, x, x, n1, w_qkvc, conv_w, cos_t, sa_t, sb_t)


def _attn_kernel(*refs):
    ins = refs[:21]
    bias_ref = refs[21]
    o_ref = refs[22]
    kbufs = refs[23:29]
    os_ref, ls_ref = refs[29], refs[30]
    j = pl.program_id(1)
    nj = pl.num_programs(1)

    lane = lax.broadcasted_iota(jnp.int32, (QB, D_GRP), 1)
    head_of_lane = lane // HEAD_DIM

    for g, d in enumerate(DILATIONS):
        q_ref, kc, kp, kn, vc, vp, vn = ins[7 * g:7 * g + 7]
        kb, vb = kbufs[2 * g], kbufs[2 * g + 1]
        n = TQ // d
        nblk = n // QB
        for buf, prev, cur, nxt in ((kb, kp, kc, kn), (vb, vp, vc, vn)):
            buf[:, 0:HALF, :] = prev[0]
            buf[:, HALF:HALF + n, :] = cur[0]
            buf[:, HALF + n:, :] = nxt[0]

        def scores(idx2, q_ref=q_ref, kb=kb, nblk=nblk):
            idx, hp = divmod(idx2, 2)
            r, jb = divmod(idx, nblk)
            qb = q_ref[0, r, jb * QB:(jb + 1) * QB, :]
            kw = kb[r, jb * QB:jb * QB + KW, :]
            qs = jnp.concatenate(
                [jnp.where(head_of_lane == 2 * hp + a, qb, jnp.zeros_like(qb)) for a in range(2)],
                axis=0)
            s = lax.dot_general(qs, kw, (((1,), (1,)), ((), ())), preferred_element_type=F32)
            variant = 0
            if jb == 0:
                variant = jnp.where(j == 0, 1, variant)
            if jb == nblk - 1:
                variant = jnp.where(j == nj - 1, 2, variant)
            return s + bias_ref[variant]

        def softmax(s):
            m = jnp.max(s, axis=1, keepdims=True)
            p = jnp.exp(s - m)
            l = jnp.sum(p, axis=1, keepdims=True)
            return p.astype(BF16), 1.0 / l, m + jnp.log(l)

        def finish(idx2, pb, inv_l, lse, vb=vb, d=d, nblk=nblk, g=g):
            idx, hp = divmod(idx2, 2)
            r, jb = divmod(idx, nblk)
            vw = vb[r, jb * QB:jb * QB + KW, :][:, hp * 128:(hp + 1) * 128]
            pv = jnp.dot(pb, vw, preferred_element_type=F32)
            first = lax.broadcasted_iota(jnp.int32, (QB, 128), 1) < HEAD_DIM
            o = jnp.where(first, pv[0:QB] * inv_l[0:QB], pv[QB:2 * QB] * inv_l[QB:2 * QB])
            ls = jnp.where(first, lse[0:QB], lse[QB:2 * QB])
            rows = pl.ds(jb * QB, QB) if d == 1 else pl.ds(r + jb * QB * d, QB, stride=d)
            os_ref[2 * g + hp, rows, :] = o
            ls_ref[2 * g + hp, rows, :] = ls

        nb = 2 * d * nblk
        s_of, sm_of = {}, {}
        for i in range(nb + 2):
            if i >= 2:
                finish(i - 2, *sm_of.pop(i - 2))
            if i < nb:
                s_of[i] = scores(i)
            if 1 <= i <= nb:
                sm_of[i - 1] = softmax(s_of.pop(i - 1))

    def merge(c, carry):
        rows = pl.ds(pl.multiple_of(c * QB, QB), QB)
        for c in range(2):
            l0, l1, l2 = ls_ref[c, rows, :], ls_ref[2 + c, rows, :], ls_ref[4 + c, rows, :]
            mm = jnp.maximum(jnp.maximum(l0, l1), l2)
            w0, w1, w2 = jnp.exp(l0 - mm), jnp.exp(l1 - mm), jnp.exp(l2 - mm)
            o = (w0 * os_ref[c, rows, :] + w1 * os_ref[2 + c, rows, :]
                 + w2 * os_ref[4 + c, rows, :]) / (w0 + w1 + w2)
            o_ref[0, rows, c * 128:(c + 1) * 128] = o.astype(BF16)
        return carry

    lax.fori_loop(0, TQ // QB, merge, 0)


def _band_bias():
    row = jnp.arange(2 * QB)[:, None] % QB
    col = jnp.arange(KW)[None, :]
    band = (col >= row) & (col <= row + 2 * HALF)
    variants = (band, band & (col >= HALF), band & (col < KW - HALF))
    return jnp.stack([jnp.where(v, 0.0, NEG).astype(F32) for v in variants])


def _attn(qkv):
    B = qkv[0].shape[0]
    S = qkv[0].shape[2]
    nj = S // TQ
    assert nj > 1
    ins, in_specs, scratch = [], [], []
    for g, d in enumerate(DILATIONS):
        q, k, v = qkv[3 * g:3 * g + 3]
        n = TQ // d
        L = S // d
        nh = n // HALF
        cur = pl.BlockSpec((1, d, n, D_GRP), lambda b, j: (b, 0, j, 0))
        prev = pl.BlockSpec((1, d, HALF, D_GRP),
                            lambda b, j, nh=nh: (b, 0, jnp.maximum(j * nh - 1, 0), 0))
        nxt = pl.BlockSpec((1, d, HALF, D_GRP),
                           lambda b, j, nh=nh, L=L: (b, 0, jnp.minimum((j + 1) * nh, L // HALF - 1), 0))
        ins += [q, k, k, k, v, v, v]
        in_specs += [cur, cur, prev, nxt, cur, prev, nxt]
        scratch += [pltpu.VMEM((d, n + 2 * HALF, D_GRP), BF16)] * 2
    scratch += [pltpu.VMEM((6, TQ, 128), F32), pltpu.VMEM((6, TQ, 128), F32)]
    ins.append(_band_bias())
    in_specs.append(_const_spec((3, 2 * QB, KW)))
    return pl.pallas_call(
        _attn_kernel,
        grid=(B, nj),
        in_specs=in_specs,
        out_specs=pl.BlockSpec((1, TQ, D_GRP), lambda b, j: (b, j, 0)),
        out_shape=jax.ShapeDtypeStruct((B, S, D_GRP), BF16),
        scratch_shapes=scratch,
        compiler_params=pltpu.CompilerParams(
            dimension_semantics=("parallel", "parallel"), vmem_limit_bytes=VMEM_LIMIT),
        name="attn",
    )(*ins)


def _post_kernel(o_ref, mix_ref, x_ref, n1_ref, wz_ref, bg_ref, wco_ref, wao_ref, wo_ref, n2_ref,
                 wr_ref, br_ref, tri_ref, x1_ref, h2_ref, rt_ref, cnt_ref, carry):
    first = (pl.program_id(0) == 0) & (pl.program_id(1) == 0)

    @pl.when(first)
    def _():
        carry[...] = jnp.zeros_like(carry)

    x = x_ref[0]
    h = _rms(x, n1_ref[...]).astype(BF16)
    g_a = _sigmoid(jnp.dot(h, wz_ref[:, :D_MODEL], preferred_element_type=F32) + bg_ref[:, :D_MODEL])
    g_b = _sigmoid(jnp.dot(h, wz_ref[:, D_MODEL:], preferred_element_type=F32) + bg_ref[:, D_MODEL:])
    att = jnp.dot(o_ref[0], wao_ref[...], preferred_element_type=F32)
    cvb = jnp.dot(mix_ref[0], wco_ref[...], preferred_element_type=F32)
    merged = (g_a * att + g_b * cvb).astype(BF16)
    x1 = x + jnp.dot(merged, wo_ref[...], preferred_element_type=F32)
    x1_ref[0] = x1
    h2 = _rms(x1, n2_ref[...]).astype(BF16)
    h2_ref[0] = _pack_bf16_pairs(h2)

    lt = lax.dot_general(wr_ref[...], h2, (((1,), (1,)), ((), ())),
                         preferred_element_type=F32) + br_ref[:, 0:1]
    grow = lax.broadcasted_iota(jnp.int32, (8, TS), 0)
    gl = jnp.where(grow < N_EXPERT_GROUPS, lt[0:8], NEG)
    gmax = jnp.max(gl, axis=0, keepdims=True)
    grp = jnp.min(jnp.where(gl == gmax, grow, 8), axis=0, keepdims=True)
    grp_w = 1.0 / jnp.sum(jnp.exp(gl - gmax), axis=0, keepdims=True)
    erow = lax.broadcasted_iota(jnp.int32, (N_EXPERTS, TS), 0)
    el = jnp.where(erow // EXPERTS_PER_GROUP == grp, lt[8:8 + N_EXPERTS], NEG)
    v1 = jnp.max(el, axis=0, keepdims=True)
    i1 = jnp.min(jnp.where(el == v1, erow, N_EXPERTS), axis=0, keepdims=True)
    el2 = jnp.where(erow == i1, NEG, el)
    v2 = jnp.max(el2, axis=0, keepdims=True)
    i2 = jnp.min(jnp.where(el2 == v2, erow, N_EXPERTS), axis=0, keepdims=True)
    t = jnp.exp(v2 - v1)
    den = 1.0 + t
    gate1 = grp_w * (1.0 / den)
    gate2 = grp_w * (t / den)

    oh1 = (erow == i1).astype(F32)
    oh2 = (erow == i2).astype(F32)
    both = oh1 + oh2
    bothb = both.astype(BF16)
    offset = carry[:, 0:1]
    chunks = []
    for c in range(TS // CUM_CHUNK):
        bc = bothb[:, c * CUM_CHUNK:(c + 1) * CUM_CHUNK]
        chunks.append(offset + jnp.dot(bc, tri_ref[...], preferred_element_type=F32))
        offset = offset + jnp.sum(both[:, c * CUM_CHUNK:(c + 1) * CUM_CHUNK], axis=1, keepdims=True)
    basec = jnp.concatenate(chunks, axis=1)
    rank1 = jnp.sum(oh1 * basec, axis=0, keepdims=True)
    rank2 = jnp.sum(oh2 * basec, axis=0, keepdims=True)
    newc = carry[...] + (offset - carry[:, 0:1])
    carry[...] = newc
    cnt_ref[...] = newc
    zero = jnp.zeros((1, TS), F32)
    rt_ref[0] = jnp.concatenate(
        [i1.astype(F32), i2.astype(F32), gate1, gate2, rank1, rank2, zero, zero], axis=0)


def _post(o_att, mix, x, p, b0, nseq):
    _, S, D = x.shape
    nj = S // TS
    src = lambda w: pl.BlockSpec((1, TS, w), lambda b, j: (b + b0, j, 0))
    tile = lambda w: pl.BlockSpec((1, TS, w), lambda b, j: (b, j, 0))
    return pl.pallas_call(
        _post_kernel,
        grid=(nseq, nj),
        in_specs=[src(D_GRP), src(D_CONV), src(D),
                  _const_spec((1, D)), _const_spec((D, 2 * D)), _const_spec((1, 2 * D)),
                  _const_spec((D_CONV, D)), _const_spec((D_GRP, D)), _const_spec((D, D)),
                  _const_spec((1, D)), _const_spec((ROUTER_ROWS, D)), _const_spec((ROUTER_ROWS, 128)),
                  _const_spec((CUM_CHUNK, CUM_CHUNK))],
        out_specs=[tile(D), tile(D // 2),
                   pl.BlockSpec((1, 8, TS), lambda b, j: (b, 0, j)),
                   pl.BlockSpec((N_EXPERTS, 128), lambda b, j: (0, 0))],
        out_shape=[jax.ShapeDtypeStruct((nseq, S, D), F32), jax.ShapeDtypeStruct((nseq, S, D // 2), jnp.uint32),
                   jax.ShapeDtypeStruct((nseq, 8, S), F32),
                   jax.ShapeDtypeStruct((N_EXPERTS, 128), F32)],
        scratch_shapes=[pltpu.VMEM((N_EXPERTS, 128), F32)],
        compiler_params=pltpu.CompilerParams(
            dimension_semantics=("arbitrary", "arbitrary"), vmem_limit_bytes=VMEM_LIMIT),
        name="post",
    )(o_att, mix, x, p["n1"], p["w_gate"], p["b_gate"], p["w_co"], p["w_ao"], p["w_o"], p["n2"],
      p["wr"], p["br"], p["tri"])


def _moe_kernel(be_ref, nu_ref, nv_ref, nx_ref, x_ref, wg_hbm, wu_hbm, wd_hbm, out_ref,
                wgf, wuf, wdf, wgb, wub, wdb, sem, nchg):
    b = pl.program_id(0)

    def weight_copies(e, slot):
        return (pltpu.make_async_copy(wg_hbm.at[e], wgf.at[slot], sem.at[slot]),
                pltpu.make_async_copy(wu_hbm.at[e], wuf.at[slot], sem.at[slot]),
                pltpu.make_async_copy(wd_hbm.at[e], wdf.at[slot], sem.at[slot]))

    @pl.when(b == 0)
    def _():
        nchg[0] = 0
        for cp in weight_copies(be_ref[0], 0):
            cp.start()

    @pl.when(b < nu_ref[0])
    def _():
        @pl.when((b == 0) | (be_ref[b] != be_ref[jnp.maximum(b - 1, 0)]))
        def _():
            slot = nchg[0] % 2
            for cp in weight_copies(be_ref[b], slot):
                cp.wait()
            wgb[...] = wgf[slot].astype(BF16)
            wub[...] = wuf[slot].astype(BF16)
            wdb[...] = wdf[slot].astype(BF16)

            @pl.when(nx_ref[b] != be_ref[b])
            def _():
                for cp in weight_copies(nx_ref[b], 1 - slot):
                    cp.start()

            nchg[0] = nchg[0] + 1

        live = lax.broadcasted_iota(jnp.int32, (MOE_BLK, 1), 0) < nv_ref[b]
        x = _unpack_bf16_pairs(jnp.where(live, x_ref[...], jnp.uint32(0)), BF16)
        a = jnp.dot(x, wgb[...], preferred_element_type=F32)
        u = jnp.dot(x, wub[...], preferred_element_type=F32)
        hm = (a * _sigmoid(a) * u).astype(BF16)
        out_ref[...] = _pack_bf16_pairs(jnp.dot(hm, wdb[...], preferred_element_type=F32).astype(BF16))

    @pl.when(b >= nu_ref[0])
    def _():
        out_ref[...] = jnp.zeros_like(out_ref)


def _moe(block_e, nused, nvalid, next_e, xs, w_g, w_u, w_d):
    nb = block_e.shape[0]
    D = 2 * xs.shape[1]
    grid_spec = pltpu.PrefetchScalarGridSpec(
        num_scalar_prefetch=4,
        grid=(nb,),
        in_specs=[
            pl.BlockSpec((MOE_BLK, D // 2), lambda b, *_: (b, 0)),
            pl.BlockSpec(memory_space=pl.ANY),
            pl.BlockSpec(memory_space=pl.ANY),
            pl.BlockSpec(memory_space=pl.ANY),
        ],
        out_specs=pl.BlockSpec((MOE_BLK, D // 2), lambda b, *_: (b, 0)),
        scratch_shapes=[pltpu.VMEM((2, D, D_EXPERT), F32), pltpu.VMEM((2, D, D_EXPERT), F32),
                        pltpu.VMEM((2, D_EXPERT, D), F32),
                        pltpu.VMEM((D, D_EXPERT), BF16), pltpu.VMEM((D, D_EXPERT), BF16),
                        pltpu.VMEM((D_EXPERT, D), BF16),
                        pltpu.SemaphoreType.DMA((2,)), pltpu.SMEM((1,), jnp.int32)],
    )
    return pl.pallas_call(
        _moe_kernel,
        grid_spec=grid_spec,
        out_shape=jax.ShapeDtypeStruct((nb * MOE_BLK, D // 2), jnp.uint32),
        compiler_params=pltpu.CompilerParams(
            dimension_semantics=("arbitrary",), vmem_limit_bytes=VMEM_LIMIT),
        name="moe",
    )(block_e, nused, nvalid, next_e, xs, w_g, w_u, w_d)


def _sc_mesh():
    return plsc.VectorSubcoreMesh(core_axis_name="c", subcore_axis_name="s")


def _sc_worker_range(total):
    info = plsc.get_sparse_core_info()
    nw = info.num_cores * info.num_subcores
    wid = lax.axis_index("s") * info.num_cores + lax.axis_index("c")
    per_w = total // nw
    assert per_w * nw == total and per_w % SC_ROWS == 0
    return wid * per_w, per_w // SC_ROWS


def _sc_scatter_rows(src, idx_a, idx_b, nslots):
    T, D = src.shape

    @functools.partial(
        pl.kernel, out_type=jax.ShapeDtypeStruct((nslots, D), src.dtype), mesh=_sc_mesh(),
        scratch_types=[pltpu.VMEM((SC_ROWS,), jnp.int32), pltpu.VMEM((SC_ROWS,), jnp.int32),
                       pltpu.VMEM((SC_ROWS, D), src.dtype), pltpu.SemaphoreType.DMA],
        name="sc_dispatch")
    def k(x_hbm, ia_hbm, ib_hbm, o_hbm, ia_v, ib_v, rows_v, sem):
        start, nchunks = _sc_worker_range(T)

        @pl.loop(0, nchunks)
        def _(c):
            rows = pl.ds(start + c * SC_ROWS, SC_ROWS)
            pltpu.sync_copy(ia_hbm.at[rows], ia_v)
            pltpu.sync_copy(ib_hbm.at[rows], ib_v)
            pltpu.sync_copy(x_hbm.at[rows], rows_v)
            ca = pltpu.async_copy(rows_v, o_hbm.at[ia_v], sem)
            cb = pltpu.async_copy(rows_v, o_hbm.at[ib_v], sem)
            ca.wait()
            cb.wait()

    return k(src, idx_a, idx_b)


def _sc_gather_rows(src, idx_a, idx_b):
    D = src.shape[1]
    T = idx_a.shape[0]
    out = jax.ShapeDtypeStruct((T, D), src.dtype)

    @functools.partial(
        pl.kernel, out_type=(out, out), mesh=_sc_mesh(),
        scratch_types=[pltpu.VMEM((SC_ROWS,), jnp.int32), pltpu.VMEM((SC_ROWS,), jnp.int32),
                       pltpu.VMEM((SC_ROWS, D), src.dtype), pltpu.VMEM((SC_ROWS, D), src.dtype),
                       pltpu.SemaphoreType.DMA],
        name="sc_combine")
    def k(x_hbm, ia_hbm, ib_hbm, oa_hbm, ob_hbm, ia_v, ib_v, ra_v, rb_v, sem):
        start, nchunks = _sc_worker_range(T)

        @pl.loop(0, nchunks)
        def _(c):
            rows = pl.ds(start + c * SC_ROWS, SC_ROWS)
            pltpu.sync_copy(ia_hbm.at[rows], ia_v)
            pltpu.sync_copy(ib_hbm.at[rows], ib_v)
            ca = pltpu.async_copy(x_hbm.at[ia_v], ra_v, sem)
            cb = pltpu.async_copy(x_hbm.at[ib_v], rb_v, sem)
            ca.wait()
            cb.wait()
            pltpu.sync_copy(ra_v, oa_hbm.at[rows])
            pltpu.sync_copy(rb_v, ob_hbm.at[rows])

    return k(src, idx_a, idx_b)


def _combine_kernel(x1_ref, ya_ref, yb_ref, rt_ref, gf_ref, out_ref):
    gt = jnp.transpose(rt_ref[0])
    y = gt[:, 2:3] * _unpack_bf16_pairs(ya_ref[...], F32) + gt[:, 3:4] * _unpack_bf16_pairs(yb_ref[...], F32)
    out_ref[...] = _rms(x1_ref[...] + y, gf_ref[...])


def _combine(x1, ya, yb, rt, gf, out, row0, total_rows):
    T, D = x1.shape
    per_seq = rt.shape[2] // TS
    tile = pl.BlockSpec((TS, D), lambda i: (i, 0))

    def body(x1_ref, ya_ref, yb_ref, rt_ref, gf_ref, *rest):
        _combine_kernel(x1_ref, ya_ref, yb_ref, rt_ref, gf_ref, rest[-1])

    half = pl.BlockSpec((TS, D // 2), lambda i: (i, 0))
    in_specs = [tile, half, half,
                pl.BlockSpec((1, 8, TS), lambda i: (i // per_seq, 0, i % per_seq)),
                _const_spec((1, D))]
    args = [x1, ya, yb, rt, gf]
    aliases = {}
    if out is not None:
        in_specs.append(pl.BlockSpec(memory_space=pl.ANY))
        args.append(out)
        aliases = {5: 0}
    return pl.pallas_call(
        body,
        grid=(T // TS,),
        in_specs=in_specs,
        out_specs=pl.BlockSpec((TS, D), lambda i: (i + row0 // TS, 0)),
        out_shape=jax.ShapeDtypeStruct((total_rows, D), F32),
        input_output_aliases=aliases,
        compiler_params=pltpu.CompilerParams(
            dimension_semantics=("parallel",), vmem_limit_bytes=VMEM_LIMIT),
        name="combine",
    )(*args)


def _slots_kernel(ps_ref, rt_ref, d_ref):
    v = rt_ref[0]
    start = jnp.zeros(v.shape, F32)
    for e in range(N_EXPERTS):
        start = jnp.where(v == float(e), ps_ref[e].astype(F32), start)
    d_ref[0] = (start[0:2] + v[4:6]).astype(jnp.int32)


def _slots(pstarts, rt):
    B, _, S = rt.shape
    grid_spec = pltpu.PrefetchScalarGridSpec(
        num_scalar_prefetch=1,
        grid=(B,),
        in_specs=[pl.BlockSpec((1, 8, S), lambda b, ps: (b, 0, 0))],
        out_specs=pl.BlockSpec((1, 2, S), lambda b, ps: (b, 0, 0)),
    )
    return pl.pallas_call(
        _slots_kernel,
        grid_spec=grid_spec,
        out_shape=jax.ShapeDtypeStruct((B, 2, S), jnp.int32),
        compiler_params=pltpu.CompilerParams(dimension_semantics=("parallel",)),
        name="slots",
    )(pstarts, rt)


def _rotary_tables(S):
    inv_freq = 1.0 / (ROPE_THETA ** (jnp.arange(0, HEAD_DIM, 2, dtype=F32) / HEAD_DIM))
    ang = jnp.arange(S, dtype=F32)[:, None] * inv_freq[None, :]
    cos, sin = jnp.cos(ang), jnp.sin(ang)
    z = jnp.zeros_like(sin)
    cos_t = jnp.concatenate([cos, cos, cos, cos], axis=1)
    sa_t = jnp.concatenate([-sin, z, -sin, z], axis=1)
    sb_t = jnp.concatenate([z, sin, z, sin], axis=1)
    return cos_t, sa_t, sb_t


def _moe_segment(o_att, mix, x, p, b0, nseq, out):
    _, S, D = x.shape
    T = nseq * S
    x1, h2, rt, cnt = _post(o_att, mix, x, p, b0, nseq)

    counts = cnt[:, 0].astype(jnp.int32)
    pcounts = (counts + MOE_BLK - 1) // MOE_BLK * MOE_BLK
    pends = jnp.cumsum(pcounts)
    pstarts = pends - pcounts
    dest = _slots(pstarts, rt)
    d1 = dest[:, 0, :].reshape(T)
    d2 = dest[:, 1, :].reshape(T)
    nb = (2 * T) // MOE_BLK + N_EXPERTS
    block_start = jnp.arange(nb, dtype=jnp.int32) * MOE_BLK
    block_e = jnp.minimum(jnp.sum((pends[None, :] <= block_start[:, None]).astype(jnp.int32), axis=1),
                          N_EXPERTS - 1)
    nused = (pends[-1:] // MOE_BLK).astype(jnp.int32)
    mine = block_e[:, None] == jnp.arange(N_EXPERTS, dtype=jnp.int32)[None, :]
    valid_end = jnp.sum(jnp.where(mine, (pstarts + counts)[None, :], 0), axis=1)
    nvalid = jnp.clip(valid_end - block_start, 0, MOE_BLK)

    xs = _sc_scatter_rows(h2.reshape(T, D // 2), d1, d2, nb * MOE_BLK)
    experts = jnp.arange(N_EXPERTS, dtype=jnp.int32)
    later = (experts[None, :] > block_e[:, None]) & (counts[None, :] > 0)
    next_e = jnp.min(jnp.where(later, experts[None, :], N_EXPERTS), axis=1)
    next_e = jnp.where(next_e == N_EXPERTS, block_e, next_e)
    yb = _moe(block_e, nused, nvalid, next_e, xs, p["w_g"], p["w_u"], p["w_d"])
    ya, yc = _sc_gather_rows(yb, d1, d2)
    return _combine(x1.reshape(T, D), ya, yc, rt, p["gf"], out, b0 * S, x.shape[0] * S)


def _trunk(x, p):
    B, S, D = x.shape
    outs = _in_proj(x, p["n1"], p["w_qkvc"], p["conv_w"], *p["rot"])
    o_att = _attn(outs[:9])
    nseg = max(1, B // SEG_SEQS)
    out = None
    for seg in range(nseg):
        out = _moe_segment(o_att, outs[9], x, p, seg * (B // nseg), B // nseg, out)
    return out.reshape(B, S, D)


def kernel(x_prompt, x_sample, norm1_g, w_in, b_gate, conv_w, w_attn_out, w_conv_out, w_out, norm2_g,
           w_router_group, b_router_group, w_router_expert, b_router_expert, w_exp_gate, w_exp_up,
           w_exp_down, norm_f_g):
    assert norm1_g.shape[0] == 1, "single-layer trunk"
    S = x_prompt.shape[1]
    wr = jnp.zeros((ROUTER_ROWS, D_MODEL), F32)
    wr = wr.at[0:N_EXPERT_GROUPS].set(w_router_group[0].T).at[8:8 + N_EXPERTS].set(w_router_expert[0].T)
    br = jnp.zeros((ROUTER_ROWS,), F32)
    br = br.at[0:N_EXPERT_GROUPS].set(b_router_group[0]).at[8:8 + N_EXPERTS].set(b_router_expert[0])
    ti = jnp.arange(CUM_CHUNK)
    p = dict(
        n1=norm1_g, w_qkvc=w_in[0, :, :C_ZA].astype(BF16), w_gate=w_in[0, :, C_ZA:].astype(BF16),
        b_gate=b_gate, conv_w=conv_w[0], w_co=w_conv_out[0].astype(BF16), rot=_rotary_tables(S),
        w_ao=w_attn_out[0].astype(BF16), w_o=w_out[0].astype(BF16), n2=norm2_g,
        wr=wr.astype(BF16), br=jnp.broadcast_to(br[:, None], (ROUTER_ROWS, 128)),
        tri=(ti[:, None] < ti[None, :]).astype(BF16),
        w_g=w_exp_gate[0], w_u=w_exp_up[0], w_d=w_exp_down[0],
        gf=norm_f_g.reshape(1, D_MODEL),
    )
    return _trunk(x_prompt, p), _trunk(x_sample, p)
```
